```python
import math
import jax, jax.numpy as jnp
from jax import lax
import numpy as np

D_MODEL = 2048
BATCH = 4
SEQ = 2048
DEPTH = 2
DEC_BATCH = 128
DEC_SEQ = 1
PAST_LEN = 16384
PAGE_SIZE = 128

GROUP_WIDTH = D_MODEL // 4
MIX_WIDTH = 4 * GROUP_WIDTH
N_GROUP_HEADS = 4
HEAD_DIM = GROUP_WIDTH // N_GROUP_HEADS
GLA_DK = HEAD_DIM // 2
GLA_QK = N_GROUP_HEADS * GLA_DK
GLA_LOWRANK = 16
GLA_TAU = 16.0
GLA_CHUNK = 64
SC_WIDTH = 3
CF_WIDTH = 31
SG_CHUNK = 128
D_FF = (8 * D_MODEL + 3 * 256 - 1) // (3 * 256) * 256
EPS = 1e-6
IN_SIZES = (GROUP_WIDTH, GROUP_WIDTH, GROUP_WIDTH,
            GLA_QK, GLA_QK, GROUP_WIDTH, GROUP_WIDTH, GLA_LOWRANK,
            2 * GROUP_WIDTH, 2 * GROUP_WIDTH)
N_IN = sum(IN_SIZES)

kernel_name = 'hybrid_conv_gla_conformer_sgu_step'


def _split_points():
    pts, acc = [], 0
    for s in IN_SIZES[:-1]:
        acc += s
        pts.append(acc)
    return pts


def rmsnorm(x, g):
    xf = x.astype(jnp.float32)
    y = xf * lax.rsqrt(jnp.mean(xf * xf, axis=-1, keepdims=True) + EPS) * g.astype(jnp.float32)
    return y.astype(x.dtype)


def layernorm(x, g, b):
    xf = x.astype(jnp.float32)
    mu = jnp.mean(xf, axis=-1, keepdims=True)
    xc = xf - mu
    y = xc * lax.rsqrt(jnp.mean(xc * xc, axis=-1, keepdims=True) + EPS)
    return (y * g.astype(jnp.float32) + b.astype(jnp.float32)).astype(x.dtype)


def causal_dwconv(u, past, w):
    ext = jnp.concatenate([past.astype(u.dtype), u], axis=1)
    y = lax.conv_general_dilated(ext, w[:, None, :].astype(u.dtype), window_strides=(1,),
                                 padding='VALID', dimension_numbers=('NWC', 'WIO', 'NWC'),
                                 feature_group_count=u.shape[-1])
    return y, ext[:, -(w.shape[0] - 1):]


def gla_chunked(q, k, v, log_a, s0, chunk):
    B, L, H, _ = q.shape
    n = L // chunk

    def to_chunks(t):
        return t.reshape(B, n, chunk, H, t.shape[-1]).transpose(1, 0, 3, 2, 4)

    qc, kc, vc, ac = (to_chunks(t) for t in (q, k, v, log_a))
    mask = jnp.tril(jnp.ones((chunk, chunk), dtype=bool))[:, :, None]

    def step(S, inp):
        qi, ki, vi, ai = inp
        b = jnp.cumsum(ai, axis=2)
        diff = b[:, :, :, None, :] - b[:, :, None, :, :]
        decay = jnp.exp(jnp.where(mask, diff, -jnp.inf))
        att = jnp.einsum('bhid,bhjd,bhijd->bhij', qi, ki, decay)
        o = jnp.einsum('bhij,bhjv->bhiv', att, vi) + jnp.einsum('bhid,bhdv->bhiv', qi * jnp.exp(b), S)
        b_last = b[:, :, -1:, :]
        S = jnp.exp(b_last[:, :, 0, :])[..., None] * S + jnp.einsum('bhjd,bhjv->bhdv', ki * jnp.exp(b_last - b), vi)
        return S, o

    S, o = lax.scan(step, s0, (qc, kc, vc, ac))
    o = o.transpose(1, 0, 3, 2, 4).reshape(B, L, H, v.shape[-1])
    return o, S


def mixer_groups(h, past_a, s0, past_c, w_in, conv_a_w, gla_a2, gla_a_bias, gla_norm,
                 conv_c_w, conv_c_b, ln_c_g, ln_c_b, ln_d_g, ln_d_b, sg_w, sg_b, w_o):
    B, L, _ = h.shape
    f32 = jnp.float32
    z = h @ w_in
    a_x, a_b, a_c, g_q, g_k, g_v, g_r, g_lr, c_glu, d_uv = jnp.split(z, _split_points(), axis=-1)

    ya, new_a = causal_dwconv(a_c * a_x, past_a, conv_a_w)
    ya = a_b * ya

    q = (g_q.reshape(B, L, N_GROUP_HEADS, GLA_DK) * (GLA_DK ** -0.5)).astype(f32)
    k = g_k.reshape(B, L, N_GROUP_HEADS, GLA_DK).astype(f32)
    v = g_v.reshape(B, L, N_GROUP_HEADS, HEAD_DIM).astype(f32)
    log_a = jax.nn.log_sigmoid((g_lr @ gla_a2 + gla_a_bias).astype(f32)) / GLA_TAU
    log_a = log_a.reshape(B, L, N_GROUP_HEADS, GLA_DK)
    o, S = gla_chunked(q, k, v, log_a, s0.astype(f32), math.gcd(L, GLA_CHUNK))
    o = o * lax.rsqrt(jnp.mean(o * o, axis=-1, keepdims=True) + EPS)
    o = o * gla_norm.reshape(N_GROUP_HEADS, HEAD_DIM).astype(f32)
    yb = o.reshape(B, L, GROUP_WIDTH).astype(h.dtype) * jax.nn.silu(g_r)

    c_in = c_glu[..., :GROUP_WIDTH] * jax.nn.sigmoid(c_glu[..., GROUP_WIDTH:])
    cc, new_c = causal_dwconv(c_in, past_c, conv_c_w)
    yc = jax.nn.silu(layernorm(cc + conv_c_b, ln_c_g, ln_c_b))

    d = jax.nn.gelu(d_uv)
    u, vd = d[..., :GROUP_WIDTH], d[..., GROUP_WIDTH:]
    vd = layernorm(vd, ln_d_g, ln_d_b)
    c = min(L, SG_CHUNK)
    vr = vd.reshape(B, L // c, c, N_GROUP_HEADS, HEAD_DIM)
    ws = jnp.where(jnp.tril(jnp.ones((c, c), dtype=bool))[None], sg_w[:, :c, :c], 0.0)
    sv = jnp.einsum('hij,bnjhd->bnihd', ws.astype(vr.dtype), vr) + sg_b[:, :c].T[:, :, None]
    yd = u * sv.reshape(B, L, GROUP_WIDTH)

    y = jnp.concatenate([ya, yb, yc, yd], axis=-1) @ w_o
    return y, new_a, S, new_c, vd[:, L - c:]


def swiglu(x, w_gate, w_up, w_down):
    return (jax.nn.silu(x @ w_gate) * (x @ w_up)) @ w_down


def setup_inputs(seed: int = 0) -> dict:
    key = jax.random.key(seed)
    ks = jax.random.split(key, 32)
    nrm = jax.random.normal
    f32 = jnp.float32
    G, H = GROUP_WIDTH, N_GROUP_HEADS
    return {
        'x_prompt': nrm(ks[0], (BATCH, SEQ, D_MODEL), f32),
        'x_sample': nrm(ks[1], (DEC_BATCH, DEC_SEQ, D_MODEL), f32),
        'state_conv_a': nrm(ks[2], (DEPTH, DEC_BATCH, SC_WIDTH - 1, G), f32),
        'state_gla': 0.5 * nrm(ks[3], (DEPTH, DEC_BATCH, H, GLA_DK, HEAD_DIM), f32),
        'state_conv_c': nrm(ks[4], (DEPTH, DEC_BATCH, CF_WIDTH - 1, G), f32),
        'norm_mix': 1.0 + 0.01 * nrm(ks[5], (DEPTH, D_MODEL), f32),
        'w_in': nrm(ks[6], (DEPTH, D_MODEL, N_IN), f32) * D_MODEL ** -0.5,
        'conv_a_w': nrm(ks[7], (DEPTH, SC_WIDTH, G), f32) * SC_WIDTH ** -0.5,
        'gla_a2': nrm(ks[8], (DEPTH, GLA_LOWRANK, GLA_QK), f32) * GLA_LOWRANK ** -0.5,
        'gla_a_bias': 0.1 * nrm(ks[9], (DEPTH, GLA_QK), f32),
        'gla_norm': 1.0 + 0.01 * nrm(ks[10], (DEPTH, G), f32),
        'conv_c_w': nrm(ks[11], (DEPTH, CF_WIDTH, G), f32) * CF_WIDTH ** -0.5,
        'conv_c_b': 0.02 * nrm(ks[12], (DEPTH, G), f32),
        'ln_c_g': 1.0 + 0.01 * nrm(ks[13], (DEPTH, G), f32),
        'ln_c_b': 0.02 * nrm(ks[14], (DEPTH, G), f32),
        'ln_d_g': 1.0 + 0.01 * nrm(ks[15], (DEPTH, G), f32),
        'ln_d_b': 0.02 * nrm(ks[16], (DEPTH, G), f32),
        'sg_w': nrm(ks[17], (DEPTH, H, SG_CHUNK, SG_CHUNK), f32) * SG_CHUNK ** -0.5,
        'sg_b': 1.0 + 0.1 * nrm(ks[18], (DEPTH, H, SG_CHUNK), f32),
        'w_o': nrm(ks[19], (DEPTH, MIX_WIDTH, D_MODEL), f32) * MIX_WIDTH ** -0.5,
        'norm_ffn': 1.0 + 0.01 * nrm(ks[20], (DEPTH, D_MODEL), f32),
        'w_gate': nrm(ks[21], (DEPTH, D_MODEL, D_FF), f32) * D_MODEL ** -0.5,
        'w_up': nrm(ks[22], (DEPTH, D_MODEL, D_FF), f32) * D_MODEL ** -0.5,
        'w_down': nrm(ks[23], (DEPTH, D_FF, D_MODEL), f32) * D_FF ** -0.5,
        'norm_final': 1.0 + 0.01 * nrm(ks[24], (D_MODEL,), f32),
    }


def reference(x_prompt, x_sample, state_conv_a, state_gla, state_conv_c,
              norm_mix, w_in, conv_a_w, gla_a2, gla_a_bias, gla_norm,
              conv_c_w, conv_c_b, ln_c_g, ln_c_b, ln_d_g, ln_d_b, sg_w, sg_b, w_o,
              norm_ffn, w_gate, w_up, w_down, norm_final):

    def trunk(x, pa, ps, pc):
        na, ns, nc, nv = [], [], [], []
        for l in range(DEPTH):
            h = rmsnorm(x, norm_mix[l])
            y, a_new, s_new, c_new, v_new = mixer_groups(
                h, pa[l], ps[l], pc[l], w_in[l], conv_a_w[l], gla_a2[l], gla_a_bias[l], gla_norm[l],
                conv_c_w[l], conv_c_b[l], ln_c_g[l], ln_c_b[l], ln_d_g[l], ln_d_b[l], sg_w[l], sg_b[l], w_o[l])
            x = x + y
            x = x + swiglu(rmsnorm(x, norm_ffn[l]), w_gate[l], w_up[l], w_down[l])
            na.append(a_new.astype(state_conv_a.dtype))
            ns.append(s_new.astype(state_gla.dtype))
            nc.append(c_new.astype(state_conv_c.dtype))
            nv.append(v_new)
        return rmsnorm(x, norm_final), jnp.stack(na), jnp.stack(ns), jnp.stack(nc), jnp.stack(nv)

    zp_a = [jnp.zeros((BATCH, SC_WIDTH - 1, GROUP_WIDTH), x_prompt.dtype)] * DEPTH
    zp_s = [jnp.zeros((BATCH, N_GROUP_HEADS, GLA_DK, HEAD_DIM), jnp.float32)] * DEPTH
    zp_c = [jnp.zeros((BATCH, CF_WIDTH - 1, GROUP_WIDTH), x_prompt.dtype)] * DEPTH
    y_prompt, conv_a_prompt, gla_prompt, conv_c_prompt, sg_v_prompt = trunk(x_prompt, zp_a, zp_s, zp_c)

    y_sample, conv_a_sample, gla_sample, conv_c_sample, sg_v_sample = trunk(
        x_sample,
        [state_conv_a[l] for l in range(DEPTH)],
        [state_gla[l] for l in range(DEPTH)],
        [state_conv_c[l] for l in range(DEPTH)])

    return (y_prompt, y_sample, conv_a_prompt, conv_a_sample, gla_prompt, gla_sample,
            conv_c_prompt, conv_c_sample, sg_v_prompt, sg_v_sample)
```

```python
import functools
import math

import jax
import jax.numpy as jnp
from jax import lax
from jax.experimental import pallas as pl
from jax.experimental.pallas import tpu as pltpu

F32 = jnp.float32
BF16 = jnp.bfloat16
EPS = 1e-6

N_HEADS = 4
GLA_LOWRANK = 16
GLA_TAU = 16.0
SC_WIDTH = 3
CF_WIDTH = 31
SG_CHUNK = 128
GLA_CHUNK = 64
GLA_SUB = 16
LANES = 128
VMEM_LIMIT = 56 * 1024 * 1024


def _dot(a, b):
    return jnp.dot(a, b, preferred_element_type=F32)


def _dot_nt(a, b):
    return lax.dot_general(a, b, (((1,), (1,)), ((), ())), preferred_element_type=F32)


def _dot_tn(a, b):
    return lax.dot_general(a, b, (((0,), (0,)), ((), ())), preferred_element_type=F32)


def _split_bf16(x):
    hi = x.astype(BF16)
    lo = (x - hi.astype(F32)).astype(BF16)
    return hi, lo


def _rmsnorm(x, g):
    return x * lax.rsqrt(jnp.mean(x * x, axis=-1, keepdims=True) + EPS) * g


def _layernorm(x, g, b):
    mu = jnp.mean(x, axis=-1, keepdims=True)
    xc = x - mu
    return xc * lax.rsqrt(jnp.mean(xc * xc, axis=-1, keepdims=True) + EPS) * g + b


def _sigmoid(x):
    return 1.0 / (1.0 + jnp.exp(-x))


def _silu(x):
    return x * _sigmoid(x)


def _gelu_tanh(x):
    c = math.sqrt(2.0 / math.pi)
    return 0.5 * x * (1.0 + jnp.tanh(c * (x + 0.044715 * (x * x * x))))


def _log_sigmoid(x):
    return jnp.minimum(x, 0.0) - jnp.log(1.0 + jnp.exp(-jnp.abs(x)))


def _norm_proj_kernel(x_ref, g_ref, w_ref, wlr_ref, z_ref, zlr_ref, h_ref):
    @pl.when(pl.program_id(1) == 0)
    def _():
        h = _rmsnorm(x_ref[...], g_ref[...]).astype(BF16)
        h_ref[...] = h
        zlr_ref[...] = _dot(h, wlr_ref[...])

    z_ref[...] = _dot(h_ref[...], w_ref[...])


def _norm_proj(x, g, w, wlr, tm, tn):
    m, d = x.shape
    n = w.shape[1]
    return pl.pallas_call(
        _norm_proj_kernel,
        grid=(m // tm, n // tn),
        in_specs=[
            pl.BlockSpec((tm, d), lambda i, j: (i, 0)),
            pl.BlockSpec((1, d), lambda i, j: (0, 0)),
            pl.BlockSpec((d, tn), lambda i, j: (0, j)),
            pl.BlockSpec((d, LANES), lambda i, j: (0, 0)),
        ],
        out_specs=[
            pl.BlockSpec((tm, tn), lambda i, j: (i, j)),
            pl.BlockSpec((tm, LANES), lambda i, j: (i, 0)),
        ],
        out_shape=[jax.ShapeDtypeStruct((m, n), F32), jax.ShapeDtypeStruct((m, LANES), F32)],
        scratch_shapes=[pltpu.VMEM((tm, d), BF16)],
        compiler_params=pltpu.CompilerParams(
            dimension_semantics=("arbitrary", "arbitrary"), vmem_limit_bytes=VMEM_LIMIT),
    )(x, g, w, wlr)


def _proj_res_kernel(y_ref, w_ref, x_ref, o_ref):
    o_ref[...] = x_ref[...] + _dot(y_ref[...], w_ref[...].astype(BF16))


def _proj_res(y, w, x, tm, tn):
    m, k = y.shape
    n = w.shape[1]
    return pl.pallas_call(
        _proj_res_kernel,
        grid=(m // tm, n // tn),
        in_specs=[
            pl.BlockSpec((tm, k), lambda i, j: (i, 0)),
            pl.BlockSpec((k, tn), lambda i, j: (0, j)),
            pl.BlockSpec((tm, tn), lambda i, j: (i, j)),
        ],
        out_specs=pl.BlockSpec((tm, tn), lambda i, j: (i, j)),
        out_shape=jax.ShapeDtypeStruct((m, n), F32),
        compiler_params=pltpu.CompilerParams(
            dimension_semantics=("arbitrary", "arbitrary"), vmem_limit_bytes=VMEM_LIMIT),
    )(y, w, x)


def _ffn_up_kernel(x_ref, g_ref, wg_ref, wu_ref, o_ref, h_ref):
    @pl.when(pl.program_id(1) == 0)
    def _():
        h_ref[...] = _rmsnorm(x_ref[...], g_ref[...]).astype(BF16)

    h = h_ref[...]
    a = _dot(h, wg_ref[...].astype(BF16))
    b = _dot(h, wu_ref[...].astype(BF16))
    o_ref[...] = (_silu(a) * b).astype(BF16)


def _ffn_up(x, g, wg, wu, tm, tf):
    m, d = x.shape
    f = wg.shape[1]
    return pl.pallas_call(
        _ffn_up_kernel,
        grid=(m // tm, f // tf),
        in_specs=[
            pl.BlockSpec((tm, d), lambda i, j: (i, 0)),
            pl.BlockSpec((1, d), lambda i, j: (0, 0)),
            pl.BlockSpec((d, tf), lambda i, j: (0, j)),
            pl.BlockSpec((d, tf), lambda i, j: (0, j)),
        ],
        out_specs=pl.BlockSpec((tm, tf), lambda i, j: (i, j)),
        out_shape=jax.ShapeDtypeStruct((m, f), BF16),
        scratch_shapes=[pltpu.VMEM((tm, d), BF16)],
        compiler_params=pltpu.CompilerParams(
            dimension_semantics=("arbitrary", "arbitrary"), vmem_limit_bytes=VMEM_LIMIT),
    )(x, g, wg, wu)


def _ffn_down_kernel(a_ref, w_ref, x_ref, g_ref, o_ref, *, final_norm):
    k = pl.program_id(1)

    @pl.when(k == 0)
    def _():
        o_ref[...] = x_ref[...]

    o_ref[...] += _dot(a_ref[...], w_ref[...].astype(BF16))

    if final_norm:
        @pl.when(k == pl.num_programs(1) - 1)
        def _():
            o_ref[...] = _rmsnorm(o_ref[...], g_ref[...])


def _ffn_down(a, w, x, g, tm, tk, final_norm):
    m, f = a.shape
    d = w.shape[1]
    return pl.pallas_call(
        functools.partial(_ffn_down_kernel, final_norm=final_norm),
        grid=(m // tm, f // tk),
        in_specs=[
            pl.BlockSpec((tm, tk), lambda i, k: (i, k)),
            pl.BlockSpec((tk, d), lambda i, k: (k, 0)),
            pl.BlockSpec((tm, d), lambda i, k: (i, 0)),
            pl.BlockSpec((1, d), lambda i, k: (0, 0)),
        ],
        out_specs=pl.BlockSpec((tm, d), lambda i, k: (i, 0)),
        out_shape=jax.ShapeDtypeStruct((m, d), F32),
        compiler_params=pltpu.CompilerParams(
            dimension_semantics=("arbitrary", "arbitrary"), vmem_limit_bytes=VMEM_LIMIT),
    )(a, w, x, g)


def _gla_gate(zlr, a2_ref, ab_ref):
    xg = _dot(zlr.astype(BF16), a2_ref[...]) + ab_ref[...]
    return _log_sigmoid(xg) * (1.0 / GLA_TAU)


def _head_masks(qk):
    lane_head = lax.broadcasted_iota(jnp.int32, (1, qk), 1) // (qk // N_HEADS)
    return [(lane_head == h).astype(F32) for h in range(N_HEADS)]


def _gla_chunk(qc, kc, vc, bc, la_hi, la_lo, s_flat, e2, hm):
    c, qk = qc.shape
    dv = vc.shape[1] // N_HEADS
    dk = qk // N_HEADS
    nb = c // GLA_SUB
    vcb = vc.astype(BF16)
    sub_row = lax.broadcasted_iota(jnp.int32, (GLA_SUB, 1), 0)
    col_j = lax.broadcasted_iota(jnp.int32, (1, c), 1)

    o_diag = []
    for blk in range(nb):
        r0 = blk * GLA_SUB
        q_b, k_b, b_b = qc[r0:r0 + GLA_SUB], kc[r0:r0 + GLA_SUB], bc[r0:r0 + GLA_SUB]
        v_b = vc[r0:r0 + GLA_SUB]
        ts = []
        for j in range(GLA_SUB):
            dec = jnp.exp(jnp.minimum(b_b - b_b[j:j + 1], 0.0))
            ts.append(jnp.where(sub_row >= j, q_b * k_b[j:j + 1] * dec, 0.0).astype(BF16))
        r = _dot(jnp.concatenate(ts, axis=0), e2)
        acc = r[0:GLA_SUB] * v_b[0:1]
        for j in range(1, GLA_SUB):
            acc = acc + r[j * GLA_SUB:(j + 1) * GLA_SUB] * v_b[j:j + 1]
        o_diag.append(acc)
    o = jnp.concatenate(o_diag, axis=0)

    bref = [None] + [bc[blk * GLA_SUB - 1:blk * GLA_SUB] for blk in range(1, nb)]
    bref_rows = jnp.concatenate(
        [jnp.zeros((GLA_SUB, qk), F32)] + [jnp.broadcast_to(bref[blk], (GLA_SUB, qk)) for blk in range(1, nb)],
        axis=0)
    qt = qc * jnp.exp(bc - bref_rows)
    att_blk = [None]
    for blk in range(1, nb):
        kt = (kc * jnp.exp(jnp.minimum(bref[blk] - bc, 0.0))).astype(BF16)
        q_b = qt[blk * GLA_SUB:(blk + 1) * GLA_SUB]
        qm = jnp.concatenate([q_b * hm[h] for h in range(N_HEADS)], axis=0).astype(BF16)
        a = _dot_nt(qm, kt)
        att_blk.append(jnp.where(col_j < blk * GLA_SUB, a, 0.0))
    o_off = []
    for h in range(N_HEADS):
        att = jnp.concatenate(
            [jnp.zeros((GLA_SUB, c), F32)] + [att_blk[blk][h * GLA_SUB:(h + 1) * GLA_SUB] for blk in range(1, nb)],
            axis=0)
        o_off.append(_dot(att.astype(BF16), vcb[:, h * dv:(h + 1) * dv]))
    o = o + jnp.concatenate(o_off, axis=1)

    qe = qc * jnp.exp(bc)
    qe_m = jnp.concatenate([qe * hm[h] for h in range(N_HEADS)], axis=0).astype(BF16)
    o_int = _dot(qe_m, s_flat.astype(BF16))
    o = o + jnp.concatenate([o_int[h * c:(h + 1) * c] for h in range(N_HEADS)], axis=1)

    kk = (kc * jnp.exp(bc[c - 1:c] - bc)).astype(BF16)
    kv = _dot_tn(kk, vcb)
    kv_d = jnp.concatenate([kv[h * dk:(h + 1) * dk, h * dv:(h + 1) * dv] for h in range(N_HEADS)], axis=0)
    ones = jnp.ones((c, dv), BF16)
    decay = jnp.exp(_dot_tn(la_hi, ones) + _dot_tn(la_lo, ones))
    return o, decay * s_flat + kv_d


def _gla_out(o, gn, r):
    dv = o.shape[1] // N_HEADS
    parts = []
    for h in range(N_HEADS):
        oh = o[:, h * dv:(h + 1) * dv]
        parts.append(oh * lax.rsqrt(jnp.mean(oh * oh, axis=-1, keepdims=True) + EPS))
    return jnp.concatenate(parts, axis=1) * gn * _silu(r)


def _cols(g):
    qk = g // 2
    c = {}
    off = 0
    for name, width in (("a_x", g), ("a_b", g), ("a_c", g), ("q", qk), ("k", qk), ("v", g), ("r", g),
                        ("c_val", g), ("c_gate", g), ("d_u", g), ("d_v", g)):
        c[name] = (off, off + width)
        off += width
    return c, off


def _mixer_prompt_kernel(z_ref, zlr_ref, caw_ref, a2_ref, ab_ref, gn_ref, ccw_ref, ccb_ref,
                         lcg_ref, lcb_ref, ldg_ref, ldb_ref, sgw_ref, sgb_ref, tri_ref, e2_ref,
                         y_ref, na_ref, s_ref, nc_ref, sv_ref, abuf, cbuf):
    t = pl.program_id(1)
    last = pl.num_programs(1) - 1
    tl = z_ref.shape[0]
    g = gn_ref.shape[1]
    qk = g // 2
    dv = g // N_HEADS
    cols, _ = _cols(g)
    a_head, c_head = 8, 32

    def zc(name):
        lo, hi = cols[name]
        return z_ref[:, lo:hi]

    @pl.when(t == 0)
    def _():
        abuf[0:a_head, :] = jnp.zeros((a_head, g), F32)
        cbuf[0:c_head, :] = jnp.zeros((c_head, g), F32)
        s_ref[...] = jnp.zeros(s_ref.shape, F32)

    xa = zc("a_c") * zc("a_x")
    abuf[a_head:a_head + tl, :] = xa
    conv = caw_ref[SC_WIDTH - 1:SC_WIDTH, :] * xa
    for kk in range(SC_WIDTH - 1):
        sh = SC_WIDTH - 1 - kk
        conv = conv + caw_ref[kk:kk + 1, :] * abuf[a_head - sh:a_head - sh + tl, :]
    y_ref[:, 0:g] = (zc("a_b") * conv).astype(BF16)

    @pl.when(t == last)
    def _():
        na_ref[...] = abuf[a_head + tl - (SC_WIDTH - 1):a_head + tl, :]

    abuf[0:a_head, :] = abuf[tl:tl + a_head, :]

    la = _gla_gate(zlr_ref[...], a2_ref, ab_ref)
    la_hi, la_lo = _split_bf16(la)
    tri = tri_ref[...]
    b_all = _dot(tri, la_hi) + _dot(tri, la_lo)
    q_all = zc("q") * (float(qk // N_HEADS) ** -0.5)
    k_all = zc("k")
    v_all = zc("v")
    hm = _head_masks(qk)
    e2 = e2_ref[...]
    s_flat = s_ref[...]
    outs = []
    for ci in range(tl // GLA_CHUNK):
        sl = slice(ci * GLA_CHUNK, (ci + 1) * GLA_CHUNK)
        o_c, s_flat = _gla_chunk(q_all[sl], k_all[sl], v_all[sl], b_all[sl], la_hi[sl], la_lo[sl], s_flat, e2, hm)
        outs.append(o_c)
    s_ref[...] = s_flat
    y_ref[:, g:2 * g] = _gla_out(jnp.concatenate(outs, axis=0), gn_ref[...], zc("r")).astype(BF16)

    cin = zc("c_val") * _sigmoid(zc("c_gate"))
    cbuf[c_head:c_head + tl, :] = cin
    acc = ccb_ref[...] + ccw_ref[CF_WIDTH - 1:CF_WIDTH, :] * cin
    for kk in range(CF_WIDTH - 1):
        sh = CF_WIDTH - 1 - kk
        acc = acc + ccw_ref[kk:kk + 1, :] * cbuf[c_head - sh:c_head - sh + tl, :]
    y_ref[:, 2 * g:3 * g] = _silu(_layernorm(acc, lcg_ref[...], lcb_ref[...])).astype(BF16)

    @pl.when(t == last)
    def _():
        nc_ref[...] = cbuf[c_head + tl - (CF_WIDTH - 1):c_head + tl, :]

    cbuf[0:c_head, :] = cbuf[tl:tl + c_head, :]

    du = _gelu_tanh(zc("d_u"))
    vd = _layernorm(_gelu_tanh(zc("d_v")), ldg_ref[...], ldb_ref[...])
    sv_ref[...] = vd
    vdb = vd.astype(BF16)
    row = lax.broadcasted_iota(jnp.int32, (tl, tl), 0)
    col = lax.broadcasted_iota(jnp.int32, (tl, tl), 1)
    sv = []
    for h in range(N_HEADS):
        ws = jnp.where(row >= col, sgw_ref[h], 0.0).astype(BF16)
        sv.append(_dot(ws, vdb[:, h * dv:(h + 1) * dv]))
    y_ref[:, 3 * g:4 * g] = (du * (jnp.concatenate(sv, axis=1) + sgb_ref[...])).astype(BF16)


def _full(shape):
    nd = len(shape)
    return pl.BlockSpec(shape, lambda *_: (0,) * nd)


def _mixer_prompt(z, zlr, batch, seq, wts):
    g = wts["gn"].shape[1]
    tl = SG_CHUNK
    nt = seq // tl
    row_map = lambda b, t: (b * nt + t, 0)
    names = ("caw", "a2", "ab", "gn", "ccw", "ccb", "lcg", "lcb", "ldg", "ldb", "sgw", "sgb", "tri", "e2")
    consts = [wts[n] for n in names]
    return pl.pallas_call(
        _mixer_prompt_kernel,
        grid=(batch, nt),
        in_specs=[pl.BlockSpec((tl, z.shape[1]), row_map), pl.BlockSpec((tl, LANES), row_map)]
                 + [_full(c.shape) for c in consts],
        out_specs=[
            pl.BlockSpec((tl, 4 * g), row_map),
            pl.BlockSpec((None, SC_WIDTH - 1, g), lambda b, t: (b, 0, 0)),
            pl.BlockSpec((None, g // 2, g // N_HEADS), lambda b, t: (b, 0, 0)),
            pl.BlockSpec((None, CF_WIDTH - 1, g), lambda b, t: (b, 0, 0)),
            pl.BlockSpec((None, tl, g), lambda b, t: (b, 0, 0)),
        ],
        out_shape=[
            jax.ShapeDtypeStruct((batch * seq, 4 * g), BF16),
            jax.ShapeDtypeStruct((batch, SC_WIDTH - 1, g), F32),
            jax.ShapeDtypeStruct((batch, g // 2, g // N_HEADS), F32),
            jax.ShapeDtypeStruct((batch, CF_WIDTH - 1, g), F32),
            jax.ShapeDtypeStruct((batch, tl, g), F32),
        ],
        scratch_shapes=[pltpu.VMEM((8 + tl, g), F32), pltpu.VMEM((32 + tl, g), F32)],
        compiler_params=pltpu.CompilerParams(
            dimension_semantics=("arbitrary", "arbitrary"), vmem_limit_bytes=VMEM_LIMIT),
    )(z, zlr, *consts)


def _mixer_sample_kernel(z_ref, zlr_ref, pa_ref, s_ref, pc_ref, caw_ref, a2_ref, ab_ref, gn_ref, ccw_ref, ccb_ref,
                         lcg_ref, lcb_ref, ldg_ref, ldb_ref, sgw0_ref, sgb0_ref,
                         y_ref, na_ref, so_ref, nc_ref, sv_ref, o_scr):
    bt = z_ref.shape[0]
    g = gn_ref.shape[1]
    qk = g // 2
    dk = qk // N_HEADS
    dv = g // N_HEADS
    cols, _ = _cols(g)

    def zc(name):
        lo, hi = cols[name]
        return z_ref[:, lo:hi]

    xa = zc("a_c") * zc("a_x")
    conv = caw_ref[SC_WIDTH - 1:SC_WIDTH, :] * xa
    for kk in range(SC_WIDTH - 1):
        conv = conv + caw_ref[kk:kk + 1, :] * pa_ref[:, kk * g:(kk + 1) * g]
    y_ref[:, 0:g] = (zc("a_b") * conv).astype(BF16)
    for kk in range(1, SC_WIDTH - 1):
        na_ref[:, (kk - 1) * g:kk * g] = pa_ref[:, kk * g:(kk + 1) * g]
    na_ref[:, (SC_WIDTH - 2) * g:(SC_WIDTH - 1) * g] = xa

    la = _gla_gate(zlr_ref[...], a2_ref, ab_ref)
    a_t = jnp.exp(la).T
    k_t = zc("k").T
    q_rows = zc("q") * (float(dk) ** -0.5)
    v_rows = zc("v")
    hm = _head_masks(qk)
    hm_rows = jnp.concatenate(hm + [jnp.zeros_like(hm[0])] * (8 - N_HEADS), axis=0)
    for b in range(bt):
        a_col = jnp.broadcast_to(a_t[:, b:b + 1], (qk, dv))
        k_col = jnp.broadcast_to(k_t[:, b:b + 1], (qk, dv))
        v_b = jnp.concatenate(
            [jnp.broadcast_to(v_rows[b:b + 1, h * dv:(h + 1) * dv], (dk, dv)) for h in range(N_HEADS)], axis=0)
        s_new = a_col * s_ref[b] + k_col * v_b
        so_ref[b] = s_new
        q_m = (q_rows[b:b + 1] * hm_rows).astype(BF16)
        o_b = _dot(q_m, s_new.astype(BF16))
        for h in range(N_HEADS):
            o_scr[b:b + 1, h * dv:(h + 1) * dv] = o_b[h:h + 1]
    y_ref[:, g:2 * g] = _gla_out(o_scr[...], gn_ref[...], zc("r")).astype(BF16)

    cin = zc("c_val") * _sigmoid(zc("c_gate"))
    acc = ccb_ref[...] + ccw_ref[CF_WIDTH - 1:CF_WIDTH, :] * cin
    for kk in range(CF_WIDTH - 1):
        acc = acc + ccw_ref[kk:kk + 1, :] * pc_ref[:, kk * g:(kk + 1) * g]
    y_ref[:, 2 * g:3 * g] = _silu(_layernorm(acc, lcg_ref[...], lcb_ref[...])).astype(BF16)
    nc_ref[:, 0:(CF_WIDTH - 2) * g] = pc_ref[:, g:(CF_WIDTH - 1) * g]
    nc_ref[:, (CF_WIDTH - 2) * g:(CF_WIDTH - 1) * g] = cin

    du = _gelu_tanh(zc("d_u"))
    vd = _layernorm(_gelu_tanh(zc("d_v")), ldg_ref[...], ldb_ref[...])
    sv_ref[...] = vd
    y_ref[:, 3 * g:4 * g] = (du * (sgw0_ref[...] * vd + sgb0_ref[...])).astype(BF16)


def _mixer_sample(z, zlr, pa, s, pc, wts, bt):
    nb, g = z.shape[0], wts["gn"].shape[1]
    names = ("caw", "a2", "ab", "gn", "ccw", "ccb", "lcg", "lcb", "ldg", "ldb", "sgw0", "sgb0")
    consts = [wts[n] for n in names]
    rows = lambda i: (i, 0)
    return pl.pallas_call(
        _mixer_sample_kernel,
        grid=(nb // bt,),
        in_specs=[
            pl.BlockSpec((bt, z.shape[1]), rows),
            pl.BlockSpec((bt, LANES), rows),
            pl.BlockSpec((bt, pa.shape[1]), rows),
            pl.BlockSpec((bt,) + s.shape[1:], lambda i: (i, 0, 0)),
            pl.BlockSpec((bt, pc.shape[1]), rows),
        ] + [_full(c.shape) for c in consts],
        out_specs=[
            pl.BlockSpec((bt, 4 * g), rows),
            pl.BlockSpec((bt, pa.shape[1]), rows),
            pl.BlockSpec((bt,) + s.shape[1:], lambda i: (i, 0, 0)),
            pl.BlockSpec((bt, pc.shape[1]), rows),
            pl.BlockSpec((bt, g), rows),
        ],
        out_shape=[
            jax.ShapeDtypeStruct((nb, 4 * g), BF16),
            jax.ShapeDtypeStruct(pa.shape, F32),
            jax.ShapeDtypeStruct(s.shape, F32),
            jax.ShapeDtypeStruct(pc.shape, F32),
            jax.ShapeDtypeStruct((nb, g), F32),
        ],
        scratch_shapes=[pltpu.VMEM((bt, g), F32)],
        compiler_params=pltpu.CompilerParams(
            dimension_semantics=("arbitrary",), vmem_limit_bytes=VMEM_LIMIT),
    )(z, zlr, pa, s, pc, *consts)


def _largest_tile(m, cap, mult):
    t = min(m, cap)
    while m % t or t % mult:
        t -= mult
    return t


def _layer_weights(l, g, w_in, conv_a_w, gla_a2, gla_a_bias, gla_norm, conv_c_w, conv_c_b,
                   ln_c_g, ln_c_b, ln_d_g, ln_d_b, sg_w, sg_b):
    qk, dv = g // 2, g // N_HEADS
    lr0 = 3 * g + 2 * qk + 2 * g
    row = lambda v: v.reshape(1, -1)
    w = w_in[l]
    return {
        "w_main": jnp.concatenate([w[:, :lr0], w[:, lr0 + GLA_LOWRANK:]], axis=1).astype(BF16),
        "w_lr": jnp.pad(w[:, lr0:lr0 + GLA_LOWRANK], ((0, 0), (0, LANES - GLA_LOWRANK))).astype(BF16),
        "caw": conv_a_w[l],
        "a2": jnp.pad(gla_a2[l], ((0, LANES - GLA_LOWRANK), (0, 0))).astype(BF16),
        "ab": row(gla_a_bias[l]), "gn": row(gla_norm[l]),
        "ccw": conv_c_w[l], "ccb": row(conv_c_b[l]),
        "lcg": row(ln_c_g[l]), "lcb": row(ln_c_b[l]), "ldg": row(ln_d_g[l]), "ldb": row(ln_d_b[l]),
        "sgw": sg_w[l],
        "sgb": jnp.repeat(sg_b[l].T, dv, axis=1),
        "sgw0": row(jnp.repeat(sg_w[l][:, 0, 0], dv)),
        "sgb0": row(jnp.repeat(sg_b[l][:, 0], dv)),
    }


def _gla_constants(g):
    qk, dv = g // 2, g // N_HEADS
    r = jnp.arange(SG_CHUNK)
    tri = ((r[:, None] >= r[None, :]) & (r[:, None] // GLA_CHUNK == r[None, :] // GLA_CHUNK)).astype(BF16)
    e2 = (jnp.arange(qk)[:, None] // (qk // N_HEADS) == jnp.arange(g)[None, :] // dv).astype(BF16)
    return tri, e2


def kernel(x_prompt, x_sample, state_conv_a, state_gla, state_conv_c, norm_mix, w_in, conv_a_w, gla_a2, gla_a_bias,
           gla_norm, conv_c_w, conv_c_b, ln_c_g, ln_c_b, ln_d_g, ln_d_b, sg_w, sg_b, w_o, norm_ffn, w_gate, w_up,
           w_down, norm_final):
    batch, seq, d = x_prompt.shape
    nb, dec_seq, _ = x_sample.shape
    depth = w_in.shape[0]
    g = d // 4
    assert dec_seq == 1 and seq % SG_CHUNK == 0 and g % (N_HEADS * LANES) == 0
    dff = w_gate.shape[2]
    n_main = w_in.shape[2] - GLA_LOWRANK
    tri, e2 = _gla_constants(g)

    mp = batch * seq
    tm_p = _largest_tile(mp, 1024, 16)
    tm_s = nb
    tn_in = _largest_tile(n_main, 1024, 256)
    tf = _largest_tile(dff, 512, 256)

    xp = x_prompt.reshape(mp, d)
    xs = x_sample.reshape(nb, d)
    outs_p = {k: [] for k in ("a", "s", "c", "v")}
    outs_s = {k: [] for k in ("a", "s", "c", "v")}
    for l in range(depth):
        wts = _layer_weights(l, g, w_in, conv_a_w, gla_a2, gla_a_bias, gla_norm, conv_c_w, conv_c_b,
                             ln_c_g, ln_c_b, ln_d_g, ln_d_b, sg_w, sg_b)
        wts["tri"], wts["e2"] = tri, e2
        gm, gf = norm_mix[l].reshape(1, d), norm_ffn[l].reshape(1, d)
        gfin = norm_final.reshape(1, d)
        final = l == depth - 1

        z, zlr = _norm_proj(xp, gm, wts["w_main"], wts["w_lr"], tm_p, tn_in)
        y, na, s_new, nc, sv = _mixer_prompt(z, zlr, batch, seq, wts)
        xp = _proj_res(y, w_o[l], xp, tm_p, 512)
        a = _ffn_up(xp, gf, w_gate[l], w_up[l], tm_p, tf)
        xp = _ffn_down(a, w_down[l], xp, gfin, tm_p, tf, final)
        outs_p["a"].append(na)
        outs_p["s"].append(s_new.reshape(batch, N_HEADS, g // 2 // N_HEADS, g // N_HEADS))
        outs_p["c"].append(nc)
        outs_p["v"].append(sv)

        z, zlr = _norm_proj(xs, gm, wts["w_main"], wts["w_lr"], tm_s, tn_in)
        y, na, s_new, nc, sv = _mixer_sample(
            z, zlr,
            state_conv_a[l].reshape(nb, (SC_WIDTH - 1) * g),
            state_gla[l].reshape(nb, g // 2, g // N_HEADS),
            state_conv_c[l].reshape(nb, (CF_WIDTH - 1) * g),
            wts, 32)
        xs = _proj_res(y, w_o[l], xs, tm_s, 512)
        a = _ffn_up(xs, gf, w_gate[l], w_up[l], tm_s, tf)
        xs = _ffn_down(a, w_down[l], xs, gfin, tm_s, tf, final)
        outs_s["a"].append(na.reshape(nb, SC_WIDTH - 1, g))
        outs_s["s"].append(s_new.reshape(nb, N_HEADS, g // 2 // N_HEADS, g // N_HEADS))
        outs_s["c"].append(nc.reshape(nb, CF_WIDTH - 1, g))
        outs_s["v"].append(sv.reshape(nb, 1, g))

    st = jnp.stack
    return (xp.reshape(batch, seq, d), xs.reshape(nb, 1, d),
            st(outs_p["a"]), st(outs_s["a"]), st(outs_p["s"]), st(outs_s["s"]),
            st(outs_p["c"]), st(outs_s["c"]), st(outs_p["v"]), st(outs_s["v"]))
```

```python
import functools
import math

import jax
import jax.numpy as jnp
from jax import lax
from jax.experimental import pallas as pl
from jax.experimental.pallas import tpu as pltpu

F32 = jnp.float32
BF16 = jnp.bfloat16
EPS = 1e-6

N_HEADS = 4
GLA_LOWRANK = 16
GLA_TAU = 16.0
SC_WIDTH = 3
CF_WIDTH = 31
SG_CHUNK = 128
GLA_CHUNK = 64
GLA_SUB = 16
LANES = 128
VMEM_LIMIT = 56 * 1024 * 1024


def _dot(a, b):
    return jnp.dot(a, b, preferred_element_type=F32)


def _dot_nt(a, b):
    return lax.dot_general(a, b, (((1,), (1,)), ((), ())), preferred_element_type=F32)


def _dot_tn(a, b):
    return lax.dot_general(a, b, (((0,), (0,)), ((), ())), preferred_element_type=F32)


def _split_bf16(x):
    hi = x.astype(BF16)
    lo = (x - hi.astype(F32)).astype(BF16)
    return hi, lo


def _rmsnorm(x, g):
    return x * lax.rsqrt(jnp.mean(x * x, axis=-1, keepdims=True) + EPS) * g


def _layernorm(x, g, b):
    mu = jnp.mean(x, axis=-1, keepdims=True)
    xc = x - mu
    return xc * lax.rsqrt(jnp.mean(xc * xc, axis=-1, keepdims=True) + EPS) * g + b


def _sigmoid(x):
    return 1.0 / (1.0 + jnp.exp(-x))


def _silu(x):
    return x * _sigmoid(x)


def _gelu_tanh(x):
    c = math.sqrt(2.0 / math.pi)
    return 0.5 * x * (1.0 + jnp.tanh(c * (x + 0.044715 * (x * x * x))))


def _log_sigmoid(x):
    return jnp.minimum(x, 0.0) - jnp.log(1.0 + jnp.exp(-jnp.abs(x)))


def _norm_proj_kernel(x_ref, g_ref, w_ref, wlr_ref, z_ref, zlr_ref, h_ref):
    @pl.when(pl.program_id(1) == 0)
    def _():
        h = _rmsnorm(x_ref[...], g_ref[...]).astype(BF16)
        h_ref[...] = h
        zlr_ref[...] = _dot(h, wlr_ref[...])

    z_ref[...] = _dot(h_ref[...], w_ref[...])


def _norm_proj(x, g, w, wlr, l, tm, tn):
    m, d = x.shape
    n = w.shape[2]
    return pl.pallas_call(
        _norm_proj_kernel,
        grid=(m // tm, n // tn),
        in_specs=[
            pl.BlockSpec((tm, d), lambda i, j: (i, 0)),
            pl.BlockSpec((1, d), lambda i, j: (0, 0)),
            pl.BlockSpec((None, d, tn), lambda i, j: (l, 0, j)),
            pl.BlockSpec((None, d, LANES), lambda i, j: (l, 0, 0)),
        ],
        out_specs=[
            pl.BlockSpec((tm, tn), lambda i, j: (i, j)),
            pl.BlockSpec((tm, LANES), lambda i, j: (i, 0)),
        ],
        out_shape=[jax.ShapeDtypeStruct((m, n), F32), jax.ShapeDtypeStruct((m, LANES), F32)],
        scratch_shapes=[pltpu.VMEM((tm, d), BF16)],
        compiler_params=pltpu.CompilerParams(
            dimension_semantics=("arbitrary", "arbitrary"), vmem_limit_bytes=VMEM_LIMIT),
    )(x, g, w, wlr)


def _proj_res_kernel(y_ref, w_ref, x_ref, o_ref):
    o_ref[...] = x_ref[...] + _dot(y_ref[...], w_ref[...].astype(BF16))


def _proj_res(y, w, x, l, tm, tn):
    m, k = y.shape
    n = w.shape[2]
    return pl.pallas_call(
        _proj_res_kernel,
        grid=(m // tm, n // tn),
        in_specs=[
            pl.BlockSpec((tm, k), lambda i, j: (i, 0)),
            pl.BlockSpec((None, k, tn), lambda i, j: (l, 0, j)),
            pl.BlockSpec((tm, tn), lambda i, j: (i, j)),
        ],
        out_specs=pl.BlockSpec((tm, tn), lambda i, j: (i, j)),
        out_shape=jax.ShapeDtypeStruct((m, n), F32),
        compiler_params=pltpu.CompilerParams(
            dimension_semantics=("arbitrary", "arbitrary"), vmem_limit_bytes=VMEM_LIMIT),
    )(y, w, x)


def _ffn_up_kernel(x_ref, g_ref, wg_ref, wu_ref, o_ref, h_ref):
    @pl.when(pl.program_id(1) == 0)
    def _():
        h_ref[...] = _rmsnorm(x_ref[...], g_ref[...]).astype(BF16)

    h = h_ref[...]
    a = _dot(h, wg_ref[...].astype(BF16))
    b = _dot(h, wu_ref[...].astype(BF16))
    o_ref[...] = (_silu(a) * b).astype(BF16)


def _ffn_up(x, g, wg, wu, l, tm, tf):
    m, d = x.shape
    f = wg.shape[2]
    return pl.pallas_call(
        _ffn_up_kernel,
        grid=(m // tm, f // tf),
        in_specs=[
            pl.BlockSpec((tm, d), lambda i, j: (i, 0)),
            pl.BlockSpec((1, d), lambda i, j: (0, 0)),
            pl.BlockSpec((None, d, tf), lambda i, j: (l, 0, j)),
            pl.BlockSpec((None, d, tf), lambda i, j: (l, 0, j)),
        ],
        out_specs=pl.BlockSpec((tm, tf), lambda i, j: (i, j)),
        out_shape=jax.ShapeDtypeStruct((m, f), BF16),
        scratch_shapes=[pltpu.VMEM((tm, d), BF16)],
        compiler_params=pltpu.CompilerParams(
            dimension_semantics=("arbitrary", "arbitrary"), vmem_limit_bytes=VMEM_LIMIT),
    )(x, g, wg, wu)


def _ffn_down_kernel(a_ref, w_ref, x_ref, g_ref, o_ref, *, final_norm):
    k = pl.program_id(1)

    @pl.when(k == 0)
    def _():
        o_ref[...] = x_ref[...]

    o_ref[...] += _dot(a_ref[...], w_ref[...].astype(BF16))

    if final_norm:
        @pl.when(k == pl.num_programs(1) - 1)
        def _():
            o_ref[...] = _rmsnorm(o_ref[...], g_ref[...])


def _ffn_down(a, w, x, g, l, tm, tk, final_norm):
    m, f = a.shape
    d = w.shape[2]
    return pl.pallas_call(
        functools.partial(_ffn_down_kernel, final_norm=final_norm),
        grid=(m // tm, f // tk),
        in_specs=[
            pl.BlockSpec((tm, tk), lambda i, k: (i, k)),
            pl.BlockSpec((None, tk, d), lambda i, k: (l, k, 0)),
            pl.BlockSpec((tm, d), lambda i, k: (i, 0)),
            pl.BlockSpec((1, d), lambda i, k: (0, 0)),
        ],
        out_specs=pl.BlockSpec((tm, d), lambda i, k: (i, 0)),
        out_shape=jax.ShapeDtypeStruct((m, d), F32),
        compiler_params=pltpu.CompilerParams(
            dimension_semantics=("arbitrary", "arbitrary"), vmem_limit_bytes=VMEM_LIMIT),
    )(a, w, x, g)


def _gla_gate(zlr, a2_ref, ab_ref):
    xg = _dot(zlr.astype(BF16), a2_ref[...]) + ab_ref[...]
    return _log_sigmoid(xg) * (1.0 / GLA_TAU)


def _head_masks(qk):
    lane_head = lax.broadcasted_iota(jnp.int32, (1, qk), 1) // (qk // N_HEADS)
    return [(lane_head == h).astype(F32) for h in range(N_HEADS)]


def _gla_chunk(qc, kc, vc, bc, la_hi, la_lo, s_flat, e2, hm):
    c, qk = qc.shape
    dv = vc.shape[1] // N_HEADS
    dk = qk // N_HEADS
    nb = c // GLA_SUB
    vcb = vc.astype(BF16)
    sub_row = lax.broadcasted_iota(jnp.int32, (GLA_SUB, 1), 0)
    col_j = lax.broadcasted_iota(jnp.int32, (1, c), 1)

    o_diag = []
    for blk in range(nb):
        r0 = blk * GLA_SUB
        q_b, k_b, b_b = qc[r0:r0 + GLA_SUB], kc[r0:r0 + GLA_SUB], bc[r0:r0 + GLA_SUB]
        v_b = vc[r0:r0 + GLA_SUB]
        ts = []
        for j in range(GLA_SUB):
            dec = jnp.exp(jnp.minimum(b_b - b_b[j:j + 1], 0.0))
            ts.append(jnp.where(sub_row >= j, q_b * k_b[j:j + 1] * dec, 0.0).astype(BF16))
        r = _dot(jnp.concatenate(ts, axis=0), e2)
        acc = r[0:GLA_SUB] * v_b[0:1]
        for j in range(1, GLA_SUB):
            acc = acc + r[j * GLA_SUB:(j + 1) * GLA_SUB] * v_b[j:j + 1]
        o_diag.append(acc)
    o = jnp.concatenate(o_diag, axis=0)

    bref = [None] + [bc[blk * GLA_SUB - 1:blk * GLA_SUB] for blk in range(1, nb)]
    bref_rows = jnp.concatenate(
        [jnp.zeros((GLA_SUB, qk), F32)] + [jnp.broadcast_to(bref[blk], (GLA_SUB, qk)) for blk in range(1, nb)],
        axis=0)
    qt = qc * jnp.exp(bc - bref_rows)
    att_blk = [None]
    for blk in range(1, nb):
        kt = (kc * jnp.exp(jnp.minimum(bref[blk] - bc, 0.0))).astype(BF16)
        q_b = qt[blk * GLA_SUB:(blk + 1) * GLA_SUB]
        qm = jnp.concatenate([q_b * hm[h] for h in range(N_HEADS)], axis=0).astype(BF16)
        a = _dot_nt(qm, kt)
        att_blk.append(jnp.where(col_j < blk * GLA_SUB, a, 0.0))
    o_off = []
    for h in range(N_HEADS):
        att = jnp.concatenate(
            [jnp.zeros((GLA_SUB, c), F32)] + [att_blk[blk][h * GLA_SUB:(h + 1) * GLA_SUB] for blk in range(1, nb)],
            axis=0)
        o_off.append(_dot(att.astype(BF16), vcb[:, h * dv:(h + 1) * dv]))
    o = o + jnp.concatenate(o_off, axis=1)

    qe = qc * jnp.exp(bc)
    qe_m = jnp.concatenate([qe * hm[h] for h in range(N_HEADS)], axis=0).astype(BF16)
    o_int = _dot(qe_m, s_flat.astype(BF16))
    o = o + jnp.concatenate([o_int[h * c:(h + 1) * c] for h in range(N_HEADS)], axis=1)

    kk = (kc * jnp.exp(bc[c - 1:c] - bc)).astype(BF16)
    kv = _dot_tn(kk, vcb)
    kv_d = jnp.concatenate([kv[h * dk:(h + 1) * dk, h * dv:(h + 1) * dv] for h in range(N_HEADS)], axis=0)
    ones = jnp.ones((c, dv), BF16)
    decay = jnp.exp(_dot_tn(la_hi, ones) + _dot_tn(la_lo, ones))
    return o, decay * s_flat + kv_d


def _gla_out(o, gn, r):
    dv = o.shape[1] // N_HEADS
    parts = []
    for h in range(N_HEADS):
        oh = o[:, h * dv:(h + 1) * dv]
        parts.append(oh * lax.rsqrt(jnp.mean(oh * oh, axis=-1, keepdims=True) + EPS))
    return jnp.concatenate(parts, axis=1) * gn * _silu(r)


def _cols(g):
    qk = g // 2
    c = {}
    off = 0
    for name, width in (("a_x", g), ("a_b", g), ("a_c", g), ("q", qk), ("k", qk), ("v", g), ("r", g),
                        ("c_val", g), ("c_gate", g), ("d_u", g), ("d_v", g)):
        c[name] = (off, off + width)
        off += width
    return c, off


def _mixer_prompt_kernel(z_ref, zlr_ref, caw_ref, a2_ref, ab_ref, gn_ref, ccw_ref, ccb_ref,
                         lcg_ref, lcb_ref, ldg_ref, ldb_ref, sgw_ref, sgb_ref, tri_ref, e2_ref,
                         y_ref, na_ref, s_ref, nc_ref, sv_ref, abuf, cbuf):
    t = pl.program_id(1)
    last = pl.num_programs(1) - 1
    tl = z_ref.shape[0]
    g = gn_ref.shape[1]
    qk = g // 2
    dv = g // N_HEADS
    cols, _ = _cols(g)
    a_head, c_head = 8, 32

    def zc(name):
        lo, hi = cols[name]
        return z_ref[:, lo:hi]

    @pl.when(t == 0)
    def _():
        abuf[0:a_head, :] = jnp.zeros((a_head, g), F32)
        cbuf[0:c_head, :] = jnp.zeros((c_head, g), F32)
        s_ref[...] = jnp.zeros(s_ref.shape, F32)

    xa = zc("a_c") * zc("a_x")
    abuf[a_head:a_head + tl, :] = xa
    conv = caw_ref[SC_WIDTH - 1:SC_WIDTH, :] * xa
    for kk in range(SC_WIDTH - 1):
        sh = SC_WIDTH - 1 - kk
        conv = conv + caw_ref[kk:kk + 1, :] * abuf[a_head - sh:a_head - sh + tl, :]
    y_ref[:, 0:g] = (zc("a_b") * conv).astype(BF16)

    @pl.when(t == last)
    def _():
        na_ref[...] = abuf[a_head + tl - (SC_WIDTH - 1):a_head + tl, :]

    abuf[0:a_head, :] = abuf[tl:tl + a_head, :]

    la = _gla_gate(zlr_ref[...], a2_ref, ab_ref)
    la_hi, la_lo = _split_bf16(la)
    tri = tri_ref[...]
    b_all = _dot(tri, la_hi) + _dot(tri, la_lo)
    q_all = zc("q") * (float(qk // N_HEADS) ** -0.5)
    k_all = zc("k")
    v_all = zc("v")
    hm = _head_masks(qk)
    e2 = e2_ref[...]
    s_flat = s_ref[...]
    outs = []
    for ci in range(tl // GLA_CHUNK):
        sl = slice(ci * GLA_CHUNK, (ci + 1) * GLA_CHUNK)
        o_c, s_flat = _gla_chunk(q_all[sl], k_all[sl], v_all[sl], b_all[sl], la_hi[sl], la_lo[sl], s_flat, e2, hm)
        outs.append(o_c)
    s_ref[...] = s_flat
    y_ref[:, g:2 * g] = _gla_out(jnp.concatenate(outs, axis=0), gn_ref[...], zc("r")).astype(BF16)

    cin = zc("c_val") * _sigmoid(zc("c_gate"))
    cbuf[c_head:c_head + tl, :] = cin
    acc = ccb_ref[...] + ccw_ref[CF_WIDTH - 1:CF_WIDTH, :] * cin
    for kk in range(CF_WIDTH - 1):
        sh = CF_WIDTH - 1 - kk
        acc = acc + ccw_ref[kk:kk + 1, :] * cbuf[c_head - sh:c_head - sh + tl, :]
    y_ref[:, 2 * g:3 * g] = _silu(_layernorm(acc, lcg_ref[...], lcb_ref[...])).astype(BF16)

    @pl.when(t == last)
    def _():
        nc_ref[...] = cbuf[c_head + tl - (CF_WIDTH - 1):c_head + tl, :]

    cbuf[0:c_head, :] = cbuf[tl:tl + c_head, :]

    du = _gelu_tanh(zc("d_u"))
    vd = _layernorm(_gelu_tanh(zc("d_v")), ldg_ref[...], ldb_ref[...])
    sv_ref[...] = vd
    vdb = vd.astype(BF16)
    row = lax.broadcasted_iota(jnp.int32, (tl, tl), 0)
    col = lax.broadcasted_iota(jnp.int32, (tl, tl), 1)
    sv = []
    for h in range(N_HEADS):
        ws = jnp.where(row >= col, sgw_ref[h], 0.0).astype(BF16)
        sv.append(_dot(ws, vdb[:, h * dv:(h + 1) * dv]))
    y_ref[:, 3 * g:4 * g] = (du * (jnp.concatenate(sv, axis=1) + sgb_ref[...])).astype(BF16)


def _full(shape):
    nd = len(shape)
    return pl.BlockSpec(shape, lambda *_: (0,) * nd)


def _mixer_prompt(z, zlr, batch, seq, wts):
    g = wts["gn"].shape[1]
    tl = SG_CHUNK
    nt = seq // tl
    row_map = lambda b, t: (b * nt + t, 0)
    names = ("caw", "a2", "ab", "gn", "ccw", "ccb", "lcg", "lcb", "ldg", "ldb", "sgw", "sgb", "tri", "e2")
    consts = [wts[n] for n in names]
    return pl.pallas_call(
        _mixer_prompt_kernel,
        grid=(batch, nt),
        in_specs=[pl.BlockSpec((tl, z.shape[1]), row_map), pl.BlockSpec((tl, LANES), row_map)]
                 + [_full(c.shape) for c in consts],
        out_specs=[
            pl.BlockSpec((tl, 4 * g), row_map),
            pl.BlockSpec((None, SC_WIDTH - 1, g), lambda b, t: (b, 0, 0)),
            pl.BlockSpec((None, g // 2, g // N_HEADS), lambda b, t: (b, 0, 0)),
            pl.BlockSpec((None, CF_WIDTH - 1, g), lambda b, t: (b, 0, 0)),
            pl.BlockSpec((None, tl, g), lambda b, t: (b, 0, 0)),
        ],
        out_shape=[
            jax.ShapeDtypeStruct((batch * seq, 4 * g), BF16),
            jax.ShapeDtypeStruct((batch, SC_WIDTH - 1, g), F32),
            jax.ShapeDtypeStruct((batch, g // 2, g // N_HEADS), F32),
            jax.ShapeDtypeStruct((batch, CF_WIDTH - 1, g), F32),
            jax.ShapeDtypeStruct((batch, tl, g), F32),
        ],
        scratch_shapes=[pltpu.VMEM((8 + tl, g), F32), pltpu.VMEM((32 + tl, g), F32)],
        compiler_params=pltpu.CompilerParams(
            dimension_semantics=("arbitrary", "arbitrary"), vmem_limit_bytes=VMEM_LIMIT),
    )(z, zlr, *consts)


def _mixer_sample_kernel(z_ref, zlr_ref, pa_ref, s_ref, pc_ref, caw_ref, a2_ref, ab_ref, gn_ref, ccw_ref, ccb_ref,
                         lcg_ref, lcb_ref, ldg_ref, ldb_ref, sgw0_ref, sgb0_ref,
                         y_ref, na_ref, so_ref, nc_ref, sv_ref, o_scr):
    bt = z_ref.shape[0]
    g = gn_ref.shape[1]
    qk = g // 2
    dk = qk // N_HEADS
    dv = g // N_HEADS
    cols, _ = _cols(g)

    def zc(name):
        lo, hi = cols[name]
        return z_ref[:, lo:hi]

    xa = zc("a_c") * zc("a_x")
    conv = caw_ref[SC_WIDTH - 1:SC_WIDTH, :] * xa
    for kk in range(SC_WIDTH - 1):
        conv = conv + caw_ref[kk:kk + 1, :] * pa_ref[:, kk, :]
    y_ref[:, 0:g] = (zc("a_b") * conv).astype(BF16)
    for kk in range(1, SC_WIDTH - 1):
        na_ref[:, kk - 1, :] = pa_ref[:, kk, :]
    na_ref[:, SC_WIDTH - 2, :] = xa

    la = _gla_gate(zlr_ref[...], a2_ref, ab_ref)
    a_t = jnp.exp(la).T
    k_t = zc("k").T
    q_rows = zc("q") * (float(dk) ** -0.5)
    v_rows = zc("v")
    hm = _head_masks(qk)
    hm_rows = jnp.concatenate(hm + [jnp.zeros_like(hm[0])] * (8 - N_HEADS), axis=0)
    for b in range(bt):
        a_col = jnp.broadcast_to(a_t[:, b:b + 1], (qk, dv))
        k_col = jnp.broadcast_to(k_t[:, b:b + 1], (qk, dv))
        v_b = jnp.concatenate(
            [jnp.broadcast_to(v_rows[b:b + 1, h * dv:(h + 1) * dv], (dk, dv)) for h in range(N_HEADS)], axis=0)
        s_new = a_col * s_ref[b] + k_col * v_b
        so_ref[b] = s_new
        q_m = (q_rows[b:b + 1] * hm_rows).astype(BF16)
        o_b = _dot(q_m, s_new.astype(BF16))
        for h in range(N_HEADS):
            o_scr[b:b + 1, h * dv:(h + 1) * dv] = o_b[h:h + 1]
    y_ref[:, g:2 * g] = _gla_out(o_scr[...], gn_ref[...], zc("r")).astype(BF16)

    cin = zc("c_val") * _sigmoid(zc("c_gate"))
    acc = ccb_ref[...] + ccw_ref[CF_WIDTH - 1:CF_WIDTH, :] * cin
    for kk in range(CF_WIDTH - 1):
        acc = acc + ccw_ref[kk:kk + 1, :] * pc_ref[:, kk, :]
    y_ref[:, 2 * g:3 * g] = _silu(_layernorm(acc, lcg_ref[...], lcb_ref[...])).astype(BF16)
    nc_ref[:, 0:CF_WIDTH - 2, :] = pc_ref[:, 1:CF_WIDTH - 1, :]
    nc_ref[:, CF_WIDTH - 2, :] = cin

    du = _gelu_tanh(zc("d_u"))
    vd = _layernorm(_gelu_tanh(zc("d_v")), ldg_ref[...], ldb_ref[...])
    sv_ref[...] = vd
    y_ref[:, 3 * g:4 * g] = (du * (sgw0_ref[...] * vd + sgb0_ref[...])).astype(BF16)


def _mixer_sample(z, zlr, pa, s, pc, wts, l, bt):
    nb, g = z.shape[0], wts["gn"].shape[1]
    names = ("caw", "a2", "ab", "gn", "ccw", "ccb", "lcg", "lcb", "ldg", "ldb", "sgw0", "sgb0")
    consts = [wts[n] for n in names]
    rows = lambda i: (i, 0)
    state_spec = lambda a: pl.BlockSpec((None, bt) + a.shape[2:], lambda i: (l, i, 0, 0))
    return pl.pallas_call(
        _mixer_sample_kernel,
        grid=(nb // bt,),
        in_specs=[pl.BlockSpec((bt, z.shape[1]), rows), pl.BlockSpec((bt, LANES), rows),
                  state_spec(pa), state_spec(s), state_spec(pc)] + [_full(c.shape) for c in consts],
        out_specs=[pl.BlockSpec((bt, 4 * g), rows), state_spec(pa), state_spec(s), state_spec(pc),
                   pl.BlockSpec((bt, g), rows)],
        out_shape=[
            jax.ShapeDtypeStruct((nb, 4 * g), BF16),
            jax.ShapeDtypeStruct(pa.shape, F32),
            jax.ShapeDtypeStruct(s.shape, F32),
            jax.ShapeDtypeStruct(pc.shape, F32),
            jax.ShapeDtypeStruct((nb, g), F32),
        ],
        input_output_aliases={2: 1, 3: 2, 4: 3},
        scratch_shapes=[pltpu.VMEM((bt, g), F32)],
        compiler_params=pltpu.CompilerParams(
            dimension_semantics=("arbitrary",), vmem_limit_bytes=VMEM_LIMIT),
    )(z, zlr, pa, s, pc, *consts)


def _largest_tile(m, cap, mult):
    t = min(m, cap)
    while m % t or t % mult:
        t -= mult
    return t


def _layer_weights(l, g, conv_a_w, gla_a2, gla_a_bias, gla_norm, conv_c_w, conv_c_b,
                   ln_c_g, ln_c_b, ln_d_g, ln_d_b, sg_w, sg_b):
    dv = g // N_HEADS
    row = lambda v: v.reshape(1, -1)
    return {
        "caw": conv_a_w[l],
        "a2": jnp.pad(gla_a2[l], ((0, LANES - GLA_LOWRANK), (0, 0))).astype(BF16),
        "ab": row(gla_a_bias[l]), "gn": row(gla_norm[l]),
        "ccw": conv_c_w[l], "ccb": row(conv_c_b[l]),
        "lcg": row(ln_c_g[l]), "lcb": row(ln_c_b[l]), "ldg": row(ln_d_g[l]), "ldb": row(ln_d_b[l]),
        "sgw": sg_w[l],
        "sgb": jnp.repeat(sg_b[l].T, dv, axis=1),
        "sgw0": row(jnp.repeat(sg_w[l][:, 0, 0], dv)),
        "sgb0": row(jnp.repeat(sg_b[l][:, 0], dv)),
    }


def _gla_constants(g):
    qk, dv = g // 2, g // N_HEADS
    r = jnp.arange(SG_CHUNK)
    tri = ((r[:, None] >= r[None, :]) & (r[:, None] // GLA_CHUNK == r[None, :] // GLA_CHUNK)).astype(BF16)
    e2 = (jnp.arange(qk)[:, None] // (qk // N_HEADS) == jnp.arange(g)[None, :] // dv).astype(BF16)
    return tri, e2


def kernel(x_prompt, x_sample, state_conv_a, state_gla, state_conv_c, norm_mix, w_in, conv_a_w, gla_a2, gla_a_bias,
           gla_norm, conv_c_w, conv_c_b, ln_c_g, ln_c_b, ln_d_g, ln_d_b, sg_w, sg_b, w_o, norm_ffn, w_gate, w_up,
           w_down, norm_final):
    batch, seq, d = x_prompt.shape
    nb, dec_seq, _ = x_sample.shape
    depth = w_in.shape[0]
    g = d // 4
    qk, dv = g // 2, g // N_HEADS
    assert dec_seq == 1 and seq % SG_CHUNK == 0 and g % (N_HEADS * LANES) == 0
    dff = w_gate.shape[2]
    n_main = w_in.shape[2] - GLA_LOWRANK
    tri, e2 = _gla_constants(g)

    lr0 = 3 * g + 2 * qk + 2 * g
    w_main = jnp.concatenate([w_in[:, :, :lr0], w_in[:, :, lr0 + GLA_LOWRANK:]], axis=2).astype(BF16)
    w_lr = jnp.pad(w_in[:, :, lr0:lr0 + GLA_LOWRANK], ((0, 0), (0, 0), (0, LANES - GLA_LOWRANK))).astype(BF16)

    mp = batch * seq
    tm_p = _largest_tile(mp, 1024, 16)
    tm_s = nb
    tn_in = _largest_tile(n_main, 1024, 256)
    tn_o = _largest_tile(d, 512, 256)
    tf = _largest_tile(dff, 512, 256)

    xp = x_prompt.reshape(mp, d)
    xs = x_sample.reshape(nb, d)
    st_a, st_c = state_conv_a, state_conv_c
    st_s = state_gla.reshape(depth, nb, qk, dv)
    outs_p = {k: [] for k in ("a", "s", "c", "v")}
    sv_s = []
    for l in range(depth):
        wts = _layer_weights(l, g, conv_a_w, gla_a2, gla_a_bias, gla_norm, conv_c_w, conv_c_b,
                             ln_c_g, ln_c_b, ln_d_g, ln_d_b, sg_w, sg_b)
        wts["tri"], wts["e2"] = tri, e2
        gm, gf = norm_mix[l].reshape(1, d), norm_ffn[l].reshape(1, d)
        gfin = norm_final.reshape(1, d)
        final = l == depth - 1

        z, zlr = _norm_proj(xp, gm, w_main, w_lr, l, tm_p, tn_in)
        y, na, s_new, nc, sv = _mixer_prompt(z, zlr, batch, seq, wts)
        xp = _proj_res(y, w_o, xp, l, tm_p, tn_o)
        a = _ffn_up(xp, gf, w_gate, w_up, l, tm_p, tf)
        xp = _ffn_down(a, w_down, xp, gfin, l, tm_p, tf, final)
        outs_p["a"].append(na)
        outs_p["s"].append(s_new.reshape(batch, N_HEADS, qk // N_HEADS, dv))
        outs_p["c"].append(nc)
        outs_p["v"].append(sv)

        z, zlr = _norm_proj(xs, gm, w_main, w_lr, l, tm_s, tn_in)
        y, st_a, st_s, st_c, sv = _mixer_sample(z, zlr, st_a, st_s, st_c, wts, l, 32)
        xs = _proj_res(y, w_o, xs, l, tm_s, tn_o)
        a = _ffn_up(xs, gf, w_gate, w_up, l, tm_s, tf)
        xs = _ffn_down(a, w_down, xs, gfin, l, tm_s, tf, final)
        sv_s.append(sv.reshape(nb, 1, g))

    st = jnp.stack
    return (xp.reshape(batch, seq, d), xs.reshape(nb, 1, d),
            st(outs_p["a"]), st_a, st(outs_p["s"]), st_s.reshape(depth, nb, N_HEADS, qk // N_HEADS, dv),
            st(outs_p["c"]), st_c, st(outs_p["v"]), st(sv_s))
```

```python
import functools
import math

import jax
import jax.numpy as jnp
from jax import lax
from jax.experimental import pallas as pl
from jax.experimental.pallas import tpu as pltpu

F32 = jnp.float32
BF16 = jnp.bfloat16
EPS = 1e-6

N_HEADS = 4
GLA_LOWRANK = 16
GLA_TAU = 16.0
SC_WIDTH = 3
CF_WIDTH = 31
SG_CHUNK = 128
GLA_CHUNK = 64
GLA_SUB = 16
FRONT_ROWS = 256
LANES = 128
VMEM_LIMIT = 56 * 1024 * 1024


def _dot(a, b):
    return jnp.dot(a, b, preferred_element_type=F32)


def _dot_nt(a, b):
    return lax.dot_general(a, b, (((1,), (1,)), ((), ())), preferred_element_type=F32)


def _dot_tn(a, b):
    return lax.dot_general(a, b, (((0,), (0,)), ((), ())), preferred_element_type=F32)


def _split_bf16(x):
    hi = x.astype(BF16)
    lo = (x - hi.astype(F32)).astype(BF16)
    return hi, lo


def _rmsnorm(x, g):
    return x * lax.rsqrt(jnp.mean(x * x, axis=-1, keepdims=True) + EPS) * g


def _layernorm(x, g, b):
    mu = jnp.mean(x, axis=-1, keepdims=True)
    xc = x - mu
    return xc * lax.rsqrt(jnp.mean(xc * xc, axis=-1, keepdims=True) + EPS) * g + b


def _sigmoid(x):
    return 1.0 / (1.0 + jnp.exp(-x))


def _silu(x):
    return x * _sigmoid(x)


def _gelu_tanh(x):
    c = math.sqrt(2.0 / math.pi)
    return 0.5 * x * (1.0 + jnp.tanh(c * (x + 0.044715 * (x * x * x))))


def _log_sigmoid(x):
    return jnp.minimum(x, 0.0) - jnp.log(1.0 + jnp.exp(-jnp.abs(x)))


def _norm_proj_kernel(x_ref, g_ref, w_ref, wlr_ref, z_ref, zlr_ref, h_ref):
    @pl.when(pl.program_id(1) == 0)
    def _():
        h = _rmsnorm(x_ref[...], g_ref[...]).astype(BF16)
        h_ref[...] = h
        zlr_ref[...] = _dot(h, wlr_ref[...])

    z_ref[...] = _dot(h_ref[...], w_ref[...])


def _norm_proj(x, g, w, wlr, l, tm, tn):
    m, d = x.shape
    n = w.shape[2]
    return pl.pallas_call(
        _norm_proj_kernel,
        grid=(m // tm, n // tn),
        in_specs=[
            pl.BlockSpec((tm, d), lambda i, j: (i, 0)),
            pl.BlockSpec((1, d), lambda i, j: (0, 0)),
            pl.BlockSpec((None, d, tn), lambda i, j: (l, 0, j)),
            pl.BlockSpec((None, d, LANES), lambda i, j: (l, 0, 0)),
        ],
        out_specs=[
            pl.BlockSpec((tm, tn), lambda i, j: (i, j)),
            pl.BlockSpec((tm, LANES), lambda i, j: (i, 0)),
        ],
        out_shape=[jax.ShapeDtypeStruct((m, n), F32), jax.ShapeDtypeStruct((m, LANES), F32)],
        scratch_shapes=[pltpu.VMEM((tm, d), BF16)],
        compiler_params=pltpu.CompilerParams(
            dimension_semantics=("arbitrary", "arbitrary"), vmem_limit_bytes=VMEM_LIMIT),
    )(x, g, w, wlr)


def _proj_res_kernel(y_ref, w_ref, x_ref, o_ref):
    o_ref[...] = x_ref[...] + _dot(y_ref[...], w_ref[...].astype(BF16))


def _proj_res(y, w, x, l, tm, tn):
    m, k = y.shape
    n = w.shape[2]
    return pl.pallas_call(
        _proj_res_kernel,
        grid=(m // tm, n // tn),
        in_specs=[
            pl.BlockSpec((tm, k), lambda i, j: (i, 0)),
            pl.BlockSpec((None, k, tn), lambda i, j: (l, 0, j)),
            pl.BlockSpec((tm, tn), lambda i, j: (i, j)),
        ],
        out_specs=pl.BlockSpec((tm, tn), lambda i, j: (i, j)),
        out_shape=jax.ShapeDtypeStruct((m, n), F32),
        compiler_params=pltpu.CompilerParams(
            dimension_semantics=("arbitrary", "arbitrary"), vmem_limit_bytes=VMEM_LIMIT),
    )(y, w, x)


def _ffn_up_kernel(x_ref, g_ref, wg_ref, wu_ref, o_ref, h_ref):
    @pl.when(pl.program_id(1) == 0)
    def _():
        h_ref[...] = _rmsnorm(x_ref[...], g_ref[...]).astype(BF16)

    h = h_ref[...]
    a = _dot(h, wg_ref[...].astype(BF16))
    b = _dot(h, wu_ref[...].astype(BF16))
    o_ref[...] = (_silu(a) * b).astype(BF16)


def _ffn_up(x, g, wg, wu, l, tm, tf):
    m, d = x.shape
    f = wg.shape[2]
    return pl.pallas_call(
        _ffn_up_kernel,
        grid=(m // tm, f // tf),
        in_specs=[
            pl.BlockSpec((tm, d), lambda i, j: (i, 0)),
            pl.BlockSpec((1, d), lambda i, j: (0, 0)),
            pl.BlockSpec((None, d, tf), lambda i, j: (l, 0, j)),
            pl.BlockSpec((None, d, tf), lambda i, j: (l, 0, j)),
        ],
        out_specs=pl.BlockSpec((tm, tf), lambda i, j: (i, j)),
        out_shape=jax.ShapeDtypeStruct((m, f), BF16),
        scratch_shapes=[pltpu.VMEM((tm, d), BF16)],
        compiler_params=pltpu.CompilerParams(
            dimension_semantics=("arbitrary", "arbitrary"), vmem_limit_bytes=VMEM_LIMIT),
    )(x, g, wg, wu)


def _ffn_down_kernel(a_ref, w_ref, x_ref, g_ref, o_ref, *, final_norm):
    k = pl.program_id(1)

    @pl.when(k == 0)
    def _():
        o_ref[...] = x_ref[...]

    o_ref[...] += _dot(a_ref[...], w_ref[...].astype(BF16))

    if final_norm:
        @pl.when(k == pl.num_programs(1) - 1)
        def _():
            o_ref[...] = _rmsnorm(o_ref[...], g_ref[...])


def _ffn_down(a, w, x, g, l, tm, tk, final_norm):
    m, f = a.shape
    d = w.shape[2]
    return pl.pallas_call(
        functools.partial(_ffn_down_kernel, final_norm=final_norm),
        grid=(m // tm, f // tk),
        in_specs=[
            pl.BlockSpec((tm, tk), lambda i, k: (i, k)),
            pl.BlockSpec((None, tk, d), lambda i, k: (l, k, 0)),
            pl.BlockSpec((tm, d), lambda i, k: (i, 0)),
            pl.BlockSpec((1, d), lambda i, k: (0, 0)),
        ],
        out_specs=pl.BlockSpec((tm, d), lambda i, k: (i, 0)),
        out_shape=jax.ShapeDtypeStruct((m, d), F32),
        compiler_params=pltpu.CompilerParams(
            dimension_semantics=("arbitrary", "arbitrary"), vmem_limit_bytes=VMEM_LIMIT),
    )(a, w, x, g)


def _gla_gate(zlr, a2_ref, ab_ref):
    xg = _dot(zlr.astype(BF16), a2_ref[...]) + ab_ref[...]
    return _log_sigmoid(xg) * (1.0 / GLA_TAU)


def _head_masks(qk):
    lane_head = lax.broadcasted_iota(jnp.int32, (1, qk), 1) // (qk // N_HEADS)
    return [(lane_head == h).astype(F32) for h in range(N_HEADS)]


def _gla_chunk(qc, kc, vc, bc, la_hi, la_lo, s_flat, e2, hm):
    c, qk = qc.shape
    dv = vc.shape[1] // N_HEADS
    dk = qk // N_HEADS
    nb = c // GLA_SUB
    vcb = vc.astype(BF16)
    sub_row = lax.broadcasted_iota(jnp.int32, (GLA_SUB, 1), 0)
    col_j = lax.broadcasted_iota(jnp.int32, (1, c), 1)

    o_diag = []
    for blk in range(nb):
        r0 = blk * GLA_SUB
        q_b, k_b, b_b = qc[r0:r0 + GLA_SUB], kc[r0:r0 + GLA_SUB], bc[r0:r0 + GLA_SUB]
        v_b = vc[r0:r0 + GLA_SUB]
        ts = []
        for j in range(GLA_SUB):
            dec = jnp.exp(jnp.minimum(b_b - b_b[j:j + 1], 0.0))
            ts.append(jnp.where(sub_row >= j, q_b * k_b[j:j + 1] * dec, 0.0).astype(BF16))
        r = _dot(jnp.concatenate(ts, axis=0), e2)
        acc = r[0:GLA_SUB] * v_b[0:1]
        for j in range(1, GLA_SUB):
            acc = acc + r[j * GLA_SUB:(j + 1) * GLA_SUB] * v_b[j:j + 1]
        o_diag.append(acc)
    o = jnp.concatenate(o_diag, axis=0)

    bref = [None] + [bc[blk * GLA_SUB - 1:blk * GLA_SUB] for blk in range(1, nb)]
    bref_rows = jnp.concatenate(
        [jnp.zeros((GLA_SUB, qk), F32)] + [jnp.broadcast_to(bref[blk], (GLA_SUB, qk)) for blk in range(1, nb)],
        axis=0)
    qt = qc * jnp.exp(bc - bref_rows)
    att_blk = [None]
    for blk in range(1, nb):
        kt = (kc * jnp.exp(jnp.minimum(bref[blk] - bc, 0.0))).astype(BF16)
        q_b = qt[blk * GLA_SUB:(blk + 1) * GLA_SUB]
        qm = jnp.concatenate([q_b * hm[h] for h in range(N_HEADS)], axis=0).astype(BF16)
        a = _dot_nt(qm, kt)
        att_blk.append(jnp.where(col_j < blk * GLA_SUB, a, 0.0))
    o_off = []
    for h in range(N_HEADS):
        att = jnp.concatenate(
            [jnp.zeros((GLA_SUB, c), F32)] + [att_blk[blk][h * GLA_SUB:(h + 1) * GLA_SUB] for blk in range(1, nb)],
            axis=0)
        o_off.append(_dot(att.astype(BF16), vcb[:, h * dv:(h + 1) * dv]))
    o = o + jnp.concatenate(o_off, axis=1)

    qe = qc * jnp.exp(bc)
    qe_m = jnp.concatenate([qe * hm[h] for h in range(N_HEADS)], axis=0).astype(BF16)
    o_int = _dot(qe_m, s_flat.astype(BF16))
    o = o + jnp.concatenate([o_int[h * c:(h + 1) * c] for h in range(N_HEADS)], axis=1)

    kk = (kc * jnp.exp(bc[c - 1:c] - bc)).astype(BF16)
    kv = _dot_tn(kk, vcb)
    kv_d = jnp.concatenate([kv[h * dk:(h + 1) * dk, h * dv:(h + 1) * dv] for h in range(N_HEADS)], axis=0)
    ones = jnp.ones((c, dv), BF16)
    decay = jnp.exp(_dot_tn(la_hi, ones) + _dot_tn(la_lo, ones))
    return o, decay * s_flat + kv_d


def _gla_out(o, gn, r):
    dv = o.shape[1] // N_HEADS
    parts = []
    for h in range(N_HEADS):
        oh = o[:, h * dv:(h + 1) * dv]
        parts.append(oh * lax.rsqrt(jnp.mean(oh * oh, axis=-1, keepdims=True) + EPS))
    return jnp.concatenate(parts, axis=1) * gn * _silu(r)


def _cols(g):
    qk = g // 2
    c = {}
    off = 0
    for name, width in (("a_x", g), ("a_b", g), ("a_c", g), ("q", qk), ("k", qk), ("v", g), ("r", g),
                        ("c_val", g), ("c_gate", g), ("d_u", g), ("d_v", g)):
        c[name] = (off, off + width)
        off += width
    return c, off


def _conv_taps(ccw_ref, cbuf, zbuf, tl, head):
    sub = 8
    base = head - (CF_WIDTH - 1)
    y = None
    for r in range(sub):
        taps = [k for k in range(CF_WIDTH) if (base + k) % sub == r]
        rows = tl if r == 0 else tl + sub
        part = None
        for k in taps:
            off = base + k - r
            term = ccw_ref[k:k + 1, :] * cbuf[off:off + rows, :]
            part = term if part is None else part + term
        if r == 0:
            shifted = part
        else:
            zbuf[0:rows, :] = part
            shifted = zbuf[r:r + tl, :]
        y = shifted if y is None else y + shifted
    return y


def _front_kernel(x_ref, gm_ref, wm_ref, wlr_ref, caw_ref, a2_ref, ab_ref, gn_ref, ccw_ref, ccb_ref,
                  lcg_ref, lcb_ref, ldg_ref, ldb_ref, sgw_ref, sgb_ref, tri_ref, e2_ref, wo_ref,
                  o_ref, na_ref, s_ref, nc_ref, sv_ref, abuf, cbuf, zbuf):
    t = pl.program_id(1)
    last = pl.num_programs(1) - 1
    tl = x_ref.shape[0]
    g = gn_ref.shape[1]
    qk = g // 2
    dv = g // N_HEADS
    cols, _ = _cols(g)
    a_head, c_head = 8, 32

    @pl.when(t == 0)
    def _():
        abuf[0:a_head, :] = jnp.zeros((a_head, g), F32)
        cbuf[0:c_head, :] = jnp.zeros((c_head, g), F32)
        s_ref[...] = jnp.zeros(s_ref.shape, F32)

    x = x_ref[...]
    h = _rmsnorm(x, gm_ref[...]).astype(BF16)

    def proj(*names):
        lo, hi = cols[names[0]][0], cols[names[-1]][1]
        zz = _dot(h, wm_ref[:, lo:hi])
        return [zz[:, cols[n][0] - lo:cols[n][1] - lo] for n in names]

    def out_proj(y, idx):
        return _dot(y.astype(BF16), wo_ref[idx * g:(idx + 1) * g, :])

    c_val, c_gate = proj("c_val", "c_gate")
    cin = c_val * _sigmoid(c_gate)
    cbuf[c_head:c_head + tl, :] = cin
    conv_c = _conv_taps(ccw_ref, cbuf, zbuf, tl, c_head) + ccb_ref[...]
    acc = x + out_proj(_silu(_layernorm(conv_c, lcg_ref[...], lcb_ref[...])), 2)

    a_x, a_b, a_c = proj("a_x", "a_b", "a_c")
    xa = a_c * a_x
    abuf[a_head:a_head + tl, :] = xa
    conv = caw_ref[SC_WIDTH - 1:SC_WIDTH, :] * xa
    for kk in range(SC_WIDTH - 1):
        sh = SC_WIDTH - 1 - kk
        conv = conv + caw_ref[kk:kk + 1, :] * abuf[a_head - sh:a_head - sh + tl, :]
    acc = acc + out_proj(a_b * conv, 0)

    d_u, d_v = proj("d_u", "d_v")
    du = _gelu_tanh(d_u)
    vd = _layernorm(_gelu_tanh(d_v), ldg_ref[...], ldb_ref[...])
    sv_ref[...] = vd[tl - SG_CHUNK:tl]
    vdb = vd.astype(BF16)
    row = lax.broadcasted_iota(jnp.int32, (SG_CHUNK, SG_CHUNK), 0)
    col = lax.broadcasted_iota(jnp.int32, (SG_CHUNK, SG_CHUNK), 1)
    ws = [jnp.where(row >= col, sgw_ref[hh], 0.0).astype(BF16) for hh in range(N_HEADS)]
    sv_rows = []
    for ci in range(tl // SG_CHUNK):
        sl = slice(ci * SG_CHUNK, (ci + 1) * SG_CHUNK)
        sv_rows.append(jnp.concatenate(
            [_dot(ws[hh], vdb[sl, hh * dv:(hh + 1) * dv]) for hh in range(N_HEADS)], axis=1) + sgb_ref[...])
    acc = acc + out_proj(du * jnp.concatenate(sv_rows, axis=0), 3)

    q_all, k_all, v_all, r_all = proj("q", "k", "v", "r")
    la = _gla_gate(_dot(h, wlr_ref[...]), a2_ref, ab_ref)
    la_hi, la_lo = _split_bf16(la)
    tri = tri_ref[...]
    b_all = _dot(tri, la_hi) + _dot(tri, la_lo)
    q_all = q_all * (float(qk // N_HEADS) ** -0.5)
    hm = _head_masks(qk)
    e2 = e2_ref[...]
    s_flat = s_ref[...]
    outs = []
    for ci in range(tl // GLA_CHUNK):
        sl = slice(ci * GLA_CHUNK, (ci + 1) * GLA_CHUNK)
        o_c, s_flat = _gla_chunk(q_all[sl], k_all[sl], v_all[sl], b_all[sl], la_hi[sl], la_lo[sl], s_flat, e2, hm)
        outs.append(o_c)
    s_ref[...] = s_flat
    acc = acc + out_proj(_gla_out(jnp.concatenate(outs, axis=0), gn_ref[...], r_all), 1)
    o_ref[...] = acc

    @pl.when(t == last)
    def _():
        na_ref[...] = abuf[a_head + tl - (SC_WIDTH - 1):a_head + tl, :]
        nc_ref[...] = cbuf[c_head + tl - (CF_WIDTH - 1):c_head + tl, :]

    abuf[0:a_head, :] = abuf[tl:tl + a_head, :]
    cbuf[0:c_head, :] = cbuf[tl:tl + c_head, :]


def _full(shape):
    nd = len(shape)
    return pl.BlockSpec(shape, lambda *_: (0,) * nd)


def _resident(shape, l):
    return pl.BlockSpec((None,) + shape, lambda *_: (l,) + (0,) * len(shape), pipeline_mode=pl.Buffered(1))


def _front(x, gm, w_main, w_lr, w_o, l, batch, seq, wts, tl):
    d = x.shape[1]
    g = wts["gn"].shape[1]
    nt = seq // tl
    row_map = lambda b, t: (b * nt + t, 0)
    names = ("caw", "a2", "ab", "gn", "ccw", "ccb", "lcg", "lcb", "ldg", "ldb", "sgw", "sgb", "tri", "e2")
    consts = [wts[n] for n in names]
    return pl.pallas_call(
        _front_kernel,
        grid=(batch, nt),
        in_specs=[pl.BlockSpec((tl, d), row_map), _full(gm.shape),
                  _resident(w_main.shape[1:], l), _resident(w_lr.shape[1:], l)]
                 + [_full(c.shape) for c in consts] + [_resident(w_o.shape[1:], l)],
        out_specs=[
            pl.BlockSpec((tl, d), row_map),
            pl.BlockSpec((None, SC_WIDTH - 1, g), lambda b, t: (b, 0, 0)),
            pl.BlockSpec((None, g // 2, g // N_HEADS), lambda b, t: (b, 0, 0)),
            pl.BlockSpec((None, CF_WIDTH - 1, g), lambda b, t: (b, 0, 0)),
            pl.BlockSpec((None, SG_CHUNK, g), lambda b, t: (b, 0, 0)),
        ],
        out_shape=[
            jax.ShapeDtypeStruct(x.shape, F32),
            jax.ShapeDtypeStruct((batch, SC_WIDTH - 1, g), F32),
            jax.ShapeDtypeStruct((batch, g // 2, g // N_HEADS), F32),
            jax.ShapeDtypeStruct((batch, CF_WIDTH - 1, g), F32),
            jax.ShapeDtypeStruct((batch, SG_CHUNK, g), F32),
        ],
        scratch_shapes=[pltpu.VMEM((8 + tl, g), F32), pltpu.VMEM((32 + tl, g), F32), pltpu.VMEM((8 + tl, g), F32)],
        compiler_params=pltpu.CompilerParams(
            dimension_semantics=("arbitrary", "arbitrary"), vmem_limit_bytes=VMEM_LIMIT),
    )(x, gm, w_main, w_lr, *consts, w_o)


def _mixer_sample_kernel(z_ref, zlr_ref, pa_ref, s_ref, pc_ref, caw_ref, a2_ref, ab_ref, gn_ref, ccw_ref, ccb_ref,
                         lcg_ref, lcb_ref, ldg_ref, ldb_ref, sgw0_ref, sgb0_ref,
                         y_ref, na_ref, so_ref, nc_ref, sv_ref, o_scr):
    bt = z_ref.shape[0]
    g = gn_ref.shape[1]
    qk = g // 2
    dk = qk // N_HEADS
    dv = g // N_HEADS
    cols, _ = _cols(g)

    def zc(name):
        lo, hi = cols[name]
        return z_ref[:, lo:hi]

    xa = zc("a_c") * zc("a_x")
    conv = caw_ref[SC_WIDTH - 1:SC_WIDTH, :] * xa
    for kk in range(SC_WIDTH - 1):
        conv = conv + caw_ref[kk:kk + 1, :] * pa_ref[:, kk, :]
    y_ref[:, 0:g] = (zc("a_b") * conv).astype(BF16)
    for kk in range(1, SC_WIDTH - 1):
        na_ref[:, kk - 1, :] = pa_ref[:, kk, :]
    na_ref[:, SC_WIDTH - 2, :] = xa

    la = _gla_gate(zlr_ref[...], a2_ref, ab_ref)
    a_t = jnp.exp(la).T
    k_t = zc("k").T
    q_rows = zc("q") * (float(dk) ** -0.5)
    v_rows = zc("v")
    hm = _head_masks(qk)
    hm_rows = jnp.concatenate(hm + [jnp.zeros_like(hm[0])] * (8 - N_HEADS), axis=0)
    for b in range(bt):
        a_col = jnp.broadcast_to(a_t[:, b:b + 1], (qk, dv))
        k_col = jnp.broadcast_to(k_t[:, b:b + 1], (qk, dv))
        v_b = jnp.concatenate(
            [jnp.broadcast_to(v_rows[b:b + 1, h * dv:(h + 1) * dv], (dk, dv)) for h in range(N_HEADS)], axis=0)
        s_new = a_col * s_ref[b] + k_col * v_b
        so_ref[b] = s_new
        q_m = (q_rows[b:b + 1] * hm_rows).astype(BF16)
        o_b = _dot(q_m, s_new.astype(BF16))
        for h in range(N_HEADS):
            o_scr[b:b + 1, h * dv:(h + 1) * dv] = o_b[h:h + 1]
    y_ref[:, g:2 * g] = _gla_out(o_scr[...], gn_ref[...], zc("r")).astype(BF16)

    cin = zc("c_val") * _sigmoid(zc("c_gate"))
    acc = ccb_ref[...] + ccw_ref[CF_WIDTH - 1:CF_WIDTH, :] * cin
    for kk in range(CF_WIDTH - 1):
        acc = acc + ccw_ref[kk:kk + 1, :] * pc_ref[:, kk, :]
    y_ref[:, 2 * g:3 * g] = _silu(_layernorm(acc, lcg_ref[...], lcb_ref[...])).astype(BF16)
    nc_ref[:, 0:CF_WIDTH - 2, :] = pc_ref[:, 1:CF_WIDTH - 1, :]
    nc_ref[:, CF_WIDTH - 2, :] = cin

    du = _gelu_tanh(zc("d_u"))
    vd = _layernorm(_gelu_tanh(zc("d_v")), ldg_ref[...], ldb_ref[...])
    sv_ref[...] = vd
    y_ref[:, 3 * g:4 * g] = (du * (sgw0_ref[...] * vd + sgb0_ref[...])).astype(BF16)


def _mixer_sample(z, zlr, pa, s, pc, wts, l, bt):
    nb, g = z.shape[0], wts["gn"].shape[1]
    names = ("caw", "a2", "ab", "gn", "ccw", "ccb", "lcg", "lcb", "ldg", "ldb", "sgw0", "sgb0")
    consts = [wts[n] for n in names]
    rows = lambda i: (i, 0)
    state_spec = lambda a: pl.BlockSpec((None, bt) + a.shape[2:], lambda i: (l, i, 0, 0))
    return pl.pallas_call(
        _mixer_sample_kernel,
        grid=(nb // bt,),
        in_specs=[pl.BlockSpec((bt, z.shape[1]), rows), pl.BlockSpec((bt, LANES), rows),
                  state_spec(pa), state_spec(s), state_spec(pc)] + [_full(c.shape) for c in consts],
        out_specs=[pl.BlockSpec((bt, 4 * g), rows), state_spec(pa), state_spec(s), state_spec(pc),
                   pl.BlockSpec((bt, g), rows)],
        out_shape=[
            jax.ShapeDtypeStruct((nb, 4 * g), BF16),
            jax.ShapeDtypeStruct(pa.shape, F32),
            jax.ShapeDtypeStruct(s.shape, F32),
            jax.ShapeDtypeStruct(pc.shape, F32),
            jax.ShapeDtypeStruct((nb, g), F32),
        ],
        input_output_aliases={2: 1, 3: 2, 4: 3},
        scratch_shapes=[pltpu.VMEM((bt, g), F32)],
        compiler_params=pltpu.CompilerParams(
            dimension_semantics=("arbitrary",), vmem_limit_bytes=VMEM_LIMIT),
    )(z, zlr, pa, s, pc, *consts)


def _largest_tile(m, cap, mult):
    t = min(m, cap)
    while m % t or t % mult:
        t -= mult
    return t


def _layer_weights(l, g, conv_a_w, gla_a2, gla_a_bias, gla_norm, conv_c_w, conv_c_b,
                   ln_c_g, ln_c_b, ln_d_g, ln_d_b, sg_w, sg_b):
    dv = g // N_HEADS
    row = lambda v: v.reshape(1, -1)
    return {
        "caw": conv_a_w[l],
        "a2": jnp.pad(gla_a2[l], ((0, LANES - GLA_LOWRANK), (0, 0))).astype(BF16),
        "ab": row(gla_a_bias[l]), "gn": row(gla_norm[l]),
        "ccw": conv_c_w[l], "ccb": row(conv_c_b[l]),
        "lcg": row(ln_c_g[l]), "lcb": row(ln_c_b[l]), "ldg": row(ln_d_g[l]), "ldb": row(ln_d_b[l]),
        "sgw": sg_w[l],
        "sgb": jnp.repeat(sg_b[l].T, dv, axis=1),
        "sgw0": row(jnp.repeat(sg_w[l][:, 0, 0], dv)),
        "sgb0": row(jnp.repeat(sg_b[l][:, 0], dv)),
    }


def _gla_constants(g, tl):
    qk, dv = g // 2, g // N_HEADS
    r = jnp.arange(tl)
    tri = ((r[:, None] >= r[None, :]) & (r[:, None] // GLA_CHUNK == r[None, :] // GLA_CHUNK)).astype(BF16)
    e2 = (jnp.arange(qk)[:, None] // (qk // N_HEADS) == jnp.arange(g)[None, :] // dv).astype(BF16)
    return tri, e2


def kernel(x_prompt, x_sample, state_conv_a, state_gla, state_conv_c, norm_mix, w_in, conv_a_w, gla_a2, gla_a_bias,
           gla_norm, conv_c_w, conv_c_b, ln_c_g, ln_c_b, ln_d_g, ln_d_b, sg_w, sg_b, w_o, norm_ffn, w_gate, w_up,
           w_down, norm_final):
    batch, seq, d = x_prompt.shape
    nb, dec_seq, _ = x_sample.shape
    depth = w_in.shape[0]
    g = d // 4
    qk, dv = g // 2, g // N_HEADS
    assert dec_seq == 1 and seq % SG_CHUNK == 0 and g % (N_HEADS * LANES) == 0
    dff = w_gate.shape[2]
    n_main = w_in.shape[2] - GLA_LOWRANK
    tl = _largest_tile(seq, FRONT_ROWS, SG_CHUNK)
    tri, e2 = _gla_constants(g, tl)

    lr0 = 3 * g + 2 * qk + 2 * g
    w_main = jnp.concatenate([w_in[:, :, :lr0], w_in[:, :, lr0 + GLA_LOWRANK:]], axis=2).astype(BF16)
    w_lr = jnp.pad(w_in[:, :, lr0:lr0 + GLA_LOWRANK], ((0, 0), (0, 0), (0, LANES - GLA_LOWRANK))).astype(BF16)
    w_ob = w_o.astype(BF16)

    mp = batch * seq
    tm_p = _largest_tile(mp, 1024, 16)
    tm_s = nb
    tn_in = _largest_tile(n_main, 1024, 256)
    tn_o = _largest_tile(d, 512, 256)
    tf = _largest_tile(dff, 512, 256)

    xp = x_prompt.reshape(mp, d)
    xs = x_sample.reshape(nb, d)
    st_a, st_c = state_conv_a, state_conv_c
    st_s = state_gla.reshape(depth, nb, qk, dv)
    outs_p = {k: [] for k in ("a", "s", "c", "v")}
    sv_s = []
    for l in range(depth):
        wts = _layer_weights(l, g, conv_a_w, gla_a2, gla_a_bias, gla_norm, conv_c_w, conv_c_b,
                             ln_c_g, ln_c_b, ln_d_g, ln_d_b, sg_w, sg_b)
        wts["tri"], wts["e2"] = tri, e2
        gm, gf = norm_mix[l].reshape(1, d), norm_ffn[l].reshape(1, d)
        gfin = norm_final.reshape(1, d)
        final = l == depth - 1

        xp, na, s_new, nc, sv = _front(xp, gm, w_main, w_lr, w_ob, l, batch, seq, wts, tl)
        a = _ffn_up(xp, gf, w_gate, w_up, l, tm_p, tf)
        xp = _ffn_down(a, w_down, xp, gfin, l, tm_p, tf, final)
        outs_p["a"].append(na)
        outs_p["s"].append(s_new.reshape(batch, N_HEADS, qk // N_HEADS, dv))
        outs_p["c"].append(nc)
        outs_p["v"].append(sv)

        z, zlr = _norm_proj(xs, gm, w_main, w_lr, l, tm_s, tn_in)
        y, st_a, st_s, st_c, sv = _mixer_sample(z, zlr, st_a, st_s, st_c, wts, l, 32)
        xs = _proj_res(y, w_ob, xs, l, tm_s, tn_o)
        a = _ffn_up(xs, gf, w_gate, w_up, l, tm_s, tf)
        xs = _ffn_down(a, w_down, xs, gfin, l, tm_s, tf, final)
        sv_s.append(sv.reshape(nb, 1, g))

    st = jnp.stack
    return (xp.reshape(batch, seq, d), xs.reshape(nb, 1, d),
            st(outs_p["a"]), st_a, st(outs_p["s"]), st_s.reshape(depth, nb, N_HEADS, qk // N_HEADS, dv),
            st(outs_p["c"]), st_c, st(outs_p["v"]), st(sv_s))
```

```python
import functools
import math

import jax
import jax.numpy as jnp
from jax import lax
from jax.experimental import pallas as pl
from jax.experimental.pallas import tpu as pltpu

F32 = jnp.float32
BF16 = jnp.bfloat16
EPS = 1e-6

N_HEADS = 4
GLA_LOWRANK = 16
GLA_TAU = 16.0
SC_WIDTH = 3
CF_WIDTH = 31
SG_CHUNK = 128
GLA_CHUNK = 64
GLA_SUB = 16
FRONT_ROWS = 256
LANES = 128
VMEM_LIMIT = 56 * 1024 * 1024


def _dot(a, b):
    return jnp.dot(a, b, preferred_element_type=F32)


def _dot_nt(a, b):
    return lax.dot_general(a, b, (((1,), (1,)), ((), ())), preferred_element_type=F32)


def _dot_tn(a, b):
    return lax.dot_general(a, b, (((0,), (0,)), ((), ())), preferred_element_type=F32)


def _split_bf16(x):
    hi = x.astype(BF16)
    lo = (x - hi.astype(F32)).astype(BF16)
    return hi, lo


def _rmsnorm(x, g):
    return x * lax.rsqrt(jnp.mean(x * x, axis=-1, keepdims=True) + EPS) * g


def _layernorm(x, g, b):
    mu = jnp.mean(x, axis=-1, keepdims=True)
    xc = x - mu
    return xc * lax.rsqrt(jnp.mean(xc * xc, axis=-1, keepdims=True) + EPS) * g + b


def _sigmoid(x):
    return 1.0 / (1.0 + jnp.exp(-x))


def _silu(x):
    return x * _sigmoid(x)


def _gelu_tanh(x):
    c = math.sqrt(2.0 / math.pi)
    return 0.5 * x * (1.0 + jnp.tanh(c * (x + 0.044715 * (x * x * x))))


def _log_sigmoid(x):
    return jnp.minimum(x, 0.0) - jnp.log(1.0 + jnp.exp(-jnp.abs(x)))


def _norm_proj_kernel(x_ref, g_ref, w_ref, wlr_ref, z_ref, zlr_ref, h_ref):
    @pl.when(pl.program_id(1) == 0)
    def _():
        h = _rmsnorm(x_ref[...], g_ref[...]).astype(BF16)
        h_ref[...] = h
        zlr_ref[...] = _dot(h, wlr_ref[...])

    z_ref[...] = _dot(h_ref[...], w_ref[...])


def _norm_proj(x, g, w, wlr, l, tm, tn):
    m, d = x.shape
    n = w.shape[2]
    return pl.pallas_call(
        _norm_proj_kernel,
        grid=(m // tm, n // tn),
        in_specs=[
            pl.BlockSpec((tm, d), lambda i, j: (i, 0)),
            pl.BlockSpec((1, d), lambda i, j: (0, 0)),
            pl.BlockSpec((None, d, tn), lambda i, j: (l, 0, j)),
            pl.BlockSpec((None, d, LANES), lambda i, j: (l, 0, 0)),
        ],
        out_specs=[
            pl.BlockSpec((tm, tn), lambda i, j: (i, j)),
            pl.BlockSpec((tm, LANES), lambda i, j: (i, 0)),
        ],
        out_shape=[jax.ShapeDtypeStruct((m, n), F32), jax.ShapeDtypeStruct((m, LANES), F32)],
        scratch_shapes=[pltpu.VMEM((tm, d), BF16)],
        compiler_params=pltpu.CompilerParams(
            dimension_semantics=("arbitrary", "arbitrary"), vmem_limit_bytes=VMEM_LIMIT),
    )(x, g, w, wlr)


def _proj_res_kernel(y_ref, w_ref, x_ref, g_ref, o_ref, h_ref):
    o = x_ref[...] + _dot(y_ref[...], w_ref[...])
    o_ref[...] = o
    h_ref[...] = _rmsnorm(o, g_ref[...]).astype(BF16)


def _proj_res(y, w, x, g, l, tm):
    m, k = y.shape
    n = w.shape[2]
    return pl.pallas_call(
        _proj_res_kernel,
        grid=(m // tm,),
        in_specs=[
            pl.BlockSpec((tm, k), lambda i: (i, 0)),
            pl.BlockSpec((None, k, n), lambda i: (l, 0, 0)),
            pl.BlockSpec((tm, n), lambda i: (i, 0)),
            pl.BlockSpec((1, n), lambda i: (0, 0)),
        ],
        out_specs=[pl.BlockSpec((tm, n), lambda i: (i, 0)), pl.BlockSpec((tm, n), lambda i: (i, 0))],
        out_shape=[jax.ShapeDtypeStruct((m, n), F32), jax.ShapeDtypeStruct((m, n), BF16)],
        compiler_params=pltpu.CompilerParams(
            dimension_semantics=("arbitrary",), vmem_limit_bytes=VMEM_LIMIT),
    )(y, w, x, g)


def _ffn_kernel(h_ref, x_ref, wg_ref, wu_ref, wd_ref, g_ref, o_ref, *, final_norm):
    j = pl.program_id(1)

    @pl.when(j == 0)
    def _():
        o_ref[...] = x_ref[...]

    h = h_ref[...]
    gate = _dot(h, wg_ref[...].astype(BF16))
    up = _dot(h, wu_ref[...].astype(BF16))
    o_ref[...] += _dot((_silu(gate) * up).astype(BF16), wd_ref[...].astype(BF16))

    if final_norm:
        @pl.when(j == pl.num_programs(1) - 1)
        def _():
            o_ref[...] = _rmsnorm(o_ref[...], g_ref[...])


def _ffn(h, x, wg, wu, wd, g, l, tm, tf, final_norm):
    m, d = x.shape
    f = wg.shape[2]
    return pl.pallas_call(
        functools.partial(_ffn_kernel, final_norm=final_norm),
        grid=(m // tm, f // tf),
        in_specs=[
            pl.BlockSpec((tm, d), lambda i, j: (i, 0)),
            pl.BlockSpec((tm, d), lambda i, j: (i, 0), pipeline_mode=pl.Buffered(1)),
            pl.BlockSpec((None, d, tf), lambda i, j: (l, 0, j)),
            pl.BlockSpec((None, d, tf), lambda i, j: (l, 0, j)),
            pl.BlockSpec((None, tf, d), lambda i, j: (l, j, 0)),
            pl.BlockSpec((1, d), lambda i, j: (0, 0)),
        ],
        out_specs=pl.BlockSpec((tm, d), lambda i, j: (i, 0)),
        out_shape=jax.ShapeDtypeStruct((m, d), F32),
        compiler_params=pltpu.CompilerParams(
            dimension_semantics=("arbitrary", "arbitrary"), vmem_limit_bytes=VMEM_LIMIT),
    )(h, x, wg, wu, wd, g)


def _gla_gate(zlr, a2_ref, ab_ref):
    xg = _dot(zlr.astype(BF16), a2_ref[...]) + ab_ref[...]
    return _log_sigmoid(xg) * (1.0 / GLA_TAU)


def _head_masks(qk):
    lane_head = lax.broadcasted_iota(jnp.int32, (1, qk), 1) // (qk // N_HEADS)
    return [(lane_head == h).astype(F32) for h in range(N_HEADS)]


def _gla_chunk(qc, kc, vc, bc, la_hi, la_lo, s_flat, e2, hm):
    c, qk = qc.shape
    dv = vc.shape[1] // N_HEADS
    dk = qk // N_HEADS
    nb = c // GLA_SUB
    vcb = vc.astype(BF16)
    sub_row = lax.broadcasted_iota(jnp.int32, (GLA_SUB, 1), 0)
    col_j = lax.broadcasted_iota(jnp.int32, (1, c), 1)

    o_diag = []
    for blk in range(nb):
        r0 = blk * GLA_SUB
        q_b, k_b, b_b = qc[r0:r0 + GLA_SUB], kc[r0:r0 + GLA_SUB], bc[r0:r0 + GLA_SUB]
        v_b = vc[r0:r0 + GLA_SUB]
        ts = []
        for j in range(GLA_SUB):
            dec = jnp.exp(jnp.minimum(b_b - b_b[j:j + 1], 0.0))
            ts.append(jnp.where(sub_row >= j, q_b * k_b[j:j + 1] * dec, 0.0).astype(BF16))
        r = _dot(jnp.concatenate(ts, axis=0), e2)
        acc = r[0:GLA_SUB] * v_b[0:1]
        for j in range(1, GLA_SUB):
            acc = acc + r[j * GLA_SUB:(j + 1) * GLA_SUB] * v_b[j:j + 1]
        o_diag.append(acc)
    o = jnp.concatenate(o_diag, axis=0)

    bref = [None] + [bc[blk * GLA_SUB - 1:blk * GLA_SUB] for blk in range(1, nb)]
    bref_rows = jnp.concatenate(
        [jnp.zeros((GLA_SUB, qk), F32)] + [jnp.broadcast_to(bref[blk], (GLA_SUB, qk)) for blk in range(1, nb)],
        axis=0)
    qt = qc * jnp.exp(bc - bref_rows)
    att_blk = [None]
    for blk in range(1, nb):
        kt = (kc * jnp.exp(jnp.minimum(bref[blk] - bc, 0.0))).astype(BF16)
        q_b = qt[blk * GLA_SUB:(blk + 1) * GLA_SUB]
        qm = jnp.concatenate([q_b * hm[h] for h in range(N_HEADS)], axis=0).astype(BF16)
        a = _dot_nt(qm, kt)
        att_blk.append(jnp.where(col_j < blk * GLA_SUB, a, 0.0))
    o_off = []
    for h in range(N_HEADS):
        att = jnp.concatenate(
            [jnp.zeros((GLA_SUB, c), F32)] + [att_blk[blk][h * GLA_SUB:(h + 1) * GLA_SUB] for blk in range(1, nb)],
            axis=0)
        o_off.append(_dot(att.astype(BF16), vcb[:, h * dv:(h + 1) * dv]))
    o = o + jnp.concatenate(o_off, axis=1)

    qe = qc * jnp.exp(bc)
    qe_m = jnp.concatenate([qe * hm[h] for h in range(N_HEADS)], axis=0).astype(BF16)
    o_int = _dot(qe_m, s_flat.astype(BF16))
    o = o + jnp.concatenate([o_int[h * c:(h + 1) * c] for h in range(N_HEADS)], axis=1)

    kk = (kc * jnp.exp(bc[c - 1:c] - bc)).astype(BF16)
    kv = _dot_tn(kk, vcb)
    kv_d = jnp.concatenate([kv[h * dk:(h + 1) * dk, h * dv:(h + 1) * dv] for h in range(N_HEADS)], axis=0)
    ones = jnp.ones((c, dv), BF16)
    decay = jnp.exp(_dot_tn(la_hi, ones) + _dot_tn(la_lo, ones))
    return o, decay * s_flat + kv_d


def _gla_out(o, gn, r):
    dv = o.shape[1] // N_HEADS
    parts = []
    for h in range(N_HEADS):
        oh = o[:, h * dv:(h + 1) * dv]
        parts.append(oh * lax.rsqrt(jnp.mean(oh * oh, axis=-1, keepdims=True) + EPS))
    return jnp.concatenate(parts, axis=1) * gn * _silu(r)


def _cols(g):
    qk = g // 2
    c = {}
    off = 0
    for name, width in (("a_x", g), ("a_b", g), ("a_c", g), ("q", qk), ("k", qk), ("v", g), ("r", g),
                        ("c_val", g), ("c_gate", g), ("d_u", g), ("d_v", g)):
        c[name] = (off, off + width)
        off += width
    return c, off


def _conv_taps(ccw_ref, cbuf, zbuf, tl, head):
    sub = 8
    base = head - (CF_WIDTH - 1)
    y = None
    for r in range(sub):
        taps = [k for k in range(CF_WIDTH) if (base + k) % sub == r]
        rows = tl if r == 0 else tl + sub
        part = None
        for k in taps:
            off = base + k - r
            term = ccw_ref[k:k + 1, :] * cbuf[off:off + rows, :]
            part = term if part is None else part + term
        if r == 0:
            shifted = part
        else:
            zbuf[0:rows, :] = part
            shifted = zbuf[r:r + tl, :]
        y = shifted if y is None else y + shifted
    return y


def _front_kernel(x_ref, gm_ref, wm_ref, wlr_ref, caw_ref, a2_ref, ab_ref, gn_ref, ccw_ref, ccb_ref,
                  lcg_ref, lcb_ref, ldg_ref, ldb_ref, sgw_ref, sgb_ref, tri_ref, e2_ref, wo_ref, gf_ref,
                  o_ref, hn_ref, na_ref, s_ref, nc_ref, sv_ref, abuf, cbuf, zbuf):
    t = pl.program_id(1)
    last = pl.num_programs(1) - 1
    tl = x_ref.shape[0]
    g = gn_ref.shape[1]
    qk = g // 2
    dv = g // N_HEADS
    cols, _ = _cols(g)
    a_head, c_head = 8, 32

    @pl.when(t == 0)
    def _():
        abuf[0:a_head, :] = jnp.zeros((a_head, g), F32)
        cbuf[0:c_head, :] = jnp.zeros((c_head, g), F32)
        s_ref[...] = jnp.zeros(s_ref.shape, F32)

    x = x_ref[...]
    h = _rmsnorm(x, gm_ref[...]).astype(BF16)

    def proj(*names):
        lo, hi = cols[names[0]][0], cols[names[-1]][1]
        zz = _dot(h, wm_ref[:, lo:hi])
        return [zz[:, cols[n][0] - lo:cols[n][1] - lo] for n in names]

    def out_proj(y, idx):
        return _dot(y.astype(BF16), wo_ref[idx * g:(idx + 1) * g, :])

    c_val, c_gate = proj("c_val", "c_gate")
    cin = c_val * _sigmoid(c_gate)
    cbuf[c_head:c_head + tl, :] = cin
    conv_c = _conv_taps(ccw_ref, cbuf, zbuf, tl, c_head) + ccb_ref[...]
    acc = x + out_proj(_silu(_layernorm(conv_c, lcg_ref[...], lcb_ref[...])), 2)

    a_x, a_b, a_c = proj("a_x", "a_b", "a_c")
    xa = a_c * a_x
    abuf[a_head:a_head + tl, :] = xa
    conv = caw_ref[SC_WIDTH - 1:SC_WIDTH, :] * xa
    for kk in range(SC_WIDTH - 1):
        sh = SC_WIDTH - 1 - kk
        conv = conv + caw_ref[kk:kk + 1, :] * abuf[a_head - sh:a_head - sh + tl, :]
    acc = acc + out_proj(a_b * conv, 0)

    d_u, d_v = proj("d_u", "d_v")
    du = _gelu_tanh(d_u)
    vd = _layernorm(_gelu_tanh(d_v), ldg_ref[...], ldb_ref[...])
    sv_ref[...] = vd[tl - SG_CHUNK:tl]
    vdb = vd.astype(BF16)
    row = lax.broadcasted_iota(jnp.int32, (SG_CHUNK, SG_CHUNK), 0)
    col = lax.broadcasted_iota(jnp.int32, (SG_CHUNK, SG_CHUNK), 1)
    ws = [jnp.where(row >= col, sgw_ref[hh], 0.0).astype(BF16) for hh in range(N_HEADS)]
    sv_rows = []
    for ci in range(tl // SG_CHUNK):
        sl = slice(ci * SG_CHUNK, (ci + 1) * SG_CHUNK)
        sv_rows.append(jnp.concatenate(
            [_dot(ws[hh], vdb[sl, hh * dv:(hh + 1) * dv]) for hh in range(N_HEADS)], axis=1) + sgb_ref[...])
    acc = acc + out_proj(du * jnp.concatenate(sv_rows, axis=0), 3)

    q_all, k_all, v_all, r_all = proj("q", "k", "v", "r")
    la = _gla_gate(_dot(h, wlr_ref[...]), a2_ref, ab_ref)
    la_hi, la_lo = _split_bf16(la)
    tri = tri_ref[...]
    b_all = _dot(tri, la_hi) + _dot(tri, la_lo)
    q_all = q_all * (float(qk // N_HEADS) ** -0.5)
    hm = _head_masks(qk)
    e2 = e2_ref[...]
    s_flat = s_ref[...]
    outs = []
    for ci in range(tl // GLA_CHUNK):
        sl = slice(ci * GLA_CHUNK, (ci + 1) * GLA_CHUNK)
        o_c, s_flat = _gla_chunk(q_all[sl], k_all[sl], v_all[sl], b_all[sl], la_hi[sl], la_lo[sl], s_flat, e2, hm)
        outs.append(o_c)
    s_ref[...] = s_flat
    acc = acc + out_proj(_gla_out(jnp.concatenate(outs, axis=0), gn_ref[...], r_all), 1)
    o_ref[...] = acc
    hn_ref[...] = _rmsnorm(acc, gf_ref[...]).astype(BF16)

    @pl.when(t == last)
    def _():
        na_ref[...] = abuf[a_head + tl - (SC_WIDTH - 1):a_head + tl, :]
        nc_ref[...] = cbuf[c_head + tl - (CF_WIDTH - 1):c_head + tl, :]

    abuf[0:a_head, :] = abuf[tl:tl + a_head, :]
    cbuf[0:c_head, :] = cbuf[tl:tl + c_head, :]


def _full(shape):
    nd = len(shape)
    return pl.BlockSpec(shape, lambda *_: (0,) * nd)


def _resident(shape, l):
    return pl.BlockSpec((None,) + shape, lambda *_: (l,) + (0,) * len(shape), pipeline_mode=pl.Buffered(1))


def _front(x, gm, w_main, w_lr, w_o, gf, l, batch, seq, wts, tl):
    d = x.shape[1]
    g = wts["gn"].shape[1]
    nt = seq // tl
    row_map = lambda b, t: (b * nt + t, 0)
    names = ("caw", "a2", "ab", "gn", "ccw", "ccb", "lcg", "lcb", "ldg", "ldb", "sgw", "sgb", "tri", "e2")
    consts = [wts[n] for n in names]
    return pl.pallas_call(
        _front_kernel,
        grid=(batch, nt),
        in_specs=[pl.BlockSpec((tl, d), row_map), _full(gm.shape),
                  _resident(w_main.shape[1:], l), _resident(w_lr.shape[1:], l)]
                 + [_full(c.shape) for c in consts] + [_resident(w_o.shape[1:], l), _full(gf.shape)],
        out_specs=[
            pl.BlockSpec((tl, d), row_map),
            pl.BlockSpec((tl, d), row_map),
            pl.BlockSpec((None, SC_WIDTH - 1, g), lambda b, t: (b, 0, 0)),
            pl.BlockSpec((None, g // 2, g // N_HEADS), lambda b, t: (b, 0, 0)),
            pl.BlockSpec((None, CF_WIDTH - 1, g), lambda b, t: (b, 0, 0)),
            pl.BlockSpec((None, SG_CHUNK, g), lambda b, t: (b, 0, 0)),
        ],
        out_shape=[
            jax.ShapeDtypeStruct(x.shape, F32),
            jax.ShapeDtypeStruct(x.shape, BF16),
            jax.ShapeDtypeStruct((batch, SC_WIDTH - 1, g), F32),
            jax.ShapeDtypeStruct((batch, g // 2, g // N_HEADS), F32),
            jax.ShapeDtypeStruct((batch, CF_WIDTH - 1, g), F32),
            jax.ShapeDtypeStruct((batch, SG_CHUNK, g), F32),
        ],
        scratch_shapes=[pltpu.VMEM((8 + tl, g), F32), pltpu.VMEM((32 + tl, g), F32), pltpu.VMEM((8 + tl, g), F32)],
        compiler_params=pltpu.CompilerParams(
            dimension_semantics=("arbitrary", "arbitrary"), vmem_limit_bytes=VMEM_LIMIT),
    )(x, gm, w_main, w_lr, *consts, w_o, gf)


def _mixer_sample_kernel(z_ref, zlr_ref, pa_ref, s_ref, pc_ref, caw_ref, a2_ref, ab_ref, gn_ref, ccw_ref, ccb_ref,
                         lcg_ref, lcb_ref, ldg_ref, ldb_ref, sgw0_ref, sgb0_ref,
                         y_ref, na_ref, so_ref, nc_ref, sv_ref, o_scr):
    bt = z_ref.shape[0]
    g = gn_ref.shape[1]
    qk = g // 2
    dk = qk // N_HEADS
    dv = g // N_HEADS
    cols, _ = _cols(g)

    def zc(name):
        lo, hi = cols[name]
        return z_ref[:, lo:hi]

    xa = zc("a_c") * zc("a_x")
    conv = caw_ref[SC_WIDTH - 1:SC_WIDTH, :] * xa
    for kk in range(SC_WIDTH - 1):
        conv = conv + caw_ref[kk:kk + 1, :] * pa_ref[:, kk, :]
    y_ref[:, 0:g] = (zc("a_b") * conv).astype(BF16)
    for kk in range(1, SC_WIDTH - 1):
        na_ref[:, kk - 1, :] = pa_ref[:, kk, :]
    na_ref[:, SC_WIDTH - 2, :] = xa

    la = _gla_gate(zlr_ref[...], a2_ref, ab_ref)
    a_t = jnp.exp(la).T
    k_t = zc("k").T
    q_rows = zc("q") * (float(dk) ** -0.5)
    v_rows = zc("v")
    hm = _head_masks(qk)
    hm_rows = jnp.concatenate(hm + [jnp.zeros_like(hm[0])] * (8 - N_HEADS), axis=0)
    for b in range(bt):
        a_col = jnp.broadcast_to(a_t[:, b:b + 1], (qk, dv))
        k_col = jnp.broadcast_to(k_t[:, b:b + 1], (qk, dv))
        v_b = jnp.concatenate(
            [jnp.broadcast_to(v_rows[b:b + 1, h * dv:(h + 1) * dv], (dk, dv)) for h in range(N_HEADS)], axis=0)
        s_new = a_col * s_ref[b] + k_col * v_b
        so_ref[b] = s_new
        q_m = (q_rows[b:b + 1] * hm_rows).astype(BF16)
        o_b = _dot(q_m, s_new.astype(BF16))
        for h in range(N_HEADS):
            o_scr[b:b + 1, h * dv:(h + 1) * dv] = o_b[h:h + 1]
    y_ref[:, g:2 * g] = _gla_out(o_scr[...], gn_ref[...], zc("r")).astype(BF16)

    cin = zc("c_val") * _sigmoid(zc("c_gate"))
    acc = ccb_ref[...] + ccw_ref[CF_WIDTH - 1:CF_WIDTH, :] * cin
    for kk in range(CF_WIDTH - 1):
        acc = acc + ccw_ref[kk:kk + 1, :] * pc_ref[:, kk, :]
    y_ref[:, 2 * g:3 * g] = _silu(_layernorm(acc, lcg_ref[...], lcb_ref[...])).astype(BF16)
    nc_ref[:, 0:CF_WIDTH - 2, :] = pc_ref[:, 1:CF_WIDTH - 1, :]
    nc_ref[:, CF_WIDTH - 2, :] = cin

    du = _gelu_tanh(zc("d_u"))
    vd = _layernorm(_gelu_tanh(zc("d_v")), ldg_ref[...], ldb_ref[...])
    sv_ref[...] = vd
    y_ref[:, 3 * g:4 * g] = (du * (sgw0_ref[...] * vd + sgb0_ref[...])).astype(BF16)


def _mixer_sample(z, zlr, pa, s, pc, wts, l, bt):
    nb, g = z.shape[0], wts["gn"].shape[1]
    names = ("caw", "a2", "ab", "gn", "ccw", "ccb", "lcg", "lcb", "ldg", "ldb", "sgw0", "sgb0")
    consts = [wts[n] for n in names]
    rows = lambda i: (i, 0)
    state_spec = lambda a: pl.BlockSpec((None, bt) + a.shape[2:], lambda i: (l, i, 0, 0))
    return pl.pallas_call(
        _mixer_sample_kernel,
        grid=(nb // bt,),
        in_specs=[pl.BlockSpec((bt, z.shape[1]), rows), pl.BlockSpec((bt, LANES), rows),
                  state_spec(pa), state_spec(s), state_spec(pc)] + [_full(c.shape) for c in consts],
        out_specs=[pl.BlockSpec((bt, 4 * g), rows), state_spec(pa), state_spec(s), state_spec(pc),
                   pl.BlockSpec((bt, g), rows)],
        out_shape=[
            jax.ShapeDtypeStruct((nb, 4 * g), BF16),
            jax.ShapeDtypeStruct(pa.shape, F32),
            jax.ShapeDtypeStruct(s.shape, F32),
            jax.ShapeDtypeStruct(pc.shape, F32),
            jax.ShapeDtypeStruct((nb, g), F32),
        ],
        input_output_aliases={2: 1, 3: 2, 4: 3},
        scratch_shapes=[pltpu.VMEM((bt, g), F32)],
        compiler_params=pltpu.CompilerParams(
            dimension_semantics=("arbitrary",), vmem_limit_bytes=VMEM_LIMIT),
    )(z, zlr, pa, s, pc, *consts)


def _largest_tile(m, cap, mult):
    t = min(m, cap)
    while m % t or t % mult:
        t -= mult
    return t


def _layer_weights(l, g, conv_a_w, gla_a2, gla_a_bias, gla_norm, conv_c_w, conv_c_b,
                   ln_c_g, ln_c_b, ln_d_g, ln_d_b, sg_w, sg_b):
    dv = g // N_HEADS
    row = lambda v: v.reshape(1, -1)
    return {
        "caw": conv_a_w[l],
        "a2": jnp.pad(gla_a2[l], ((0, LANES - GLA_LOWRANK), (0, 0))).astype(BF16),
        "ab": row(gla_a_bias[l]), "gn": row(gla_norm[l]),
        "ccw": conv_c_w[l], "ccb": row(conv_c_b[l]),
        "lcg": row(ln_c_g[l]), "lcb": row(ln_c_b[l]), "ldg": row(ln_d_g[l]), "ldb": row(ln_d_b[l]),
        "sgw": sg_w[l],
        "sgb": jnp.repeat(sg_b[l].T, dv, axis=1),
        "sgw0": row(jnp.repeat(sg_w[l][:, 0, 0], dv)),
        "sgb0": row(jnp.repeat(sg_b[l][:, 0], dv)),
    }


def _gla_constants(g, tl):
    qk, dv = g // 2, g // N_HEADS
    r = jnp.arange(tl)
    tri = ((r[:, None] >= r[None, :]) & (r[:, None] // GLA_CHUNK == r[None, :] // GLA_CHUNK)).astype(BF16)
    e2 = (jnp.arange(qk)[:, None] // (qk // N_HEADS) == jnp.arange(g)[None, :] // dv).astype(BF16)
    return tri, e2


def kernel(x_prompt, x_sample, state_conv_a, state_gla, state_conv_c, norm_mix, w_in, conv_a_w, gla_a2, gla_a_bias,
           gla_norm, conv_c_w, conv_c_b, ln_c_g, ln_c_b, ln_d_g, ln_d_b, sg_w, sg_b, w_o, norm_ffn, w_gate, w_up,
           w_down, norm_final):
    batch, seq, d = x_prompt.shape
    nb, dec_seq, _ = x_sample.shape
    depth = w_in.shape[0]
    g = d // 4
    qk, dv = g // 2, g // N_HEADS
    assert dec_seq == 1 and seq % SG_CHUNK == 0 and g % (N_HEADS * LANES) == 0
    dff = w_gate.shape[2]
    n_main = w_in.shape[2] - GLA_LOWRANK
    tl = _largest_tile(seq, FRONT_ROWS, SG_CHUNK)
    tri, e2 = _gla_constants(g, tl)

    lr0 = 3 * g + 2 * qk + 2 * g
    w_main = jnp.concatenate([w_in[:, :, :lr0], w_in[:, :, lr0 + GLA_LOWRANK:]], axis=2).astype(BF16)
    w_lr = jnp.pad(w_in[:, :, lr0:lr0 + GLA_LOWRANK], ((0, 0), (0, 0), (0, LANES - GLA_LOWRANK))).astype(BF16)
    w_ob = w_o.astype(BF16)

    mp = batch * seq
    tm_p = _largest_tile(mp, 1024, 16)
    tm_s = nb
    tn_in = _largest_tile(n_main, 1024, 256)
    tf_p = _largest_tile(dff, 256, 256)
    tf_s = _largest_tile(dff, 512, 256)

    xp = x_prompt.reshape(mp, d)
    xs = x_sample.reshape(nb, d)
    st_a, st_c = state_conv_a, state_conv_c
    st_s = state_gla.reshape(depth, nb, qk, dv)
    outs_p = {k: [] for k in ("a", "s", "c", "v")}
    sv_s = []
    for l in range(depth):
        wts = _layer_weights(l, g, conv_a_w, gla_a2, gla_a_bias, gla_norm, conv_c_w, conv_c_b,
                             ln_c_g, ln_c_b, ln_d_g, ln_d_b, sg_w, sg_b)
        wts["tri"], wts["e2"] = tri, e2
        gm, gf = norm_mix[l].reshape(1, d), norm_ffn[l].reshape(1, d)
        gfin = norm_final.reshape(1, d)
        final = l == depth - 1

        xp, hn, na, s_new, nc, sv = _front(xp, gm, w_main, w_lr, w_ob, gf, l, batch, seq, wts, tl)
        xp = _ffn(hn, xp, w_gate, w_up, w_down, gfin, l, tm_p, tf_p, final)
        outs_p["a"].append(na)
        outs_p["s"].append(s_new.reshape(batch, N_HEADS, qk // N_HEADS, dv))
        outs_p["c"].append(nc)
        outs_p["v"].append(sv)

        z, zlr = _norm_proj(xs, gm, w_main, w_lr, l, tm_s, tn_in)
        y, st_a, st_s, st_c, sv = _mixer_sample(z, zlr, st_a, st_s, st_c, wts, l, 32)
        xs, hn = _proj_res(y, w_ob, xs, gf, l, tm_s)
        xs = _ffn(hn, xs, w_gate, w_up, w_down, gfin, l, tm_s, tf_s, final)
        sv_s.append(sv.reshape(nb, 1, g))

    st = jnp.stack
    return (xp.reshape(batch, seq, d), xs.reshape(nb, 1, d),
            st(outs_p["a"]), st_a, st(outs_p["s"]), st_s.reshape(depth, nb, N_HEADS, qk // N_HEADS, dv),
            st(outs_p["c"]), st_c, st(outs_p["v"]), st(sv_s))
```

```python
import functools
import math

import jax
import jax.numpy as jnp
from jax import lax
from jax.experimental import pallas as pl
from jax.experimental.pallas import tpu as pltpu

F32 = jnp.float32
BF16 = jnp.bfloat16
EPS = 1e-6

N_HEADS = 4
GLA_LOWRANK = 16
GLA_TAU = 16.0
SC_WIDTH = 3
CF_WIDTH = 31
SG_CHUNK = 128
GLA_CHUNK = 64
GLA_SUB = 16
FRONT_ROWS = 256
LANES = 128
VMEM_LIMIT = 56 * 1024 * 1024


def _dot(a, b):
    return jnp.dot(a, b, preferred_element_type=F32)


def _dot_nt(a, b):
    return lax.dot_general(a, b, (((1,), (1,)), ((), ())), preferred_element_type=F32)


def _dot_tn(a, b):
    return lax.dot_general(a, b, (((0,), (0,)), ((), ())), preferred_element_type=F32)


def _split_bf16(x):
    hi = x.astype(BF16)
    lo = (x - hi.astype(F32)).astype(BF16)
    return hi, lo


def _rmsnorm(x, g):
    return x * lax.rsqrt(jnp.mean(x * x, axis=-1, keepdims=True) + EPS) * g


def _layernorm(x, g, b):
    mu = jnp.mean(x, axis=-1, keepdims=True)
    xc = x - mu
    return xc * lax.rsqrt(jnp.mean(xc * xc, axis=-1, keepdims=True) + EPS) * g + b


def _sigmoid(x):
    return 1.0 / (1.0 + jnp.exp(-x))


def _silu(x):
    return x * _sigmoid(x)


def _gelu_tanh(x):
    c = math.sqrt(2.0 / math.pi)
    return 0.5 * x * (1.0 + jnp.tanh(c * (x + 0.044715 * (x * x * x))))


def _log_sigmoid(x):
    return jnp.minimum(x, 0.0) - jnp.log(1.0 + jnp.exp(-jnp.abs(x)))


def _norm_proj_kernel(x_ref, g_ref, w_ref, wlr_ref, z_ref, zlr_ref, h_ref):
    @pl.when(pl.program_id(1) == 0)
    def _():
        h = _rmsnorm(x_ref[...], g_ref[...]).astype(BF16)
        h_ref[...] = h
        zlr_ref[...] = _dot(h, wlr_ref[...])

    z_ref[...] = _dot(h_ref[...], w_ref[...])


def _norm_proj(x, g, w, wlr, l, tm, tn):
    m, d = x.shape
    n = w.shape[2]
    return pl.pallas_call(
        _norm_proj_kernel,
        grid=(m // tm, n // tn),
        in_specs=[
            pl.BlockSpec((tm, d), lambda i, j: (i, 0)),
            pl.BlockSpec((1, d), lambda i, j: (0, 0)),
            pl.BlockSpec((None, d, tn), lambda i, j: (l, 0, j)),
            pl.BlockSpec((None, d, LANES), lambda i, j: (l, 0, 0)),
        ],
        out_specs=[
            pl.BlockSpec((tm, tn), lambda i, j: (i, j)),
            pl.BlockSpec((tm, LANES), lambda i, j: (i, 0)),
        ],
        out_shape=[jax.ShapeDtypeStruct((m, n), F32), jax.ShapeDtypeStruct((m, LANES), F32)],
        scratch_shapes=[pltpu.VMEM((tm, d), BF16)],
        compiler_params=pltpu.CompilerParams(
            dimension_semantics=("arbitrary", "arbitrary"), vmem_limit_bytes=VMEM_LIMIT),
    )(x, g, w, wlr)


def _proj_res_kernel(y_ref, w_ref, x_ref, g_ref, o_ref, h_ref):
    o = x_ref[...] + _dot(y_ref[...], w_ref[...])
    o_ref[...] = o
    h_ref[...] = _rmsnorm(o, g_ref[...]).astype(BF16)


def _proj_res(y, w, x, g, l, tm):
    m, k = y.shape
    n = w.shape[2]
    return pl.pallas_call(
        _proj_res_kernel,
        grid=(m // tm,),
        in_specs=[
            pl.BlockSpec((tm, k), lambda i: (i, 0)),
            pl.BlockSpec((None, k, n), lambda i: (l, 0, 0)),
            pl.BlockSpec((tm, n), lambda i: (i, 0)),
            pl.BlockSpec((1, n), lambda i: (0, 0)),
        ],
        out_specs=[pl.BlockSpec((tm, n), lambda i: (i, 0)), pl.BlockSpec((tm, n), lambda i: (i, 0))],
        out_shape=[jax.ShapeDtypeStruct((m, n), F32), jax.ShapeDtypeStruct((m, n), BF16)],
        compiler_params=pltpu.CompilerParams(
            dimension_semantics=("arbitrary",), vmem_limit_bytes=VMEM_LIMIT),
    )(y, w, x, g)


def _ffn_kernel(h_ref, x_ref, wg_ref, wu_ref, wd_ref, g_ref, o_ref, *cast_refs, final_norm):
    j = pl.program_id(1)

    @pl.when(j == 0)
    def _():
        o_ref[...] = x_ref[...]

    wg, wu, wd = wg_ref[...].astype(BF16), wu_ref[...].astype(BF16), wd_ref[...].astype(BF16)
    for ref, w in zip(cast_refs, (wg, wu, wd)):
        ref[...] = w
    h = h_ref[...]
    o_ref[...] += _dot((_silu(_dot(h, wg)) * _dot(h, wu)).astype(BF16), wd)

    if final_norm:
        @pl.when(j == pl.num_programs(1) - 1)
        def _():
            o_ref[...] = _rmsnorm(o_ref[...], g_ref[...])


def _ffn(h, x, wg, wu, wd, g, l, tm, tf, tile0, n_tiles, final_norm):
    m, d = x.shape
    f = wg.shape[-1]
    rows = lambda i, j: (tile0 + i, 0)
    if l is None:
        w_specs = [pl.BlockSpec((d, tf), lambda i, j: (0, j)), pl.BlockSpec((d, tf), lambda i, j: (0, j)),
                   pl.BlockSpec((tf, d), lambda i, j: (j, 0))]
        cast_specs, cast_shapes = [], []
    else:
        w_specs = [pl.BlockSpec((None, d, tf), lambda i, j: (l, 0, j)),
                   pl.BlockSpec((None, d, tf), lambda i, j: (l, 0, j)),
                   pl.BlockSpec((None, tf, d), lambda i, j: (l, j, 0))]
        cast_specs = [pl.BlockSpec((d, tf), lambda i, j: (0, j)), pl.BlockSpec((d, tf), lambda i, j: (0, j)),
                      pl.BlockSpec((tf, d), lambda i, j: (j, 0))]
        cast_shapes = [jax.ShapeDtypeStruct((d, f), BF16), jax.ShapeDtypeStruct((d, f), BF16),
                       jax.ShapeDtypeStruct((f, d), BF16)]
    return pl.pallas_call(
        functools.partial(_ffn_kernel, final_norm=final_norm),
        grid=(n_tiles, f // tf),
        in_specs=[pl.BlockSpec((tm, d), rows), pl.BlockSpec((tm, d), rows, pipeline_mode=pl.Buffered(1))]
                 + w_specs + [pl.BlockSpec((1, d), lambda i, j: (0, 0))],
        out_specs=[pl.BlockSpec((tm, d), rows)] + cast_specs,
        out_shape=[jax.ShapeDtypeStruct((m, d), F32)] + cast_shapes,
        input_output_aliases={1: 0},
        compiler_params=pltpu.CompilerParams(
            dimension_semantics=("arbitrary", "arbitrary"), vmem_limit_bytes=VMEM_LIMIT),
    )(h, x, wg, wu, wd, g)


def _gla_gate(zlr, a2_ref, ab_ref):
    xg = _dot(zlr.astype(BF16), a2_ref[...]) + ab_ref[...]
    return _log_sigmoid(xg) * (1.0 / GLA_TAU)


def _head_masks(qk):
    lane_head = lax.broadcasted_iota(jnp.int32, (1, qk), 1) // (qk // N_HEADS)
    return [(lane_head == h).astype(F32) for h in range(N_HEADS)]


def _gla_chunk(qc, kc, vc, bc, la_hi, la_lo, s_flat, e2, hm):
    c, qk = qc.shape
    dv = vc.shape[1] // N_HEADS
    dk = qk // N_HEADS
    nb = c // GLA_SUB
    vcb = vc.astype(BF16)
    sub_row = lax.broadcasted_iota(jnp.int32, (GLA_SUB, 1), 0)
    col_j = lax.broadcasted_iota(jnp.int32, (1, c), 1)

    o_diag = []
    for blk in range(nb):
        r0 = blk * GLA_SUB
        q_b, k_b, b_b = qc[r0:r0 + GLA_SUB], kc[r0:r0 + GLA_SUB], bc[r0:r0 + GLA_SUB]
        v_b = vc[r0:r0 + GLA_SUB]
        ts = []
        for j in range(GLA_SUB):
            dec = jnp.exp(jnp.minimum(b_b - b_b[j:j + 1], 0.0))
            ts.append(jnp.where(sub_row >= j, q_b * k_b[j:j + 1] * dec, 0.0).astype(BF16))
        r = _dot(jnp.concatenate(ts, axis=0), e2)
        acc = r[0:GLA_SUB] * v_b[0:1]
        for j in range(1, GLA_SUB):
            acc = acc + r[j * GLA_SUB:(j + 1) * GLA_SUB] * v_b[j:j + 1]
        o_diag.append(acc)
    o = jnp.concatenate(o_diag, axis=0)

    bref = [None] + [bc[blk * GLA_SUB - 1:blk * GLA_SUB] for blk in range(1, nb)]
    bref_rows = jnp.concatenate(
        [jnp.zeros((GLA_SUB, qk), F32)] + [jnp.broadcast_to(bref[blk], (GLA_SUB, qk)) for blk in range(1, nb)],
        axis=0)
    qt = qc * jnp.exp(bc - bref_rows)
    att_blk = [None]
    for blk in range(1, nb):
        kt = (kc * jnp.exp(jnp.minimum(bref[blk] - bc, 0.0))).astype(BF16)
        q_b = qt[blk * GLA_SUB:(blk + 1) * GLA_SUB]
        qm = jnp.concatenate([q_b * hm[h] for h in range(N_HEADS)], axis=0).astype(BF16)
        a = _dot_nt(qm, kt)
        att_blk.append(jnp.where(col_j < blk * GLA_SUB, a, 0.0))
    o_off = []
    for h in range(N_HEADS):
        att = jnp.concatenate(
            [jnp.zeros((GLA_SUB, c), F32)] + [att_blk[blk][h * GLA_SUB:(h + 1) * GLA_SUB] for blk in range(1, nb)],
            axis=0)
        o_off.append(_dot(att.astype(BF16), vcb[:, h * dv:(h + 1) * dv]))
    o = o + jnp.concatenate(o_off, axis=1)

    qe = qc * jnp.exp(bc)
    qe_m = jnp.concatenate([qe * hm[h] for h in range(N_HEADS)], axis=0).astype(BF16)
    o_int = _dot(qe_m, s_flat.astype(BF16))
    o = o + jnp.concatenate([o_int[h * c:(h + 1) * c] for h in range(N_HEADS)], axis=1)

    kk = (kc * jnp.exp(bc[c - 1:c] - bc)).astype(BF16)
    kv = _dot_tn(kk, vcb)
    kv_d = jnp.concatenate([kv[h * dk:(h + 1) * dk, h * dv:(h + 1) * dv] for h in range(N_HEADS)], axis=0)
    ones = jnp.ones((c, dv), BF16)
    decay = jnp.exp(_dot_tn(la_hi, ones) + _dot_tn(la_lo, ones))
    return o, decay * s_flat + kv_d


def _gla_out(o, gn, r):
    dv = o.shape[1] // N_HEADS
    parts = []
    for h in range(N_HEADS):
        oh = o[:, h * dv:(h + 1) * dv]
        parts.append(oh * lax.rsqrt(jnp.mean(oh * oh, axis=-1, keepdims=True) + EPS))
    return jnp.concatenate(parts, axis=1) * gn * _silu(r)


def _cols(g):
    qk = g // 2
    c = {}
    off = 0
    for name, width in (("a_x", g), ("a_b", g), ("a_c", g), ("q", qk), ("k", qk), ("v", g), ("r", g),
                        ("c_val", g), ("c_gate", g), ("d_u", g), ("d_v", g)):
        c[name] = (off, off + width)
        off += width
    return c, off


def _conv_taps(ccw_ref, cbuf, zbuf, tl, head):
    sub = 8
    base = head - (CF_WIDTH - 1)
    y = None
    for r in range(sub):
        taps = [k for k in range(CF_WIDTH) if (base + k) % sub == r]
        rows = tl if r == 0 else tl + sub
        part = None
        for k in taps:
            off = base + k - r
            term = ccw_ref[k:k + 1, :] * cbuf[off:off + rows, :]
            part = term if part is None else part + term
        if r == 0:
            shifted = part
        else:
            zbuf[0:rows, :] = part
            shifted = zbuf[r:r + tl, :]
        y = shifted if y is None else y + shifted
    return y


def _front_kernel(x_ref, gm_ref, wm_ref, wlr_ref, caw_ref, a2_ref, ab_ref, gn_ref, ccw_ref, ccb_ref,
                  lcg_ref, lcb_ref, ldg_ref, ldb_ref, sgw_ref, sgb_ref, tri_ref, e2_ref, wo_ref, gf_ref,
                  o_ref, hn_ref, na_ref, s_ref, nc_ref, sv_ref, abuf, cbuf, zbuf):
    t = pl.program_id(1)
    last = pl.num_programs(1) - 1
    tl = x_ref.shape[0]
    g = gn_ref.shape[1]
    qk = g // 2
    dv = g // N_HEADS
    cols, _ = _cols(g)
    a_head, c_head = 8, 32

    @pl.when(t == 0)
    def _():
        abuf[0:a_head, :] = jnp.zeros((a_head, g), F32)
        cbuf[0:c_head, :] = jnp.zeros((c_head, g), F32)
        s_ref[...] = jnp.zeros(s_ref.shape, F32)

    x = x_ref[...]
    h = _rmsnorm(x, gm_ref[...]).astype(BF16)

    def proj(*names):
        lo, hi = cols[names[0]][0], cols[names[-1]][1]
        zz = _dot(h, wm_ref[:, lo:hi])
        return [zz[:, cols[n][0] - lo:cols[n][1] - lo] for n in names]

    def out_proj(y, idx):
        return _dot(y.astype(BF16), wo_ref[idx * g:(idx + 1) * g, :])

    c_val, c_gate = proj("c_val", "c_gate")
    cin = c_val * _sigmoid(c_gate)
    cbuf[c_head:c_head + tl, :] = cin
    conv_c = _conv_taps(ccw_ref, cbuf, zbuf, tl, c_head) + ccb_ref[...]
    acc = x + out_proj(_silu(_layernorm(conv_c, lcg_ref[...], lcb_ref[...])), 2)

    a_x, a_b, a_c = proj("a_x", "a_b", "a_c")
    xa = a_c * a_x
    abuf[a_head:a_head + tl, :] = xa
    conv = caw_ref[SC_WIDTH - 1:SC_WIDTH, :] * xa
    for kk in range(SC_WIDTH - 1):
        sh = SC_WIDTH - 1 - kk
        conv = conv + caw_ref[kk:kk + 1, :] * abuf[a_head - sh:a_head - sh + tl, :]
    acc = acc + out_proj(a_b * conv, 0)

    d_u, d_v = proj("d_u", "d_v")
    du = _gelu_tanh(d_u)
    vd = _layernorm(_gelu_tanh(d_v), ldg_ref[...], ldb_ref[...])
    sv_ref[...] = vd[tl - SG_CHUNK:tl]
    vdb = vd.astype(BF16)
    row = lax.broadcasted_iota(jnp.int32, (SG_CHUNK, SG_CHUNK), 0)
    col = lax.broadcasted_iota(jnp.int32, (SG_CHUNK, SG_CHUNK), 1)
    ws = [jnp.where(row >= col, sgw_ref[hh], 0.0).astype(BF16) for hh in range(N_HEADS)]
    sv_rows = []
    for ci in range(tl // SG_CHUNK):
        sl = slice(ci * SG_CHUNK, (ci + 1) * SG_CHUNK)
        sv_rows.append(jnp.concatenate(
            [_dot(ws[hh], vdb[sl, hh * dv:(hh + 1) * dv]) for hh in range(N_HEADS)], axis=1) + sgb_ref[...])
    acc = acc + out_proj(du * jnp.concatenate(sv_rows, axis=0), 3)

    q_all, k_all, v_all, r_all = proj("q", "k", "v", "r")
    la = _gla_gate(_dot(h, wlr_ref[...]), a2_ref, ab_ref)
    la_hi, la_lo = _split_bf16(la)
    tri = tri_ref[...]
    b_all = _dot(tri, la_hi) + _dot(tri, la_lo)
    q_all = q_all * (float(qk // N_HEADS) ** -0.5)
    hm = _head_masks(qk)
    e2 = e2_ref[...]
    s_flat = s_ref[...]
    outs = []
    for ci in range(tl // GLA_CHUNK):
        sl = slice(ci * GLA_CHUNK, (ci + 1) * GLA_CHUNK)
        o_c, s_flat = _gla_chunk(q_all[sl], k_all[sl], v_all[sl], b_all[sl], la_hi[sl], la_lo[sl], s_flat, e2, hm)
        outs.append(o_c)
    s_ref[...] = s_flat
    acc = acc + out_proj(_gla_out(jnp.concatenate(outs, axis=0), gn_ref[...], r_all), 1)
    o_ref[...] = acc
    hn_ref[...] = _rmsnorm(acc, gf_ref[...]).astype(BF16)

    @pl.when(t == last)
    def _():
        na_ref[...] = abuf[a_head + tl - (SC_WIDTH - 1):a_head + tl, :]
        nc_ref[...] = cbuf[c_head + tl - (CF_WIDTH - 1):c_head + tl, :]

    abuf[0:a_head, :] = abuf[tl:tl + a_head, :]
    cbuf[0:c_head, :] = cbuf[tl:tl + c_head, :]


def _full(shape):
    nd = len(shape)
    return pl.BlockSpec(shape, lambda *_: (0,) * nd)


def _resident(shape, l):
    return pl.BlockSpec((None,) + shape, lambda *_: (l,) + (0,) * len(shape), pipeline_mode=pl.Buffered(1))


def _front(x, gm, w_main, w_lr, w_o, gf, l, batch, seq, wts, tl):
    d = x.shape[1]
    g = wts["gn"].shape[1]
    nt = seq // tl
    row_map = lambda b, t: (b * nt + t, 0)
    names = ("caw", "a2", "ab", "gn", "ccw", "ccb", "lcg", "lcb", "ldg", "ldb", "sgw", "sgb", "tri", "e2")
    consts = [wts[n] for n in names]
    return pl.pallas_call(
        _front_kernel,
        grid=(batch, nt),
        in_specs=[pl.BlockSpec((tl, d), row_map), _full(gm.shape),
                  _resident(w_main.shape[1:], l), _resident(w_lr.shape[1:], l)]
                 + [_full(c.shape) for c in consts] + [_resident(w_o.shape[1:], l), _full(gf.shape)],
        out_specs=[
            pl.BlockSpec((tl, d), row_map),
            pl.BlockSpec((tl, d), row_map),
            pl.BlockSpec((None, SC_WIDTH - 1, g), lambda b, t: (b, 0, 0)),
            pl.BlockSpec((None, g // 2, g // N_HEADS), lambda b, t: (b, 0, 0)),
            pl.BlockSpec((None, CF_WIDTH - 1, g), lambda b, t: (b, 0, 0)),
            pl.BlockSpec((None, SG_CHUNK, g), lambda b, t: (b, 0, 0)),
        ],
        out_shape=[
            jax.ShapeDtypeStruct(x.shape, F32),
            jax.ShapeDtypeStruct(x.shape, BF16),
            jax.ShapeDtypeStruct((batch, SC_WIDTH - 1, g), F32),
            jax.ShapeDtypeStruct((batch, g // 2, g // N_HEADS), F32),
            jax.ShapeDtypeStruct((batch, CF_WIDTH - 1, g), F32),
            jax.ShapeDtypeStruct((batch, SG_CHUNK, g), F32),
        ],
        scratch_shapes=[pltpu.VMEM((8 + tl, g), F32), pltpu.VMEM((32 + tl, g), F32), pltpu.VMEM((8 + tl, g), F32)],
        compiler_params=pltpu.CompilerParams(
            dimension_semantics=("arbitrary", "arbitrary"), vmem_limit_bytes=VMEM_LIMIT),
    )(x, gm, w_main, w_lr, *consts, w_o, gf)


def _mixer_sample_kernel(z_ref, zlr_ref, pa_ref, s_ref, pc_ref, caw_ref, a2_ref, ab_ref, gn_ref, ccw_ref, ccb_ref,
                         lcg_ref, lcb_ref, ldg_ref, ldb_ref, sgw0_ref, sgb0_ref,
                         y_ref, na_ref, so_ref, nc_ref, sv_ref, o_scr):
    bt = z_ref.shape[0]
    g = gn_ref.shape[1]
    qk = g // 2
    dk = qk // N_HEADS
    dv = g // N_HEADS
    cols, _ = _cols(g)

    def zc(name):
        lo, hi = cols[name]
        return z_ref[:, lo:hi]

    xa = zc("a_c") * zc("a_x")
    conv = caw_ref[SC_WIDTH - 1:SC_WIDTH, :] * xa
    for kk in range(SC_WIDTH - 1):
        conv = conv + caw_ref[kk:kk + 1, :] * pa_ref[:, kk, :]
    y_ref[:, 0:g] = (zc("a_b") * conv).astype(BF16)
    for kk in range(1, SC_WIDTH - 1):
        na_ref[:, kk - 1, :] = pa_ref[:, kk, :]
    na_ref[:, SC_WIDTH - 2, :] = xa

    la = _gla_gate(zlr_ref[...], a2_ref, ab_ref)
    a_t = jnp.exp(la).T
    k_t = zc("k").T
    q_rows = zc("q") * (float(dk) ** -0.5)
    v_rows = zc("v")
    hm = _head_masks(qk)
    hm_rows = jnp.concatenate(hm + [jnp.zeros_like(hm[0])] * (8 - N_HEADS), axis=0)
    for b in range(bt):
        a_col = jnp.broadcast_to(a_t[:, b:b + 1], (qk, dv))
        k_col = jnp.broadcast_to(k_t[:, b:b + 1], (qk, dv))
        v_b = jnp.concatenate(
            [jnp.broadcast_to(v_rows[b:b + 1, h * dv:(h + 1) * dv], (dk, dv)) for h in range(N_HEADS)], axis=0)
        s_new = a_col * s_ref[b] + k_col * v_b
        so_ref[b] = s_new
        q_m = (q_rows[b:b + 1] * hm_rows).astype(BF16)
        o_b = _dot(q_m, s_new.astype(BF16))
        for h in range(N_HEADS):
            o_scr[b:b + 1, h * dv:(h + 1) * dv] = o_b[h:h + 1]
    y_ref[:, g:2 * g] = _gla_out(o_scr[...], gn_ref[...], zc("r")).astype(BF16)

    cin = zc("c_val") * _sigmoid(zc("c_gate"))
    acc = ccb_ref[...] + ccw_ref[CF_WIDTH - 1:CF_WIDTH, :] * cin
    for kk in range(CF_WIDTH - 1):
        acc = acc + ccw_ref[kk:kk + 1, :] * pc_ref[:, kk, :]
    y_ref[:, 2 * g:3 * g] = _silu(_layernorm(acc, lcg_ref[...], lcb_ref[...])).astype(BF16)
    nc_ref[:, 0:CF_WIDTH - 2, :] = pc_ref[:, 1:CF_WIDTH - 1, :]
    nc_ref[:, CF_WIDTH - 2, :] = cin

    du = _gelu_tanh(zc("d_u"))
    vd = _layernorm(_gelu_tanh(zc("d_v")), ldg_ref[...], ldb_ref[...])
    sv_ref[...] = vd
    y_ref[:, 3 * g:4 * g] = (du * (sgw0_ref[...] * vd + sgb0_ref[...])).astype(BF16)


def _mixer_sample(z, zlr, pa, s, pc, wts, l, bt):
    nb, g = z.shape[0], wts["gn"].shape[1]
    names = ("caw", "a2", "ab", "gn", "ccw", "ccb", "lcg", "lcb", "ldg", "ldb", "sgw0", "sgb0")
    consts = [wts[n] for n in names]
    rows = lambda i: (i, 0)
    state_spec = lambda a: pl.BlockSpec((None, bt) + a.shape[2:], lambda i: (l, i, 0, 0))
    return pl.pallas_call(
        _mixer_sample_kernel,
        grid=(nb // bt,),
        in_specs=[pl.BlockSpec((bt, z.shape[1]), rows), pl.BlockSpec((bt, LANES), rows),
                  state_spec(pa), state_spec(s), state_spec(pc)] + [_full(c.shape) for c in consts],
        out_specs=[pl.BlockSpec((bt, 4 * g), rows), state_spec(pa), state_spec(s), state_spec(pc),
                   pl.BlockSpec((bt, g), rows)],
        out_shape=[
            jax.ShapeDtypeStruct((nb, 4 * g), BF16),
            jax.ShapeDtypeStruct(pa.shape, F32),
            jax.ShapeDtypeStruct(s.shape, F32),
            jax.ShapeDtypeStruct(pc.shape, F32),
            jax.ShapeDtypeStruct((nb, g), F32),
        ],
        input_output_aliases={2: 1, 3: 2, 4: 3},
        scratch_shapes=[pltpu.VMEM((bt, g), F32)],
        compiler_params=pltpu.CompilerParams(
            dimension_semantics=("arbitrary",), vmem_limit_bytes=VMEM_LIMIT),
    )(z, zlr, pa, s, pc, *consts)


def _largest_tile(m, cap, mult):
    t = min(m, cap)
    while m % t or t % mult:
        t -= mult
    return t


def _layer_weights(l, g, conv_a_w, gla_a2, gla_a_bias, gla_norm, conv_c_w, conv_c_b,
                   ln_c_g, ln_c_b, ln_d_g, ln_d_b, sg_w, sg_b):
    dv = g // N_HEADS
    row = lambda v: v.reshape(1, -1)
    return {
        "caw": conv_a_w[l],
        "a2": jnp.pad(gla_a2[l], ((0, LANES - GLA_LOWRANK), (0, 0))).astype(BF16),
        "ab": row(gla_a_bias[l]), "gn": row(gla_norm[l]),
        "ccw": conv_c_w[l], "ccb": row(conv_c_b[l]),
        "lcg": row(ln_c_g[l]), "lcb": row(ln_c_b[l]), "ldg": row(ln_d_g[l]), "ldb": row(ln_d_b[l]),
        "sgw": sg_w[l],
        "sgb": jnp.repeat(sg_b[l].T, dv, axis=1),
        "sgw0": row(jnp.repeat(sg_w[l][:, 0, 0], dv)),
        "sgb0": row(jnp.repeat(sg_b[l][:, 0], dv)),
    }


def _gla_constants(g, tl):
    qk, dv = g // 2, g // N_HEADS
    r = jnp.arange(tl)
    tri = ((r[:, None] >= r[None, :]) & (r[:, None] // GLA_CHUNK == r[None, :] // GLA_CHUNK)).astype(BF16)
    e2 = (jnp.arange(qk)[:, None] // (qk // N_HEADS) == jnp.arange(g)[None, :] // dv).astype(BF16)
    return tri, e2


def kernel(x_prompt, x_sample, state_conv_a, state_gla, state_conv_c, norm_mix, w_in, conv_a_w, gla_a2, gla_a_bias,
           gla_norm, conv_c_w, conv_c_b, ln_c_g, ln_c_b, ln_d_g, ln_d_b, sg_w, sg_b, w_o, norm_ffn, w_gate, w_up,
           w_down, norm_final):
    batch, seq, d = x_prompt.shape
    nb, dec_seq, _ = x_sample.shape
    depth = w_in.shape[0]
    g = d // 4
    qk, dv = g // 2, g // N_HEADS
    assert dec_seq == 1 and seq % SG_CHUNK == 0 and g % (N_HEADS * LANES) == 0
    dff = w_gate.shape[2]
    n_main = w_in.shape[2] - GLA_LOWRANK
    tl = _largest_tile(seq, FRONT_ROWS, SG_CHUNK)
    tri, e2 = _gla_constants(g, tl)

    lr0 = 3 * g + 2 * qk + 2 * g
    w_main = jnp.concatenate([w_in[:, :, :lr0], w_in[:, :, lr0 + GLA_LOWRANK:]], axis=2).astype(BF16)
    w_lr = jnp.pad(w_in[:, :, lr0:lr0 + GLA_LOWRANK], ((0, 0), (0, 0), (0, LANES - GLA_LOWRANK))).astype(BF16)
    w_ob = w_o.astype(BF16)

    mp = batch * seq
    tm_p = _largest_tile(mp, 1024, 16)
    tm_s = nb
    tn_in = _largest_tile(n_main, 1024, 256)
    tf_cast = _largest_tile(dff, 256, 256)
    tf = _largest_tile(dff, 512, 256)

    xp = x_prompt.reshape(mp, d)
    xs = x_sample.reshape(nb, d)
    st_a, st_c = state_conv_a, state_conv_c
    st_s = state_gla.reshape(depth, nb, qk, dv)
    outs_p = {k: [] for k in ("a", "s", "c", "v")}
    sv_s = []
    for l in range(depth):
        wts = _layer_weights(l, g, conv_a_w, gla_a2, gla_a_bias, gla_norm, conv_c_w, conv_c_b,
                             ln_c_g, ln_c_b, ln_d_g, ln_d_b, sg_w, sg_b)
        wts["tri"], wts["e2"] = tri, e2
        gm, gf = norm_mix[l].reshape(1, d), norm_ffn[l].reshape(1, d)
        gfin = norm_final.reshape(1, d)
        final = l == depth - 1

        xp, hn, na, s_new, nc, sv = _front(xp, gm, w_main, w_lr, w_ob, gf, l, batch, seq, wts, tl)
        xp, wg_b, wu_b, wd_b = _ffn(hn, xp, w_gate, w_up, w_down, gfin, l, tm_p, tf_cast, 0, 1, final)
        if mp > tm_p:
            xp, = _ffn(hn, xp, wg_b, wu_b, wd_b, gfin, None, tm_p, tf, 1, mp // tm_p - 1, final)
        outs_p["a"].append(na)
        outs_p["s"].append(s_new.reshape(batch, N_HEADS, qk // N_HEADS, dv))
        outs_p["c"].append(nc)
        outs_p["v"].append(sv)

        z, zlr = _norm_proj(xs, gm, w_main, w_lr, l, tm_s, tn_in)
        y, st_a, st_s, st_c, sv = _mixer_sample(z, zlr, st_a, st_s, st_c, wts, l, 32)
        xs, hn = _proj_res(y, w_ob, xs, gf, l, tm_s)
        xs, = _ffn(hn, xs, wg_b, wu_b, wd_b, gfin, None, tm_s, tf, 0, 1, final)
        sv_s.append(sv.reshape(nb, 1, g))

    st = jnp.stack
    return (xp.reshape(batch, seq, d), xs.reshape(nb, 1, d),
            st(outs_p["a"]), st_a, st(outs_p["s"]), st_s.reshape(depth, nb, N_HEADS, qk // N_HEADS, dv),
            st(outs_p["c"]), st_c, st(outs_p["v"]), st(sv_s))
```

```python
import functools
import math

import jax
import jax.numpy as jnp
from jax import lax
from jax.experimental import pallas as pl
from jax.experimental.pallas import tpu as pltpu

F32 = jnp.float32
BF16 = jnp.bfloat16
EPS = 1e-6

N_HEADS = 4
GLA_LOWRANK = 16
GLA_TAU = 16.0
SC_WIDTH = 3
CF_WIDTH = 31
SG_CHUNK = 128
GLA_CHUNK = 64
GLA_SUB = 16
FRONT_ROWS = 256
PROJ_SLAB = 256
LANES = 128
VMEM_LIMIT = 56 * 1024 * 1024


def _dot(a, b):
    return jnp.dot(a, b, preferred_element_type=F32)


def _dot_nt(a, b):
    return lax.dot_general(a, b, (((1,), (1,)), ((), ())), preferred_element_type=F32)


def _dot_tn(a, b):
    return lax.dot_general(a, b, (((0,), (0,)), ((), ())), preferred_element_type=F32)


def _split_bf16(x):
    hi = x.astype(BF16)
    lo = (x - hi.astype(F32)).astype(BF16)
    return hi, lo


def _rmsnorm(x, g):
    return x * lax.rsqrt(jnp.mean(x * x, axis=-1, keepdims=True) + EPS) * g


def _layernorm(x, g, b):
    mu = jnp.mean(x, axis=-1, keepdims=True)
    xc = x - mu
    return xc * lax.rsqrt(jnp.mean(xc * xc, axis=-1, keepdims=True) + EPS) * g + b


def _sigmoid(x):
    return 1.0 / (1.0 + jnp.exp(-x))


def _silu(x):
    return x * _sigmoid(x)


def _gelu_tanh(x):
    c = math.sqrt(2.0 / math.pi)
    return 0.5 * x * (1.0 + jnp.tanh(c * (x + 0.044715 * (x * x * x))))


def _log_sigmoid(x):
    return jnp.minimum(x, 0.0) - jnp.log(1.0 + jnp.exp(-jnp.abs(x)))


def _norm_proj_kernel(x_ref, g_ref, wa_ref, wb_ref, z_ref, zlr_ref, wm_ref, wlr_ref, h_ref, *, j_lr):
    j = pl.program_id(1)

    @pl.when(j == 0)
    def _():
        h_ref[...] = _rmsnorm(x_ref[...], g_ref[...]).astype(BF16)

    @pl.when(j < j_lr)
    def _():
        wm_ref[...] = wa_ref[...].astype(BF16)

    @pl.when(j >= j_lr)
    def _():
        wm_ref[...] = jnp.concatenate([wa_ref[:, GLA_LOWRANK:], wb_ref[:, :GLA_LOWRANK]], axis=1).astype(BF16)

    @pl.when(j == j_lr)
    def _():
        w_lr = jnp.concatenate([wa_ref[:, :GLA_LOWRANK],
                                jnp.zeros((wa_ref.shape[0], LANES - GLA_LOWRANK), F32)], axis=1).astype(BF16)
        wlr_ref[...] = w_lr
        zlr_ref[...] = _dot(h_ref[...], w_lr)

    z_ref[...] = _dot(h_ref[...], wm_ref[...])


def _norm_proj(x, g, w_in, l, lr0, tm, tn):
    m, d = x.shape
    n = w_in.shape[2] - GLA_LOWRANK
    assert lr0 % tn == 0 and tn % LANES == 0
    kern = functools.partial(_norm_proj_kernel, j_lr=lr0 // tn)
    return pl.pallas_call(
        kern,
        grid=(m // tm, n // tn),
        in_specs=[
            pl.BlockSpec((tm, d), lambda i, j: (i, 0)),
            pl.BlockSpec((1, d), lambda i, j: (0, 0)),
            pl.BlockSpec((None, d, tn), lambda i, j: (l, 0, j)),
            pl.BlockSpec((None, d, LANES), lambda i, j: (l, 0, (j + 1) * (tn // LANES))),
        ],
        out_specs=[
            pl.BlockSpec((tm, tn), lambda i, j: (i, j)),
            pl.BlockSpec((tm, LANES), lambda i, j: (i, 0)),
            pl.BlockSpec((d, tn), lambda i, j: (0, j)),
            pl.BlockSpec((d, LANES), lambda i, j: (0, 0)),
        ],
        out_shape=[jax.ShapeDtypeStruct((m, n), F32), jax.ShapeDtypeStruct((m, LANES), F32),
                   jax.ShapeDtypeStruct((d, n), BF16), jax.ShapeDtypeStruct((d, LANES), BF16)],
        scratch_shapes=[pltpu.VMEM((tm, d), BF16)],
        compiler_params=pltpu.CompilerParams(
            dimension_semantics=("arbitrary", "arbitrary"), vmem_limit_bytes=VMEM_LIMIT),
    )(x, g, w_in, w_in)


def _proj_res_kernel(y_ref, w_ref, x_ref, g_ref, o_ref, h_ref, wb_ref):
    k = pl.program_id(0)

    @pl.when(k == 0)
    def _():
        o_ref[...] = x_ref[...]

    w = w_ref[...].astype(BF16)
    wb_ref[...] = w
    o_ref[...] += _dot(y_ref[...], w)

    @pl.when(k == pl.num_programs(0) - 1)
    def _():
        h_ref[...] = _rmsnorm(o_ref[...], g_ref[...]).astype(BF16)


def _proj_res(y, w, x, g, l, tk):
    m, k = y.shape
    n = w.shape[2]
    return pl.pallas_call(
        _proj_res_kernel,
        grid=(k // tk,),
        in_specs=[
            pl.BlockSpec((m, tk), lambda i: (0, i)),
            pl.BlockSpec((None, tk, n), lambda i: (l, i, 0)),
            pl.BlockSpec((m, n), lambda i: (0, 0)),
            pl.BlockSpec((1, n), lambda i: (0, 0)),
        ],
        out_specs=[pl.BlockSpec((m, n), lambda i: (0, 0)), pl.BlockSpec((m, n), lambda i: (0, 0)),
                   pl.BlockSpec((tk, n), lambda i: (i, 0))],
        out_shape=[jax.ShapeDtypeStruct((m, n), F32), jax.ShapeDtypeStruct((m, n), BF16),
                   jax.ShapeDtypeStruct((k, n), BF16)],
        compiler_params=pltpu.CompilerParams(
            dimension_semantics=("arbitrary",), vmem_limit_bytes=VMEM_LIMIT),
    )(y, w, x, g)


def _ffn_kernel(h_ref, x_ref, wg_ref, wu_ref, wd_ref, g_ref, o_ref, *cast_refs, final_norm):
    j = pl.program_id(1)

    @pl.when(j == 0)
    def _():
        o_ref[...] = x_ref[...]

    wg, wu, wd = wg_ref[...].astype(BF16), wu_ref[...].astype(BF16), wd_ref[...].astype(BF16)
    for ref, w in zip(cast_refs, (wg, wu, wd)):
        ref[...] = w
    h = h_ref[...]
    o_ref[...] += _dot((_silu(_dot(h, wg)) * _dot(h, wu)).astype(BF16), wd)

    if final_norm:
        @pl.when(j == pl.num_programs(1) - 1)
        def _():
            o_ref[...] = _rmsnorm(o_ref[...], g_ref[...])


def _ffn(h, x, wg, wu, wd, g, l, tm, tf, tile0, n_tiles, final_norm):
    m, d = x.shape
    f = wg.shape[-1]
    rows = lambda i, j: (tile0 + i, 0)
    if l is None:
        w_specs = [pl.BlockSpec((d, tf), lambda i, j: (0, j)), pl.BlockSpec((d, tf), lambda i, j: (0, j)),
                   pl.BlockSpec((tf, d), lambda i, j: (j, 0))]
        cast_specs, cast_shapes = [], []
    else:
        w_specs = [pl.BlockSpec((None, d, tf), lambda i, j: (l, 0, j)),
                   pl.BlockSpec((None, d, tf), lambda i, j: (l, 0, j)),
                   pl.BlockSpec((None, tf, d), lambda i, j: (l, j, 0))]
        cast_specs = [pl.BlockSpec((d, tf), lambda i, j: (0, j)), pl.BlockSpec((d, tf), lambda i, j: (0, j)),
                      pl.BlockSpec((tf, d), lambda i, j: (j, 0))]
        cast_shapes = [jax.ShapeDtypeStruct((d, f), BF16), jax.ShapeDtypeStruct((d, f), BF16),
                       jax.ShapeDtypeStruct((f, d), BF16)]
    return pl.pallas_call(
        functools.partial(_ffn_kernel, final_norm=final_norm),
        grid=(n_tiles, f // tf),
        in_specs=[pl.BlockSpec((tm, d), rows), pl.BlockSpec((tm, d), rows, pipeline_mode=pl.Buffered(1))]
                 + w_specs + [pl.BlockSpec((1, d), lambda i, j: (0, 0))],
        out_specs=[pl.BlockSpec((tm, d), rows)] + cast_specs,
        out_shape=[jax.ShapeDtypeStruct((m, d), F32)] + cast_shapes,
        input_output_aliases={1: 0},
        compiler_params=pltpu.CompilerParams(
            dimension_semantics=("arbitrary", "arbitrary"), vmem_limit_bytes=VMEM_LIMIT),
    )(h, x, wg, wu, wd, g)


def _gla_gate(zlr, a2_ref, ab_ref):
    xg = _dot(zlr.astype(BF16), a2_ref[...]) + ab_ref[...]
    return _log_sigmoid(xg) * (1.0 / GLA_TAU)


def _head_masks(qk):
    lane_head = lax.broadcasted_iota(jnp.int32, (1, qk), 1) // (qk // N_HEADS)
    return [(lane_head == h).astype(F32) for h in range(N_HEADS)]


def _gla_chunk(qc, kc, vc, bc, la_hi, la_lo, s_flat, e2, hm):
    c, qk = qc.shape
    dv = vc.shape[1] // N_HEADS
    dk = qk // N_HEADS
    nb = c // GLA_SUB
    vcb = vc.astype(BF16)
    sub_row = lax.broadcasted_iota(jnp.int32, (GLA_SUB, 1), 0)
    col_j = lax.broadcasted_iota(jnp.int32, (1, c), 1)

    o_diag = []
    for blk in range(nb):
        r0 = blk * GLA_SUB
        q_b, k_b, b_b = qc[r0:r0 + GLA_SUB], kc[r0:r0 + GLA_SUB], bc[r0:r0 + GLA_SUB]
        v_b = vc[r0:r0 + GLA_SUB]
        ts = []
        for j in range(GLA_SUB):
            dec = jnp.exp(jnp.minimum(b_b - b_b[j:j + 1], 0.0))
            ts.append(jnp.where(sub_row >= j, q_b * k_b[j:j + 1] * dec, 0.0).astype(BF16))
        r = _dot(jnp.concatenate(ts, axis=0), e2)
        acc = r[0:GLA_SUB] * v_b[0:1]
        for j in range(1, GLA_SUB):
            acc = acc + r[j * GLA_SUB:(j + 1) * GLA_SUB] * v_b[j:j + 1]
        o_diag.append(acc)
    o = jnp.concatenate(o_diag, axis=0)

    bref = [None] + [bc[blk * GLA_SUB - 1:blk * GLA_SUB] for blk in range(1, nb)]
    bref_rows = jnp.concatenate(
        [jnp.zeros((GLA_SUB, qk), F32)] + [jnp.broadcast_to(bref[blk], (GLA_SUB, qk)) for blk in range(1, nb)],
        axis=0)
    qt = qc * jnp.exp(bc - bref_rows)
    att_blk = [None]
    for blk in range(1, nb):
        kt = (kc * jnp.exp(jnp.minimum(bref[blk] - bc, 0.0))).astype(BF16)
        q_b = qt[blk * GLA_SUB:(blk + 1) * GLA_SUB]
        qm = jnp.concatenate([q_b * hm[h] for h in range(N_HEADS)], axis=0).astype(BF16)
        a = _dot_nt(qm, kt)
        att_blk.append(jnp.where(col_j < blk * GLA_SUB, a, 0.0))
    o_off = []
    for h in range(N_HEADS):
        att = jnp.concatenate(
            [jnp.zeros((GLA_SUB, c), F32)] + [att_blk[blk][h * GLA_SUB:(h + 1) * GLA_SUB] for blk in range(1, nb)],
            axis=0)
        o_off.append(_dot(att.astype(BF16), vcb[:, h * dv:(h + 1) * dv]))
    o = o + jnp.concatenate(o_off, axis=1)

    qe = qc * jnp.exp(bc)
    qe_m = jnp.concatenate([qe * hm[h] for h in range(N_HEADS)], axis=0).astype(BF16)
    o_int = _dot(qe_m, s_flat.astype(BF16))
    o = o + jnp.concatenate([o_int[h * c:(h + 1) * c] for h in range(N_HEADS)], axis=1)

    kk = (kc * jnp.exp(bc[c - 1:c] - bc)).astype(BF16)
    kv = _dot_tn(kk, vcb)
    kv_d = jnp.concatenate([kv[h * dk:(h + 1) * dk, h * dv:(h + 1) * dv] for h in range(N_HEADS)], axis=0)
    ones = jnp.ones((c, dv), BF16)
    decay = jnp.exp(_dot_tn(la_hi, ones) + _dot_tn(la_lo, ones))
    return o, decay * s_flat + kv_d


def _gla_out(o, gn, r):
    dv = o.shape[1] // N_HEADS
    parts = []
    for h in range(N_HEADS):
        oh = o[:, h * dv:(h + 1) * dv]
        parts.append(oh * lax.rsqrt(jnp.mean(oh * oh, axis=-1, keepdims=True) + EPS))
    return jnp.concatenate(parts, axis=1) * gn * _silu(r)


def _cols(g):
    qk = g // 2
    c = {}
    off = 0
    for name, width in (("a_x", g), ("a_b", g), ("a_c", g), ("q", qk), ("k", qk), ("v", g), ("r", g),
                        ("c_val", g), ("c_gate", g), ("d_u", g), ("d_v", g)):
        c[name] = (off, off + width)
        off += width
    return c, off


def _conv_taps(ccw_ref, cbuf, zbuf, tl, head, between):
    sub = 8
    base = head - (CF_WIDTH - 1)
    y = None
    for r in range(sub):
        taps = [k for k in range(CF_WIDTH) if (base + k) % sub == r]
        rows = tl if r == 0 else tl + sub
        part = None
        for k in taps:
            off = base + k - r
            term = ccw_ref[k:k + 1, :] * cbuf[off:off + rows, :]
            part = term if part is None else part + term
        if r == 0:
            shifted = part
        else:
            zbuf[0:rows, :] = part
            shifted = zbuf[r:r + tl, :]
        y = shifted if y is None else y + shifted
        between()
    return y


def _front_kernel(x_ref, gm_ref, wm_ref, wlr_ref, caw_ref, a2_ref, ab_ref, gn_ref, ccw_ref, ccb_ref,
                  lcg_ref, lcb_ref, ldg_ref, ldb_ref, sgw_ref, sgb_ref, tri_ref, e2_ref, wo_ref, gf_ref,
                  o_ref, hn_ref, na_ref, s_ref, nc_ref, sv_ref, abuf, cbuf, zbuf):
    t = pl.program_id(1)
    last = pl.num_programs(1) - 1
    tl = x_ref.shape[0]
    g = gn_ref.shape[1]
    qk = g // 2
    dv = g // N_HEADS
    cols, _ = _cols(g)
    a_head, c_head = 8, 32

    @pl.when(t == 0)
    def _():
        abuf[0:a_head, :] = jnp.zeros((a_head, g), F32)
        cbuf[0:c_head, :] = jnp.zeros((c_head, g), F32)
        s_ref[...] = jnp.zeros(s_ref.shape, F32)

    x = x_ref[...]
    h = _rmsnorm(x, gm_ref[...]).astype(BF16)

    slab = PROJ_SLAB
    order = [c0 for n in ("c_val", "c_gate", "a_x", "a_b", "a_c", "d_u", "d_v", "q", "k", "v", "r")
             for c0 in range(cols[n][0], cols[n][1], slab) if c0 % slab == 0]
    pending = list(dict.fromkeys(order))
    slabs = {}

    def pump():
        if pending:
            c0 = pending.pop(0)
            slabs[c0] = _dot(h, wm_ref[:, c0:c0 + slab])

    def proj(*names):
        out = []
        for n in names:
            lo, hi = cols[n]
            parts = []
            for c0 in range(lo - lo % slab, hi, slab):
                while c0 not in slabs:
                    pump()
                s_lo, s_hi = max(lo, c0) - c0, min(hi, c0 + slab) - c0
                parts.append(slabs[c0][:, s_lo:s_hi])
            out.append(parts[0] if len(parts) == 1 else jnp.concatenate(parts, axis=1))
        return out

    def out_proj(y, idx):
        return _dot(y.astype(BF16), wo_ref[idx * g:(idx + 1) * g, :])

    c_val, c_gate = proj("c_val", "c_gate")
    cin = c_val * _sigmoid(c_gate)
    cbuf[c_head:c_head + tl, :] = cin
    conv_c = _conv_taps(ccw_ref, cbuf, zbuf, tl, c_head, pump) + ccb_ref[...]
    acc = x + out_proj(_silu(_layernorm(conv_c, lcg_ref[...], lcb_ref[...])), 2)

    a_x, a_b, a_c = proj("a_x", "a_b", "a_c")
    xa = a_c * a_x
    abuf[a_head:a_head + tl, :] = xa
    conv = caw_ref[SC_WIDTH - 1:SC_WIDTH, :] * xa
    for kk in range(SC_WIDTH - 1):
        sh = SC_WIDTH - 1 - kk
        conv = conv + caw_ref[kk:kk + 1, :] * abuf[a_head - sh:a_head - sh + tl, :]
    acc = acc + out_proj(a_b * conv, 0)

    d_u, d_v = proj("d_u", "d_v")
    du = _gelu_tanh(d_u)
    vd = _layernorm(_gelu_tanh(d_v), ldg_ref[...], ldb_ref[...])
    sv_ref[...] = vd[tl - SG_CHUNK:tl]
    vdb = vd.astype(BF16)
    row = lax.broadcasted_iota(jnp.int32, (SG_CHUNK, SG_CHUNK), 0)
    col = lax.broadcasted_iota(jnp.int32, (SG_CHUNK, SG_CHUNK), 1)
    ws = [jnp.where(row >= col, sgw_ref[hh], 0.0).astype(BF16) for hh in range(N_HEADS)]
    sv_rows = []
    for ci in range(tl // SG_CHUNK):
        sl = slice(ci * SG_CHUNK, (ci + 1) * SG_CHUNK)
        sv_rows.append(jnp.concatenate(
            [_dot(ws[hh], vdb[sl, hh * dv:(hh + 1) * dv]) for hh in range(N_HEADS)], axis=1) + sgb_ref[...])
    acc = acc + out_proj(du * jnp.concatenate(sv_rows, axis=0), 3)

    q_all, k_all, v_all, r_all = proj("q", "k", "v", "r")
    la = _gla_gate(_dot(h, wlr_ref[...]), a2_ref, ab_ref)
    la_hi, la_lo = _split_bf16(la)
    tri = tri_ref[...]
    b_all = _dot(tri, la_hi) + _dot(tri, la_lo)
    q_all = q_all * (float(qk // N_HEADS) ** -0.5)
    hm = _head_masks(qk)
    e2 = e2_ref[...]
    s_flat = s_ref[...]
    outs = []
    for ci in range(tl // GLA_CHUNK):
        sl = slice(ci * GLA_CHUNK, (ci + 1) * GLA_CHUNK)
        o_c, s_flat = _gla_chunk(q_all[sl], k_all[sl], v_all[sl], b_all[sl], la_hi[sl], la_lo[sl], s_flat, e2, hm)
        outs.append(o_c)
    s_ref[...] = s_flat
    acc = acc + out_proj(_gla_out(jnp.concatenate(outs, axis=0), gn_ref[...], r_all), 1)
    o_ref[...] = acc
    hn_ref[...] = _rmsnorm(acc, gf_ref[...]).astype(BF16)

    @pl.when(t == last)
    def _():
        na_ref[...] = abuf[a_head + tl - (SC_WIDTH - 1):a_head + tl, :]
        nc_ref[...] = cbuf[c_head + tl - (CF_WIDTH - 1):c_head + tl, :]

    abuf[0:a_head, :] = abuf[tl:tl + a_head, :]
    cbuf[0:c_head, :] = cbuf[tl:tl + c_head, :]


def _full(shape):
    nd = len(shape)
    return pl.BlockSpec(shape, lambda *_: (0,) * nd)


def _resident(shape):
    return pl.BlockSpec(shape, lambda *_: (0,) * len(shape), pipeline_mode=pl.Buffered(1))


def _front(x, gm, w_main, w_lr, w_o, gf, batch, seq, wts, tl):
    d = x.shape[1]
    g = wts["gn"].shape[1]
    nt = seq // tl
    row_map = lambda b, t: (b * nt + t, 0)
    names = ("caw", "a2", "ab", "gn", "ccw", "ccb", "lcg", "lcb", "ldg", "ldb", "sgw", "sgb", "tri", "e2")
    consts = [wts[n] for n in names]
    return pl.pallas_call(
        _front_kernel,
        grid=(batch, nt),
        in_specs=[pl.BlockSpec((tl, d), row_map), _full(gm.shape),
                  _resident(w_main.shape), _resident(w_lr.shape)]
                 + [_full(c.shape) for c in consts] + [_resident(w_o.shape), _full(gf.shape)],
        out_specs=[
            pl.BlockSpec((tl, d), row_map),
            pl.BlockSpec((tl, d), row_map),
            pl.BlockSpec((None, SC_WIDTH - 1, g), lambda b, t: (b, 0, 0)),
            pl.BlockSpec((None, g // 2, g // N_HEADS), lambda b, t: (b, 0, 0)),
            pl.BlockSpec((None, CF_WIDTH - 1, g), lambda b, t: (b, 0, 0)),
            pl.BlockSpec((None, SG_CHUNK, g), lambda b, t: (b, 0, 0)),
        ],
        out_shape=[
            jax.ShapeDtypeStruct(x.shape, F32),
            jax.ShapeDtypeStruct(x.shape, BF16),
            jax.ShapeDtypeStruct((batch, SC_WIDTH - 1, g), F32),
            jax.ShapeDtypeStruct((batch, g // 2, g // N_HEADS), F32),
            jax.ShapeDtypeStruct((batch, CF_WIDTH - 1, g), F32),
            jax.ShapeDtypeStruct((batch, SG_CHUNK, g), F32),
        ],
        scratch_shapes=[pltpu.VMEM((8 + tl, g), F32), pltpu.VMEM((32 + tl, g), F32), pltpu.VMEM((8 + tl, g), F32)],
        compiler_params=pltpu.CompilerParams(
            dimension_semantics=("arbitrary", "arbitrary"), vmem_limit_bytes=VMEM_LIMIT),
    )(x, gm, w_main, w_lr, *consts, w_o, gf)


def _mixer_sample_kernel(z_ref, zlr_ref, pa_ref, s_ref, pc_ref, caw_ref, a2_ref, ab_ref, gn_ref, ccw_ref, ccb_ref,
                         lcg_ref, lcb_ref, ldg_ref, ldb_ref, sgw0_ref, sgb0_ref,
                         y_ref, na_ref, so_ref, nc_ref, sv_ref, o_scr):
    bt = z_ref.shape[0]
    g = gn_ref.shape[1]
    qk = g // 2
    dk = qk // N_HEADS
    dv = g // N_HEADS
    cols, _ = _cols(g)

    def zc(name):
        lo, hi = cols[name]
        return z_ref[:, lo:hi]

    xa = zc("a_c") * zc("a_x")
    conv = caw_ref[SC_WIDTH - 1:SC_WIDTH, :] * xa
    for kk in range(SC_WIDTH - 1):
        conv = conv + caw_ref[kk:kk + 1, :] * pa_ref[:, kk, :]
    y_ref[:, 0:g] = (zc("a_b") * conv).astype(BF16)
    for kk in range(1, SC_WIDTH - 1):
        na_ref[:, kk - 1, :] = pa_ref[:, kk, :]
    na_ref[:, SC_WIDTH - 2, :] = xa

    la = _gla_gate(zlr_ref[...], a2_ref, ab_ref)
    a_t = jnp.exp(la).T
    k_t = zc("k").T
    q_rows = zc("q") * (float(dk) ** -0.5)
    v_rows = zc("v")
    hm = _head_masks(qk)
    hm_rows = jnp.concatenate(hm + [jnp.zeros_like(hm[0])] * (8 - N_HEADS), axis=0)
    for b in range(bt):
        a_col = jnp.broadcast_to(a_t[:, b:b + 1], (qk, dv))
        k_col = jnp.broadcast_to(k_t[:, b:b + 1], (qk, dv))
        v_b = jnp.concatenate(
            [jnp.broadcast_to(v_rows[b:b + 1, h * dv:(h + 1) * dv], (dk, dv)) for h in range(N_HEADS)], axis=0)
        s_new = a_col * s_ref[b] + k_col * v_b
        so_ref[b] = s_new
        q_m = (q_rows[b:b + 1] * hm_rows).astype(BF16)
        o_b = _dot(q_m, s_new.astype(BF16))
        for h in range(N_HEADS):
            o_scr[b:b + 1, h * dv:(h + 1) * dv] = o_b[h:h + 1]
    y_ref[:, g:2 * g] = _gla_out(o_scr[...], gn_ref[...], zc("r")).astype(BF16)

    cin = zc("c_val") * _sigmoid(zc("c_gate"))
    acc = ccb_ref[...] + ccw_ref[CF_WIDTH - 1:CF_WIDTH, :] * cin
    for kk in range(CF_WIDTH - 1):
        acc = acc + ccw_ref[kk:kk + 1, :] * pc_ref[:, kk, :]
    y_ref[:, 2 * g:3 * g] = _silu(_layernorm(acc, lcg_ref[...], lcb_ref[...])).astype(BF16)
    nc_ref[:, 0:CF_WIDTH - 2, :] = pc_ref[:, 1:CF_WIDTH - 1, :]
    nc_ref[:, CF_WIDTH - 2, :] = cin

    du = _gelu_tanh(zc("d_u"))
    vd = _layernorm(_gelu_tanh(zc("d_v")), ldg_ref[...], ldb_ref[...])
    sv_ref[...] = vd
    y_ref[:, 3 * g:4 * g] = (du * (sgw0_ref[...] * vd + sgb0_ref[...])).astype(BF16)


def _mixer_sample(z, zlr, pa, s, pc, wts, l, bt):
    nb, g = z.shape[0], wts["gn"].shape[1]
    names = ("caw", "a2", "ab", "gn", "ccw", "ccb", "lcg", "lcb", "ldg", "ldb", "sgw0", "sgb0")
    consts = [wts[n] for n in names]
    rows = lambda i: (i, 0)
    state_spec = lambda a: pl.BlockSpec((None, bt) + a.shape[2:], lambda i: (l, i, 0, 0))
    return pl.pallas_call(
        _mixer_sample_kernel,
        grid=(nb // bt,),
        in_specs=[pl.BlockSpec((bt, z.shape[1]), rows), pl.BlockSpec((bt, LANES), rows),
                  state_spec(pa), state_spec(s), state_spec(pc)] + [_full(c.shape) for c in consts],
        out_specs=[pl.BlockSpec((bt, 4 * g), rows), state_spec(pa), state_spec(s), state_spec(pc),
                   pl.BlockSpec((bt, g), rows)],
        out_shape=[
            jax.ShapeDtypeStruct((nb, 4 * g), BF16),
            jax.ShapeDtypeStruct(pa.shape, F32),
            jax.ShapeDtypeStruct(s.shape, F32),
            jax.ShapeDtypeStruct(pc.shape, F32),
            jax.ShapeDtypeStruct((nb, g), F32),
        ],
        input_output_aliases={2: 1, 3: 2, 4: 3},
        scratch_shapes=[pltpu.VMEM((bt, g), F32)],
        compiler_params=pltpu.CompilerParams(
            dimension_semantics=("arbitrary",), vmem_limit_bytes=VMEM_LIMIT),
    )(z, zlr, pa, s, pc, *consts)


def _largest_tile(m, cap, mult):
    t = min(m, cap)
    while m % t or t % mult:
        t -= mult
    return t


def _layer_weights(l, g, conv_a_w, gla_a2, gla_a_bias, gla_norm, conv_c_w, conv_c_b,
                   ln_c_g, ln_c_b, ln_d_g, ln_d_b, sg_w, sg_b):
    dv = g // N_HEADS
    row = lambda v: v.reshape(1, -1)
    return {
        "caw": conv_a_w[l],
        "a2": jnp.pad(gla_a2[l], ((0, LANES - GLA_LOWRANK), (0, 0))).astype(BF16),
        "ab": row(gla_a_bias[l]), "gn": row(gla_norm[l]),
        "ccw": conv_c_w[l], "ccb": row(conv_c_b[l]),
        "lcg": row(ln_c_g[l]), "lcb": row(ln_c_b[l]), "ldg": row(ln_d_g[l]), "ldb": row(ln_d_b[l]),
        "sgw": sg_w[l],
        "sgb": jnp.repeat(sg_b[l].T, dv, axis=1),
        "sgw0": row(jnp.repeat(sg_w[l][:, 0, 0], dv)),
        "sgb0": row(jnp.repeat(sg_b[l][:, 0], dv)),
    }


def _gla_constants(g, tl):
    qk, dv = g // 2, g // N_HEADS
    r = jnp.arange(tl)
    tri = ((r[:, None] >= r[None, :]) & (r[:, None] // GLA_CHUNK == r[None, :] // GLA_CHUNK)).astype(BF16)
    e2 = (jnp.arange(qk)[:, None] // (qk // N_HEADS) == jnp.arange(g)[None, :] // dv).astype(BF16)
    return tri, e2


def kernel(x_prompt, x_sample, state_conv_a, state_gla, state_conv_c, norm_mix, w_in, conv_a_w, gla_a2, gla_a_bias,
           gla_norm, conv_c_w, conv_c_b, ln_c_g, ln_c_b, ln_d_g, ln_d_b, sg_w, sg_b, w_o, norm_ffn, w_gate, w_up,
           w_down, norm_final):
    batch, seq, d = x_prompt.shape
    nb, dec_seq, _ = x_sample.shape
    depth = w_in.shape[0]
    g = d // 4
    qk, dv = g // 2, g // N_HEADS
    assert dec_seq == 1 and seq % SG_CHUNK == 0 and g % (N_HEADS * LANES) == 0
    dff = w_gate.shape[2]
    n_main = w_in.shape[2] - GLA_LOWRANK
    tl = _largest_tile(seq, FRONT_ROWS, SG_CHUNK)
    tri, e2 = _gla_constants(g, tl)

    lr0 = 3 * g + 2 * qk + 2 * g

    mp = batch * seq
    tm_p = _largest_tile(mp, 1024, 16)
    tm_s = nb
    tn_in = _largest_tile(math.gcd(n_main, lr0), 512, 256)
    tf_cast = _largest_tile(dff, 256, 256)
    tf = _largest_tile(dff, 512, 256)

    xp = x_prompt.reshape(mp, d)
    xs = x_sample.reshape(nb, d)
    st_a, st_c = state_conv_a, state_conv_c
    st_s = state_gla.reshape(depth, nb, qk, dv)
    outs_p = {k: [] for k in ("a", "s", "c", "v")}
    sv_s = []
    for l in range(depth):
        wts = _layer_weights(l, g, conv_a_w, gla_a2, gla_a_bias, gla_norm, conv_c_w, conv_c_b,
                             ln_c_g, ln_c_b, ln_d_g, ln_d_b, sg_w, sg_b)
        wts["tri"], wts["e2"] = tri, e2
        gm, gf = norm_mix[l].reshape(1, d), norm_ffn[l].reshape(1, d)
        gfin = norm_final.reshape(1, d)
        final = l == depth - 1

        z, zlr, w_main, w_lr = _norm_proj(xs, gm, w_in, l, lr0, tm_s, tn_in)
        y, st_a, st_s, st_c, sv = _mixer_sample(z, zlr, st_a, st_s, st_c, wts, l, 32)
        xs, hs, w_ob = _proj_res(y, w_o, xs, gf, l, tn_in)

        xp, hn, na, s_new, nc, svp = _front(xp, gm, w_main, w_lr, w_ob, gf, batch, seq, wts, tl)
        xp, wg_b, wu_b, wd_b = _ffn(hn, xp, w_gate, w_up, w_down, gfin, l, tm_p, tf_cast, 0, 1, final)
        if mp > tm_p:
            xp, = _ffn(hn, xp, wg_b, wu_b, wd_b, gfin, None, tm_p, tf, 1, mp // tm_p - 1, final)
        outs_p["a"].append(na)
        outs_p["s"].append(s_new.reshape(batch, N_HEADS, qk // N_HEADS, dv))
        outs_p["c"].append(nc)
        outs_p["v"].append(svp)

        xs, = _ffn(hs, xs, wg_b, wu_b, wd_b, gfin, None, tm_s, tf, 0, 1, final)
        sv_s.append(sv.reshape(nb, 1, g))

    st = jnp.stack
    return (xp.reshape(batch, seq, d), xs.reshape(nb, 1, d),
            st(outs_p["a"]), st_a, st(outs_p["s"]), st_s.reshape(depth, nb, N_HEADS, qk // N_HEADS, dv),
            st(outs_p["c"]), st_c, st(outs_p["v"]), st(sv_s))
```

```python
import functools
import math

import jax
import jax.numpy as jnp
from jax import lax
from jax.experimental import pallas as pl
from jax.experimental.pallas import tpu as pltpu

F32 = jnp.float32
BF16 = jnp.bfloat16
EPS = 1e-6

N_HEADS = 4
GLA_LOWRANK = 16
GLA_TAU = 16.0
SC_WIDTH = 3
CF_WIDTH = 31
SG_CHUNK = 128
GLA_CHUNK = 64
GLA_SUB = 16
FRONT_ROWS = 256
PROJ_SLAB = 256
LANES = 128
VMEM_LIMIT = 56 * 1024 * 1024


def _dot(a, b):
    return jnp.dot(a, b, preferred_element_type=F32)


def _dot_nt(a, b):
    return lax.dot_general(a, b, (((1,), (1,)), ((), ())), preferred_element_type=F32)


def _dot_tn(a, b):
    return lax.dot_general(a, b, (((0,), (0,)), ((), ())), preferred_element_type=F32)


def _split_bf16(x):
    hi = x.astype(BF16)
    lo = (x - hi.astype(F32)).astype(BF16)
    return hi, lo


def _rmsnorm(x, g):
    return x * lax.rsqrt(jnp.mean(x * x, axis=-1, keepdims=True) + EPS) * g


def _layernorm(x, g, b):
    mu = jnp.mean(x, axis=-1, keepdims=True)
    xc = x - mu
    return xc * lax.rsqrt(jnp.mean(xc * xc, axis=-1, keepdims=True) + EPS) * g + b


def _sigmoid(x):
    return 1.0 / (1.0 + jnp.exp(-x))


def _silu(x):
    return x * _sigmoid(x)


def _gelu_tanh(x):
    c = math.sqrt(2.0 / math.pi)
    return 0.5 * x * (1.0 + jnp.tanh(c * (x + 0.044715 * (x * x * x))))


def _log_sigmoid(x):
    return jnp.minimum(x, 0.0) - jnp.log(1.0 + jnp.exp(-jnp.abs(x)))


def _norm_proj_kernel(x_ref, g_ref, wh_ref, wt_ref, wlr_ref, z_ref, zlr_ref, h_ref, *, n_head):
    j = pl.program_id(1)

    @pl.when(j == 0)
    def _():
        h = _rmsnorm(x_ref[...], g_ref[...]).astype(BF16)
        h_ref[...] = h
        zlr_ref[...] = _dot(h, wlr_ref[...])

    @pl.when(j < n_head)
    def _():
        z_ref[...] = _dot(h_ref[...], wh_ref[...])

    @pl.when(j >= n_head)
    def _():
        z_ref[...] = _dot(h_ref[...], wt_ref[...])


def _norm_proj(x, g, w_head, w_tail, w_lr, l, tm, tn):
    m, d = x.shape
    n_head, n_tail = w_head.shape[2] // tn, w_tail.shape[2] // tn
    return pl.pallas_call(
        functools.partial(_norm_proj_kernel, n_head=n_head),
        grid=(m // tm, n_head + n_tail),
        in_specs=[
            pl.BlockSpec((tm, d), lambda i, j: (i, 0)),
            pl.BlockSpec((1, d), lambda i, j: (0, 0)),
            pl.BlockSpec((None, d, tn), lambda i, j: (l, 0, jnp.minimum(j, n_head - 1))),
            pl.BlockSpec((None, d, tn), lambda i, j: (l, 0, jnp.maximum(j - n_head, 0))),
            pl.BlockSpec((None, d, LANES), lambda i, j: (l, 0, 0)),
        ],
        out_specs=[
            pl.BlockSpec((tm, tn), lambda i, j: (i, j)),
            pl.BlockSpec((tm, LANES), lambda i, j: (i, 0)),
        ],
        out_shape=[jax.ShapeDtypeStruct((m, (n_head + n_tail) * tn), F32), jax.ShapeDtypeStruct((m, LANES), F32)],
        scratch_shapes=[pltpu.VMEM((tm, d), BF16)],
        compiler_params=pltpu.CompilerParams(
            dimension_semantics=("arbitrary", "arbitrary"), vmem_limit_bytes=VMEM_LIMIT),
    )(x, g, w_head, w_tail, w_lr)


def _proj_res_kernel(y_ref, w_ref, x_ref, g_ref, o_ref, h_ref, wb_ref):
    k = pl.program_id(0)

    @pl.when(k == 0)
    def _():
        o_ref[...] = x_ref[...]

    w = w_ref[...].astype(BF16)
    wb_ref[...] = w
    o_ref[...] += _dot(y_ref[...], w)

    @pl.when(k == pl.num_programs(0) - 1)
    def _():
        h_ref[...] = _rmsnorm(o_ref[...], g_ref[...]).astype(BF16)


def _proj_res(y, w, x, g, l, tk):
    m, k = y.shape
    n = w.shape[2]
    return pl.pallas_call(
        _proj_res_kernel,
        grid=(k // tk,),
        in_specs=[
            pl.BlockSpec((m, tk), lambda i: (0, i)),
            pl.BlockSpec((None, tk, n), lambda i: (l, i, 0)),
            pl.BlockSpec((m, n), lambda i: (0, 0)),
            pl.BlockSpec((1, n), lambda i: (0, 0)),
        ],
        out_specs=[pl.BlockSpec((m, n), lambda i: (0, 0)), pl.BlockSpec((m, n), lambda i: (0, 0)),
                   pl.BlockSpec((tk, n), lambda i: (i, 0))],
        out_shape=[jax.ShapeDtypeStruct((m, n), F32), jax.ShapeDtypeStruct((m, n), BF16),
                   jax.ShapeDtypeStruct((k, n), BF16)],
        compiler_params=pltpu.CompilerParams(
            dimension_semantics=("arbitrary",), vmem_limit_bytes=VMEM_LIMIT),
    )(y, w, x, g)


def _ffn_kernel(h_ref, x_ref, wg_ref, wu_ref, wd_ref, g_ref, o_ref, *cast_refs, final_norm):
    j = pl.program_id(1)

    @pl.when(j == 0)
    def _():
        o_ref[...] = x_ref[...]

    wg, wu, wd = wg_ref[...].astype(BF16), wu_ref[...].astype(BF16), wd_ref[...].astype(BF16)
    for ref, w in zip(cast_refs, (wg, wu, wd)):
        ref[...] = w
    h = h_ref[...]
    o_ref[...] += _dot((_silu(_dot(h, wg)) * _dot(h, wu)).astype(BF16), wd)

    if final_norm:
        @pl.when(j == pl.num_programs(1) - 1)
        def _():
            o_ref[...] = _rmsnorm(o_ref[...], g_ref[...])


def _ffn(h, x, wg, wu, wd, g, l, tm, tf, tile0, n_tiles, final_norm):
    m, d = x.shape
    f = wg.shape[-1]
    rows = lambda i, j: (tile0 + i, 0)
    if l is None:
        w_specs = [pl.BlockSpec((d, tf), lambda i, j: (0, j)), pl.BlockSpec((d, tf), lambda i, j: (0, j)),
                   pl.BlockSpec((tf, d), lambda i, j: (j, 0))]
        cast_specs, cast_shapes = [], []
    else:
        w_specs = [pl.BlockSpec((None, d, tf), lambda i, j: (l, 0, j)),
                   pl.BlockSpec((None, d, tf), lambda i, j: (l, 0, j)),
                   pl.BlockSpec((None, tf, d), lambda i, j: (l, j, 0))]
        cast_specs = [pl.BlockSpec((d, tf), lambda i, j: (0, j)), pl.BlockSpec((d, tf), lambda i, j: (0, j)),
                      pl.BlockSpec((tf, d), lambda i, j: (j, 0))]
        cast_shapes = [jax.ShapeDtypeStruct((d, f), BF16), jax.ShapeDtypeStruct((d, f), BF16),
                       jax.ShapeDtypeStruct((f, d), BF16)]
    return pl.pallas_call(
        functools.partial(_ffn_kernel, final_norm=final_norm),
        grid=(n_tiles, f // tf),
        in_specs=[pl.BlockSpec((tm, d), rows), pl.BlockSpec((tm, d), rows, pipeline_mode=pl.Buffered(1))]
                 + w_specs + [pl.BlockSpec((1, d), lambda i, j: (0, 0))],
        out_specs=[pl.BlockSpec((tm, d), rows)] + cast_specs,
        out_shape=[jax.ShapeDtypeStruct((m, d), F32)] + cast_shapes,
        input_output_aliases={1: 0},
        compiler_params=pltpu.CompilerParams(
            dimension_semantics=("arbitrary", "arbitrary"), vmem_limit_bytes=VMEM_LIMIT),
    )(h, x, wg, wu, wd, g)


def _gla_gate(zlr, a2_ref, ab_ref):
    xg = _dot(zlr.astype(BF16), a2_ref[...]) + ab_ref[...]
    return _log_sigmoid(xg) * (1.0 / GLA_TAU)


def _head_masks(qk):
    lane_head = lax.broadcasted_iota(jnp.int32, (1, qk), 1) // (qk // N_HEADS)
    return [(lane_head == h).astype(F32) for h in range(N_HEADS)]


def _gla_chunk(qc, kc, vc, bc, la_hi, la_lo, s_flat, e2, hm):
    c, qk = qc.shape
    dv = vc.shape[1] // N_HEADS
    dk = qk // N_HEADS
    nb = c // GLA_SUB
    vcb = vc.astype(BF16)
    sub_row = lax.broadcasted_iota(jnp.int32, (GLA_SUB, 1), 0)
    col_j = lax.broadcasted_iota(jnp.int32, (1, c), 1)

    o_diag = []
    for blk in range(nb):
        r0 = blk * GLA_SUB
        q_b, k_b, b_b = qc[r0:r0 + GLA_SUB], kc[r0:r0 + GLA_SUB], bc[r0:r0 + GLA_SUB]
        v_b = vc[r0:r0 + GLA_SUB]
        ts = []
        for j in range(GLA_SUB):
            dec = jnp.exp(jnp.minimum(b_b - b_b[j:j + 1], 0.0))
            ts.append(jnp.where(sub_row >= j, q_b * k_b[j:j + 1] * dec, 0.0).astype(BF16))
        r = _dot(jnp.concatenate(ts, axis=0), e2)
        acc = r[0:GLA_SUB] * v_b[0:1]
        for j in range(1, GLA_SUB):
            acc = acc + r[j * GLA_SUB:(j + 1) * GLA_SUB] * v_b[j:j + 1]
        o_diag.append(acc)
    o = jnp.concatenate(o_diag, axis=0)

    bref = [None] + [bc[blk * GLA_SUB - 1:blk * GLA_SUB] for blk in range(1, nb)]
    bref_rows = jnp.concatenate(
        [jnp.zeros((GLA_SUB, qk), F32)] + [jnp.broadcast_to(bref[blk], (GLA_SUB, qk)) for blk in range(1, nb)],
        axis=0)
    qt = qc * jnp.exp(bc - bref_rows)
    att_blk = [None]
    for blk in range(1, nb):
        kt = (kc * jnp.exp(jnp.minimum(bref[blk] - bc, 0.0))).astype(BF16)
        q_b = qt[blk * GLA_SUB:(blk + 1) * GLA_SUB]
        qm = jnp.concatenate([q_b * hm[h] for h in range(N_HEADS)], axis=0).astype(BF16)
        a = _dot_nt(qm, kt)
        att_blk.append(jnp.where(col_j < blk * GLA_SUB, a, 0.0))
    o_off = []
    for h in range(N_HEADS):
        att = jnp.concatenate(
            [jnp.zeros((GLA_SUB, c), F32)] + [att_blk[blk][h * GLA_SUB:(h + 1) * GLA_SUB] for blk in range(1, nb)],
            axis=0)
        o_off.append(_dot(att.astype(BF16), vcb[:, h * dv:(h + 1) * dv]))
    o = o + jnp.concatenate(o_off, axis=1)

    qe = qc * jnp.exp(bc)
    qe_m = jnp.concatenate([qe * hm[h] for h in range(N_HEADS)], axis=0).astype(BF16)
    o_int = _dot(qe_m, s_flat.astype(BF16))
    o = o + jnp.concatenate([o_int[h * c:(h + 1) * c] for h in range(N_HEADS)], axis=1)

    kk = (kc * jnp.exp(bc[c - 1:c] - bc)).astype(BF16)
    kv = _dot_tn(kk, vcb)
    kv_d = jnp.concatenate([kv[h * dk:(h + 1) * dk, h * dv:(h + 1) * dv] for h in range(N_HEADS)], axis=0)
    ones = jnp.ones((c, dv), BF16)
    decay = jnp.exp(_dot_tn(la_hi, ones) + _dot_tn(la_lo, ones))
    return o, decay * s_flat + kv_d


def _gla_out(o, gn, r):
    dv = o.shape[1] // N_HEADS
    parts = []
    for h in range(N_HEADS):
        oh = o[:, h * dv:(h + 1) * dv]
        parts.append(oh * lax.rsqrt(jnp.mean(oh * oh, axis=-1, keepdims=True) + EPS))
    return jnp.concatenate(parts, axis=1) * gn * _silu(r)


def _cols(g):
    qk = g // 2
    c = {}
    off = 0
    for name, width in (("a_x", g), ("a_b", g), ("a_c", g), ("q", qk), ("k", qk), ("v", g), ("r", g),
                        ("c_val", g), ("c_gate", g), ("d_u", g), ("d_v", g)):
        c[name] = (off, off + width)
        off += width
    return c, off


def _conv_taps(ccw_ref, cbuf, zbuf, tl, head, between):
    sub = 8
    base = head - (CF_WIDTH - 1)
    y = None
    for r in range(sub):
        taps = [k for k in range(CF_WIDTH) if (base + k) % sub == r]
        rows = tl if r == 0 else tl + sub
        part = None
        for k in taps:
            off = base + k - r
            term = ccw_ref[k:k + 1, :] * cbuf[off:off + rows, :]
            part = term if part is None else part + term
        if r == 0:
            shifted = part
        else:
            zbuf[0:rows, :] = part
            shifted = zbuf[r:r + tl, :]
        y = shifted if y is None else y + shifted
        between()
    return y


def _front_kernel(x_ref, gm_ref, wh_ref, wt_ref, wlr_ref, caw_ref, a2_ref, ab_ref, gn_ref, ccw_ref, ccb_ref,
                  lcg_ref, lcb_ref, ldg_ref, ldb_ref, sgw_ref, sgb_ref, tri_ref, e2_ref, wo_ref, gf_ref,
                  o_ref, hn_ref, na_ref, s_ref, nc_ref, sv_ref, abuf, cbuf, zbuf):
    t = pl.program_id(1)
    last = pl.num_programs(1) - 1
    tl = x_ref.shape[0]
    g = gn_ref.shape[1]
    qk = g // 2
    dv = g // N_HEADS
    cols, _ = _cols(g)
    a_head, c_head = 8, 32

    @pl.when(t == 0)
    def _():
        abuf[0:a_head, :] = jnp.zeros((a_head, g), F32)
        cbuf[0:c_head, :] = jnp.zeros((c_head, g), F32)
        s_ref[...] = jnp.zeros(s_ref.shape, F32)

    x = x_ref[...]
    h = _rmsnorm(x, gm_ref[...]).astype(BF16)

    slab = PROJ_SLAB
    order = [c0 for n in ("c_val", "c_gate", "a_x", "a_b", "a_c", "d_u", "d_v", "q", "k", "v", "r")
             for c0 in range(cols[n][0], cols[n][1], slab) if c0 % slab == 0]
    pending = list(dict.fromkeys(order))
    slabs = {}

    n_head = wh_ref.shape[1]

    def pump():
        if pending:
            c0 = pending.pop(0)
            w = wh_ref[:, c0:c0 + slab] if c0 < n_head else wt_ref[:, c0 - n_head:c0 - n_head + slab]
            slabs[c0] = _dot(h, w)

    def proj(*names):
        out = []
        for n in names:
            lo, hi = cols[n]
            parts = []
            for c0 in range(lo - lo % slab, hi, slab):
                while c0 not in slabs:
                    pump()
                s_lo, s_hi = max(lo, c0) - c0, min(hi, c0 + slab) - c0
                parts.append(slabs[c0][:, s_lo:s_hi])
            out.append(parts[0] if len(parts) == 1 else jnp.concatenate(parts, axis=1))
        return out

    def out_proj(y, idx):
        return _dot(y.astype(BF16), wo_ref[idx * g:(idx + 1) * g, :])

    c_val, c_gate = proj("c_val", "c_gate")
    cin = c_val * _sigmoid(c_gate)
    cbuf[c_head:c_head + tl, :] = cin
    conv_c = _conv_taps(ccw_ref, cbuf, zbuf, tl, c_head, pump) + ccb_ref[...]
    acc = x + out_proj(_silu(_layernorm(conv_c, lcg_ref[...], lcb_ref[...])), 2)

    a_x, a_b, a_c = proj("a_x", "a_b", "a_c")
    xa = a_c * a_x
    abuf[a_head:a_head + tl, :] = xa
    conv = caw_ref[SC_WIDTH - 1:SC_WIDTH, :] * xa
    for kk in range(SC_WIDTH - 1):
        sh = SC_WIDTH - 1 - kk
        conv = conv + caw_ref[kk:kk + 1, :] * abuf[a_head - sh:a_head - sh + tl, :]
    acc = acc + out_proj(a_b * conv, 0)

    d_u, d_v = proj("d_u", "d_v")
    du = _gelu_tanh(d_u)
    vd = _layernorm(_gelu_tanh(d_v), ldg_ref[...], ldb_ref[...])
    sv_ref[...] = vd[tl - SG_CHUNK:tl]
    vdb = vd.astype(BF16)
    row = lax.broadcasted_iota(jnp.int32, (SG_CHUNK, SG_CHUNK), 0)
    col = lax.broadcasted_iota(jnp.int32, (SG_CHUNK, SG_CHUNK), 1)
    ws = [jnp.where(row >= col, sgw_ref[hh], 0.0).astype(BF16) for hh in range(N_HEADS)]
    sv_rows = []
    for ci in range(tl // SG_CHUNK):
        sl = slice(ci * SG_CHUNK, (ci + 1) * SG_CHUNK)
        sv_rows.append(jnp.concatenate(
            [_dot(ws[hh], vdb[sl, hh * dv:(hh + 1) * dv]) for hh in range(N_HEADS)], axis=1) + sgb_ref[...])
    acc = acc + out_proj(du * jnp.concatenate(sv_rows, axis=0), 3)

    q_all, k_all, v_all, r_all = proj("q", "k", "v", "r")
    la = _gla_gate(_dot(h, wlr_ref[...]), a2_ref, ab_ref)
    la_hi, la_lo = _split_bf16(la)
    tri = tri_ref[...]
    b_all = _dot(tri, la_hi) + _dot(tri, la_lo)
    q_all = q_all * (float(qk // N_HEADS) ** -0.5)
    hm = _head_masks(qk)
    e2 = e2_ref[...]
    s_flat = s_ref[...]
    outs = []
    for ci in range(tl // GLA_CHUNK):
        sl = slice(ci * GLA_CHUNK, (ci + 1) * GLA_CHUNK)
        o_c, s_flat = _gla_chunk(q_all[sl], k_all[sl], v_all[sl], b_all[sl], la_hi[sl], la_lo[sl], s_flat, e2, hm)
        outs.append(o_c)
    s_ref[...] = s_flat
    acc = acc + out_proj(_gla_out(jnp.concatenate(outs, axis=0), gn_ref[...], r_all), 1)
    o_ref[...] = acc
    hn_ref[...] = _rmsnorm(acc, gf_ref[...]).astype(BF16)

    @pl.when(t == last)
    def _():
        na_ref[...] = abuf[a_head + tl - (SC_WIDTH - 1):a_head + tl, :]
        nc_ref[...] = cbuf[c_head + tl - (CF_WIDTH - 1):c_head + tl, :]

    abuf[0:a_head, :] = abuf[tl:tl + a_head, :]
    cbuf[0:c_head, :] = cbuf[tl:tl + c_head, :]


def _full(shape):
    nd = len(shape)
    return pl.BlockSpec(shape, lambda *_: (0,) * nd)


def _resident(a, l=None):
    if l is None:
        return pl.BlockSpec(a.shape, lambda *_: (0,) * a.ndim, pipeline_mode=pl.Buffered(1))
    return pl.BlockSpec((None,) + a.shape[1:], lambda *_: (l,) + (0,) * (a.ndim - 1), pipeline_mode=pl.Buffered(1))


def _front(x, gm, w_head, w_tail, w_lr, w_o, gf, l, batch, seq, wts, tl):
    d = x.shape[1]
    g = wts["gn"].shape[1]
    nt = seq // tl
    row_map = lambda b, t: (b * nt + t, 0)
    names = ("caw", "a2", "ab", "gn", "ccw", "ccb", "lcg", "lcb", "ldg", "ldb", "sgw", "sgb", "tri", "e2")
    consts = [wts[n] for n in names]
    return pl.pallas_call(
        _front_kernel,
        grid=(batch, nt),
        in_specs=[pl.BlockSpec((tl, d), row_map), _full(gm.shape),
                  _resident(w_head, l), _resident(w_tail, l), _resident(w_lr, l)]
                 + [_full(c.shape) for c in consts] + [_resident(w_o), _full(gf.shape)],
        out_specs=[
            pl.BlockSpec((tl, d), row_map),
            pl.BlockSpec((tl, d), row_map),
            pl.BlockSpec((None, SC_WIDTH - 1, g), lambda b, t: (b, 0, 0)),
            pl.BlockSpec((None, g // 2, g // N_HEADS), lambda b, t: (b, 0, 0)),
            pl.BlockSpec((None, CF_WIDTH - 1, g), lambda b, t: (b, 0, 0)),
            pl.BlockSpec((None, SG_CHUNK, g), lambda b, t: (b, 0, 0)),
        ],
        out_shape=[
            jax.ShapeDtypeStruct(x.shape, F32),
            jax.ShapeDtypeStruct(x.shape, BF16),
            jax.ShapeDtypeStruct((batch, SC_WIDTH - 1, g), F32),
            jax.ShapeDtypeStruct((batch, g // 2, g // N_HEADS), F32),
            jax.ShapeDtypeStruct((batch, CF_WIDTH - 1, g), F32),
            jax.ShapeDtypeStruct((batch, SG_CHUNK, g), F32),
        ],
        scratch_shapes=[pltpu.VMEM((8 + tl, g), F32), pltpu.VMEM((32 + tl, g), F32), pltpu.VMEM((8 + tl, g), F32)],
        compiler_params=pltpu.CompilerParams(
            dimension_semantics=("arbitrary", "arbitrary"), vmem_limit_bytes=VMEM_LIMIT),
    )(x, gm, w_head, w_tail, w_lr, *consts, w_o, gf)


def _mixer_sample_kernel(z_ref, zlr_ref, pa_ref, s_ref, pc_ref, caw_ref, a2_ref, ab_ref, gn_ref, ccw_ref, ccb_ref,
                         lcg_ref, lcb_ref, ldg_ref, ldb_ref, sgw0_ref, sgb0_ref,
                         y_ref, na_ref, so_ref, nc_ref, sv_ref, o_scr, *, layer):
    if layer is not None:
        for ll in range(pa_ref.shape[0]):
            if ll != layer:
                na_ref[ll] = pa_ref[ll]
                so_ref[ll] = s_ref[ll]
                nc_ref[ll] = pc_ref[ll]
        pa_ref, s_ref, pc_ref = pa_ref.at[layer], s_ref.at[layer], pc_ref.at[layer]
        na_ref, so_ref, nc_ref = na_ref.at[layer], so_ref.at[layer], nc_ref.at[layer]
    bt = z_ref.shape[0]
    g = gn_ref.shape[1]
    qk = g // 2
    dk = qk // N_HEADS
    dv = g // N_HEADS
    cols, _ = _cols(g)

    def zc(name):
        lo, hi = cols[name]
        return z_ref[:, lo:hi]

    xa = zc("a_c") * zc("a_x")
    conv = caw_ref[SC_WIDTH - 1:SC_WIDTH, :] * xa
    for kk in range(SC_WIDTH - 1):
        conv = conv + caw_ref[kk:kk + 1, :] * pa_ref[:, kk, :]
    y_ref[:, 0:g] = (zc("a_b") * conv).astype(BF16)
    for kk in range(1, SC_WIDTH - 1):
        na_ref[:, kk - 1, :] = pa_ref[:, kk, :]
    na_ref[:, SC_WIDTH - 2, :] = xa

    la = _gla_gate(zlr_ref[...], a2_ref, ab_ref)
    a_t = jnp.exp(la).T
    k_t = zc("k").T
    q_rows = zc("q") * (float(dk) ** -0.5)
    v_rows = zc("v")
    hm = _head_masks(qk)
    hm_rows = jnp.concatenate(hm + [jnp.zeros_like(hm[0])] * (8 - N_HEADS), axis=0)
    for b in range(bt):
        a_col = jnp.broadcast_to(a_t[:, b:b + 1], (qk, dv))
        k_col = jnp.broadcast_to(k_t[:, b:b + 1], (qk, dv))
        v_b = jnp.concatenate(
            [jnp.broadcast_to(v_rows[b:b + 1, h * dv:(h + 1) * dv], (dk, dv)) for h in range(N_HEADS)], axis=0)
        s_new = a_col * s_ref[b] + k_col * v_b
        so_ref[b] = s_new
        q_m = (q_rows[b:b + 1] * hm_rows).astype(BF16)
        o_b = _dot(q_m, s_new.astype(BF16))
        for h in range(N_HEADS):
            o_scr[b:b + 1, h * dv:(h + 1) * dv] = o_b[h:h + 1]
    y_ref[:, g:2 * g] = _gla_out(o_scr[...], gn_ref[...], zc("r")).astype(BF16)

    cin = zc("c_val") * _sigmoid(zc("c_gate"))
    acc = ccb_ref[...] + ccw_ref[CF_WIDTH - 1:CF_WIDTH, :] * cin
    for kk in range(CF_WIDTH - 1):
        acc = acc + ccw_ref[kk:kk + 1, :] * pc_ref[:, kk, :]
    y_ref[:, 2 * g:3 * g] = _silu(_layernorm(acc, lcg_ref[...], lcb_ref[...])).astype(BF16)
    nc_ref[:, 0:CF_WIDTH - 2, :] = pc_ref[:, 1:CF_WIDTH - 1, :]
    nc_ref[:, CF_WIDTH - 2, :] = cin

    du = _gelu_tanh(zc("d_u"))
    vd = _layernorm(_gelu_tanh(zc("d_v")), ldg_ref[...], ldb_ref[...])
    sv_ref[...] = vd
    y_ref[:, 3 * g:4 * g] = (du * (sgw0_ref[...] * vd + sgb0_ref[...])).astype(BF16)


def _mixer_sample(z, zlr, pa, s, pc, wts, l, bt, in_place):
    nb, g = z.shape[0], wts["gn"].shape[1]
    names = ("caw", "a2", "ab", "gn", "ccw", "ccb", "lcg", "lcb", "ldg", "ldb", "sgw0", "sgb0")
    consts = [wts[n] for n in names]
    rows = lambda i: (i, 0)
    if in_place:
        state_spec = lambda a: pl.BlockSpec((None, bt) + a.shape[2:], lambda i: (l, i, 0, 0))
    else:
        state_spec = lambda a: pl.BlockSpec((a.shape[0], bt) + a.shape[2:], lambda i: (0, i, 0, 0))
    return pl.pallas_call(
        functools.partial(_mixer_sample_kernel, layer=None if in_place else l),
        grid=(nb // bt,),
        in_specs=[pl.BlockSpec((bt, z.shape[1]), rows), pl.BlockSpec((bt, LANES), rows),
                  state_spec(pa), state_spec(s), state_spec(pc)] + [_full(c.shape) for c in consts],
        out_specs=[pl.BlockSpec((bt, 4 * g), rows), state_spec(pa), state_spec(s), state_spec(pc),
                   pl.BlockSpec((bt, g), rows)],
        out_shape=[
            jax.ShapeDtypeStruct((nb, 4 * g), BF16),
            jax.ShapeDtypeStruct(pa.shape, F32),
            jax.ShapeDtypeStruct(s.shape, F32),
            jax.ShapeDtypeStruct(pc.shape, F32),
            jax.ShapeDtypeStruct((nb, g), F32),
        ],
        input_output_aliases={2: 1, 3: 2, 4: 3} if in_place else {},
        scratch_shapes=[pltpu.VMEM((bt, g), F32)],
        compiler_params=pltpu.CompilerParams(
            dimension_semantics=("arbitrary",), vmem_limit_bytes=VMEM_LIMIT),
    )(z, zlr, pa, s, pc, *consts)


def _largest_tile(m, cap, mult):
    t = min(m, cap)
    while m % t or t % mult:
        t -= mult
    return t


def _layer_weights(l, g, conv_a_w, gla_a2, gla_a_bias, gla_norm, conv_c_w, conv_c_b,
                   ln_c_g, ln_c_b, ln_d_g, ln_d_b, sg_w, sg_b):
    dv = g // N_HEADS
    row = lambda v: v.reshape(1, -1)
    return {
        "caw": conv_a_w[l],
        "a2": jnp.pad(gla_a2[l], ((0, LANES - GLA_LOWRANK), (0, 0))).astype(BF16),
        "ab": row(gla_a_bias[l]), "gn": row(gla_norm[l]),
        "ccw": conv_c_w[l], "ccb": row(conv_c_b[l]),
        "lcg": row(ln_c_g[l]), "lcb": row(ln_c_b[l]), "ldg": row(ln_d_g[l]), "ldb": row(ln_d_b[l]),
        "sgw": sg_w[l],
        "sgb": jnp.repeat(sg_b[l].T, dv, axis=1),
        "sgw0": row(jnp.repeat(sg_w[l][:, 0, 0], dv)),
        "sgb0": row(jnp.repeat(sg_b[l][:, 0], dv)),
    }


def _gla_constants(g, tl):
    qk, dv = g // 2, g // N_HEADS
    r = jnp.arange(tl)
    tri = ((r[:, None] >= r[None, :]) & (r[:, None] // GLA_CHUNK == r[None, :] // GLA_CHUNK)).astype(BF16)
    e2 = (jnp.arange(qk)[:, None] // (qk // N_HEADS) == jnp.arange(g)[None, :] // dv).astype(BF16)
    return tri, e2


def kernel(x_prompt, x_sample, state_conv_a, state_gla, state_conv_c, norm_mix, w_in, conv_a_w, gla_a2, gla_a_bias,
           gla_norm, conv_c_w, conv_c_b, ln_c_g, ln_c_b, ln_d_g, ln_d_b, sg_w, sg_b, w_o, norm_ffn, w_gate, w_up,
           w_down, norm_final):
    batch, seq, d = x_prompt.shape
    nb, dec_seq, _ = x_sample.shape
    depth = w_in.shape[0]
    g = d // 4
    qk, dv = g // 2, g // N_HEADS
    assert dec_seq == 1 and seq % SG_CHUNK == 0 and g % (N_HEADS * LANES) == 0
    dff = w_gate.shape[2]
    n_main = w_in.shape[2] - GLA_LOWRANK
    tl = _largest_tile(seq, FRONT_ROWS, SG_CHUNK)
    tri, e2 = _gla_constants(g, tl)

    lr0 = 3 * g + 2 * qk + 2 * g
    w_head = w_in[:, :, :lr0].astype(BF16)
    w_tail = w_in[:, :, lr0 + GLA_LOWRANK:].astype(BF16)
    w_lr = jnp.pad(w_in[:, :, lr0:lr0 + GLA_LOWRANK], ((0, 0), (0, 0), (0, LANES - GLA_LOWRANK))).astype(BF16)

    mp = batch * seq
    tm_p = _largest_tile(mp, 1024, 16)
    tm_s = nb
    tn_in = _largest_tile(math.gcd(n_main, lr0), 1024, 256)
    tf_cast = _largest_tile(dff, 256, 256)
    tf = _largest_tile(dff, 512, 256)

    xp = x_prompt.reshape(mp, d)
    xs = x_sample.reshape(nb, d)
    st_a, st_c = state_conv_a, state_conv_c
    st_s = state_gla.reshape(depth, nb, qk, dv)
    outs_p = {k: [] for k in ("a", "s", "c", "v")}
    sv_s = []
    for l in range(depth):
        wts = _layer_weights(l, g, conv_a_w, gla_a2, gla_a_bias, gla_norm, conv_c_w, conv_c_b,
                             ln_c_g, ln_c_b, ln_d_g, ln_d_b, sg_w, sg_b)
        wts["tri"], wts["e2"] = tri, e2
        gm, gf = norm_mix[l].reshape(1, d), norm_ffn[l].reshape(1, d)
        gfin = norm_final.reshape(1, d)
        final = l == depth - 1

        z, zlr = _norm_proj(xs, gm, w_head, w_tail, w_lr, l, tm_s, tn_in)
        y, st_a, st_s, st_c, sv = _mixer_sample(z, zlr, st_a, st_s, st_c, wts, l, 32 if l else 16, l > 0)
        xs, hs, w_ob = _proj_res(y, w_o, xs, gf, l, tn_in)

        xp, hn, na, s_new, nc, svp = _front(xp, gm, w_head, w_tail, w_lr, w_ob, gf, l, batch, seq, wts, tl)
        xp, wg_b, wu_b, wd_b = _ffn(hn, xp, w_gate, w_up, w_down, gfin, l, tm_p, tf_cast, 0, 1, final)
        if mp > tm_p:
            xp, = _ffn(hn, xp, wg_b, wu_b, wd_b, gfin, None, tm_p, tf, 1, mp // tm_p - 1, final)
        outs_p["a"].append(na)
        outs_p["s"].append(s_new.reshape(batch, N_HEADS, qk // N_HEADS, dv))
        outs_p["c"].append(nc)
        outs_p["v"].append(svp)

        xs, = _ffn(hs, xs, wg_b, wu_b, wd_b, gfin, None, tm_s, tf, 0, 1, final)
        sv_s.append(sv.reshape(nb, 1, g))

    st = jnp.stack
    return (xp.reshape(batch, seq, d), xs.reshape(nb, 1, d),
            st(outs_p["a"]), st_a, st(outs_p["s"]), st_s.reshape(depth, nb, N_HEADS, qk // N_HEADS, dv),
            st(outs_p["c"]), st_c, st(outs_p["v"]), st(sv_s))
```

```python
import functools
import math

import jax
import jax.numpy as jnp
from jax import lax
from jax.experimental import pallas as pl
from jax.experimental.pallas import tpu as pltpu

F32 = jnp.float32
BF16 = jnp.bfloat16
EPS = 1e-6

N_HEADS = 4
GLA_LOWRANK = 16
GLA_TAU = 16.0
SC_WIDTH = 3
CF_WIDTH = 31
SG_CHUNK = 128
GLA_CHUNK = 64
GLA_SUB = 16
FRONT_ROWS = 256
PROJ_SLAB = 256
LANES = 128
VMEM_LIMIT = 60 * 1024 * 1024


def _dot(a, b):
    return jnp.dot(a, b, preferred_element_type=F32)


def _dot_nt(a, b):
    return lax.dot_general(a, b, (((1,), (1,)), ((), ())), preferred_element_type=F32)


def _dot_tn(a, b):
    return lax.dot_general(a, b, (((0,), (0,)), ((), ())), preferred_element_type=F32)


def _split_bf16(x):
    hi = x.astype(BF16)
    lo = (x - hi.astype(F32)).astype(BF16)
    return hi, lo


def _rmsnorm(x, g):
    return x * lax.rsqrt(jnp.mean(x * x, axis=-1, keepdims=True) + EPS) * g


def _layernorm(x, g, b):
    mu = jnp.mean(x, axis=-1, keepdims=True)
    xc = x - mu
    return xc * lax.rsqrt(jnp.mean(xc * xc, axis=-1, keepdims=True) + EPS) * g + b


def _sigmoid(x):
    return 1.0 / (1.0 + jnp.exp(-x))


def _silu(x):
    return x * _sigmoid(x)


def _gelu_tanh(x):
    c = math.sqrt(2.0 / math.pi)
    return 0.5 * x * (1.0 + jnp.tanh(c * (x + 0.044715 * (x * x * x))))


def _log_sigmoid(x):
    return jnp.minimum(x, 0.0) - jnp.log(1.0 + jnp.exp(-jnp.abs(x)))


def _norm_proj_kernel(x_ref, g_ref, wh_ref, wt_ref, wlr_ref, z_ref, zlr_ref, h_ref, *, n_head):
    j = pl.program_id(1)

    @pl.when(j == 0)
    def _():
        h = _rmsnorm(x_ref[...], g_ref[...]).astype(BF16)
        h_ref[...] = h
        zlr_ref[...] = _dot(h, wlr_ref[...])

    @pl.when(j < n_head)
    def _():
        z_ref[...] = _dot(h_ref[...], wh_ref[...])

    @pl.when(j >= n_head)
    def _():
        z_ref[...] = _dot(h_ref[...], wt_ref[...])


def _norm_proj(x, g, w_head, w_tail, w_lr, l, tm, tn):
    m, d = x.shape
    n_head, n_tail = w_head.shape[2] // tn, w_tail.shape[2] // tn
    return pl.pallas_call(
        functools.partial(_norm_proj_kernel, n_head=n_head),
        grid=(m // tm, n_head + n_tail),
        in_specs=[
            pl.BlockSpec((tm, d), lambda i, j: (i, 0)),
            pl.BlockSpec((1, d), lambda i, j: (0, 0)),
            pl.BlockSpec((None, d, tn), lambda i, j: (l, 0, jnp.minimum(j, n_head - 1))),
            pl.BlockSpec((None, d, tn), lambda i, j: (l, 0, jnp.maximum(j - n_head, 0))),
            pl.BlockSpec((None, d, LANES), lambda i, j: (l, 0, 0)),
        ],
        out_specs=[
            pl.BlockSpec((tm, tn), lambda i, j: (i, j)),
            pl.BlockSpec((tm, LANES), lambda i, j: (i, 0)),
        ],
        out_shape=[jax.ShapeDtypeStruct((m, (n_head + n_tail) * tn), F32), jax.ShapeDtypeStruct((m, LANES), F32)],
        scratch_shapes=[pltpu.VMEM((tm, d), BF16)],
        compiler_params=pltpu.CompilerParams(
            dimension_semantics=("arbitrary", "arbitrary"), vmem_limit_bytes=VMEM_LIMIT),
    )(x, g, w_head, w_tail, w_lr)


def _proj_res_kernel(y_ref, w_ref, x_ref, g_ref, o_ref, h_ref, wb_ref):
    k = pl.program_id(0)

    @pl.when(k == 0)
    def _():
        o_ref[...] = x_ref[...]

    w = w_ref[...].astype(BF16)
    wb_ref[...] = w
    o_ref[...] += _dot(y_ref[...], w)

    @pl.when(k == pl.num_programs(0) - 1)
    def _():
        h_ref[...] = _rmsnorm(o_ref[...], g_ref[...]).astype(BF16)


def _proj_res(y, w, x, g, l, tk):
    m, k = y.shape
    n = w.shape[2]
    return pl.pallas_call(
        _proj_res_kernel,
        grid=(k // tk,),
        in_specs=[
            pl.BlockSpec((m, tk), lambda i: (0, i)),
            pl.BlockSpec((None, tk, n), lambda i: (l, i, 0)),
            pl.BlockSpec((m, n), lambda i: (0, 0)),
            pl.BlockSpec((1, n), lambda i: (0, 0)),
        ],
        out_specs=[pl.BlockSpec((m, n), lambda i: (0, 0)), pl.BlockSpec((m, n), lambda i: (0, 0)),
                   pl.BlockSpec((tk, n), lambda i: (i, 0))],
        out_shape=[jax.ShapeDtypeStruct((m, n), F32), jax.ShapeDtypeStruct((m, n), BF16),
                   jax.ShapeDtypeStruct((k, n), BF16)],
        compiler_params=pltpu.CompilerParams(
            dimension_semantics=("arbitrary",), vmem_limit_bytes=VMEM_LIMIT),
    )(y, w, x, g)


def _ffn_kernel(h_ref, x_ref, wg_ref, wu_ref, wd_ref, g_ref, o_ref, *cast_refs, final_norm):
    j = pl.program_id(1)

    @pl.when(j == 0)
    def _():
        o_ref[...] = x_ref[...]

    wg, wu, wd = wg_ref[...].astype(BF16), wu_ref[...].astype(BF16), wd_ref[...].astype(BF16)
    for ref, w in zip(cast_refs, (wg, wu, wd)):
        ref[...] = w
    h = h_ref[...]
    o_ref[...] += _dot((_silu(_dot(h, wg)) * _dot(h, wu)).astype(BF16), wd)

    if final_norm:
        @pl.when(j == pl.num_programs(1) - 1)
        def _():
            o_ref[...] = _rmsnorm(o_ref[...], g_ref[...])


def _ffn(h, x, wg, wu, wd, g, l, tm, tf, tile0, n_tiles, final_norm):
    m, d = x.shape
    f = wg.shape[-1]
    rows = lambda i, j: (tile0 + i, 0)
    if l is None:
        w_specs = [pl.BlockSpec((d, tf), lambda i, j: (0, j)), pl.BlockSpec((d, tf), lambda i, j: (0, j)),
                   pl.BlockSpec((tf, d), lambda i, j: (j, 0))]
        cast_specs, cast_shapes = [], []
    else:
        w_specs = [pl.BlockSpec((None, d, tf), lambda i, j: (l, 0, j)),
                   pl.BlockSpec((None, d, tf), lambda i, j: (l, 0, j)),
                   pl.BlockSpec((None, tf, d), lambda i, j: (l, j, 0))]
        cast_specs = [pl.BlockSpec((d, tf), lambda i, j: (0, j)), pl.BlockSpec((d, tf), lambda i, j: (0, j)),
                      pl.BlockSpec((tf, d), lambda i, j: (j, 0))]
        cast_shapes = [jax.ShapeDtypeStruct((d, f), BF16), jax.ShapeDtypeStruct((d, f), BF16),
                       jax.ShapeDtypeStruct((f, d), BF16)]
    return pl.pallas_call(
        functools.partial(_ffn_kernel, final_norm=final_norm),
        grid=(n_tiles, f // tf),
        in_specs=[pl.BlockSpec((tm, d), rows), pl.BlockSpec((tm, d), rows)]
                 + w_specs + [pl.BlockSpec((1, d), lambda i, j: (0, 0))],
        out_specs=[pl.BlockSpec((tm, d), rows)] + cast_specs,
        out_shape=[jax.ShapeDtypeStruct((m, d), F32)] + cast_shapes,
        input_output_aliases={1: 0},
        compiler_params=pltpu.CompilerParams(
            dimension_semantics=("arbitrary", "arbitrary"), vmem_limit_bytes=VMEM_LIMIT),
    )(h, x, wg, wu, wd, g)


def _gla_gate(zlr, a2_ref, ab_ref):
    xg = _dot(zlr.astype(BF16), a2_ref[...]) + ab_ref[...]
    return _log_sigmoid(xg) * (1.0 / GLA_TAU)


def _head_masks(qk):
    lane_head = lax.broadcasted_iota(jnp.int32, (1, qk), 1) // (qk // N_HEADS)
    return [(lane_head == h).astype(F32) for h in range(N_HEADS)]


def _gla_chunk(qc, kc, vc, bc, la_hi, la_lo, s_flat, e2, hm):
    c, qk = qc.shape
    dv = vc.shape[1] // N_HEADS
    dk = qk // N_HEADS
    nb = c // GLA_SUB
    vcb = vc.astype(BF16)
    sub_row = lax.broadcasted_iota(jnp.int32, (GLA_SUB, 1), 0)
    col_j = lax.broadcasted_iota(jnp.int32, (1, c), 1)

    o_diag = []
    for blk in range(nb):
        r0 = blk * GLA_SUB
        q_b, k_b, b_b = qc[r0:r0 + GLA_SUB], kc[r0:r0 + GLA_SUB], bc[r0:r0 + GLA_SUB]
        v_b = vc[r0:r0 + GLA_SUB]
        ts = []
        for j in range(GLA_SUB):
            dec = jnp.exp2(jnp.where(sub_row >= j, b_b - b_b[j:j + 1], -jnp.inf))
            ts.append(q_b * k_b[j:j + 1] * dec)
        r = _dot(jnp.concatenate(ts, axis=0).astype(BF16), e2)
        acc = r[0:GLA_SUB] * v_b[0:1]
        for j in range(1, GLA_SUB):
            acc = acc + r[j * GLA_SUB:(j + 1) * GLA_SUB] * v_b[j:j + 1]
        o_diag.append(acc)
    o = jnp.concatenate(o_diag, axis=0)

    bref = [None] + [bc[blk * GLA_SUB - 1:blk * GLA_SUB] for blk in range(1, nb)]
    bref_rows = jnp.concatenate(
        [jnp.zeros((GLA_SUB, qk), F32)] + [jnp.broadcast_to(bref[blk], (GLA_SUB, qk)) for blk in range(1, nb)],
        axis=0)
    qt = qc * jnp.exp2(bc - bref_rows)
    att_blk = [None]
    for blk in range(1, nb):
        kt = (kc * jnp.exp2(jnp.minimum(bref[blk] - bc, 0.0))).astype(BF16)
        q_b = qt[blk * GLA_SUB:(blk + 1) * GLA_SUB]
        qm = jnp.concatenate([q_b * hm[h] for h in range(N_HEADS)], axis=0).astype(BF16)
        a = _dot_nt(qm, kt)
        att_blk.append(jnp.where(col_j < blk * GLA_SUB, a, 0.0))
    o_off = []
    for h in range(N_HEADS):
        att = jnp.concatenate(
            [jnp.zeros((GLA_SUB, c), F32)] + [att_blk[blk][h * GLA_SUB:(h + 1) * GLA_SUB] for blk in range(1, nb)],
            axis=0)
        o_off.append(_dot(att.astype(BF16), vcb[:, h * dv:(h + 1) * dv]))
    o = o + jnp.concatenate(o_off, axis=1)

    qe = qc * jnp.exp2(bc)
    qe_m = jnp.concatenate([qe * hm[h] for h in range(N_HEADS)], axis=0).astype(BF16)
    o_int = _dot(qe_m, s_flat.astype(BF16))
    o = o + jnp.concatenate([o_int[h * c:(h + 1) * c] for h in range(N_HEADS)], axis=1)

    kk = (kc * jnp.exp2(bc[c - 1:c] - bc)).astype(BF16)
    kv = _dot_tn(kk, vcb)
    kv_d = jnp.concatenate([kv[h * dk:(h + 1) * dk, h * dv:(h + 1) * dv] for h in range(N_HEADS)], axis=0)
    ones = jnp.ones((c, dv), BF16)
    decay = jnp.exp(_dot_tn(la_hi, ones) + _dot_tn(la_lo, ones))
    return o, decay * s_flat + kv_d


def _gla_out(o, gn, r):
    dv = o.shape[1] // N_HEADS
    parts = []
    for h in range(N_HEADS):
        oh = o[:, h * dv:(h + 1) * dv]
        parts.append(oh * lax.rsqrt(jnp.mean(oh * oh, axis=-1, keepdims=True) + EPS))
    return jnp.concatenate(parts, axis=1) * gn * _silu(r)


def _cols(g):
    qk = g // 2
    c = {}
    off = 0
    for name, width in (("a_x", g), ("a_b", g), ("a_c", g), ("q", qk), ("k", qk), ("v", g), ("r", g),
                        ("c_val", g), ("c_gate", g), ("d_u", g), ("d_v", g)):
        c[name] = (off, off + width)
        off += width
    return c, off


def _conv_taps(ccw_ref, cbuf, zbuf, tl, head, between):
    sub = 8
    base = head - (CF_WIDTH - 1)
    cols = []
    for c0 in range(0, cbuf.shape[1], LANES):
        lanes = slice(c0, c0 + LANES)
        y = None
        for r in range(sub):
            taps = [k for k in range(CF_WIDTH) if (base + k) % sub == r]
            rows = tl if r == 0 else tl + sub
            part = None
            for k in taps:
                off = base + k - r
                term = ccw_ref[k:k + 1, lanes] * cbuf[off:off + rows, lanes]
                part = term if part is None else part + term
            if r == 0:
                shifted = part
            else:
                zbuf[0:rows, lanes] = part
                shifted = zbuf[r:r + tl, lanes]
            y = shifted if y is None else y + shifted
            between()
        cols.append(y)
    return jnp.concatenate(cols, axis=1)


def _front_kernel(x_ref, gm_ref, wh_ref, wt_ref, wlr_ref, caw_ref, a2_ref, ab_ref, gn_ref, ccw_ref, ccb_ref,
                  lcg_ref, lcb_ref, ldg_ref, ldb_ref, sgw_ref, sgb_ref, tri_ref, e2_ref, wo_ref, gf_ref,
                  o_ref, hn_ref, na_ref, s_ref, nc_ref, sv_ref, abuf, cbuf, zbuf):
    t = pl.program_id(1)
    last = pl.num_programs(1) - 1
    tl = x_ref.shape[0]
    g = gn_ref.shape[1]
    qk = g // 2
    dv = g // N_HEADS
    cols, _ = _cols(g)
    a_head, c_head = 8, 32

    @pl.when(t == 0)
    def _():
        abuf[0:a_head, :] = jnp.zeros((a_head, g), F32)
        cbuf[0:c_head, :] = jnp.zeros((c_head, g), F32)
        s_ref[...] = jnp.zeros(s_ref.shape, F32)

    x = x_ref[...]
    h = _rmsnorm(x, gm_ref[...]).astype(BF16)

    slab = PROJ_SLAB
    order = [c0 for n in ("c_val", "c_gate", "a_x", "a_b", "a_c", "d_u", "d_v", "q", "k", "v", "r")
             for c0 in range(cols[n][0], cols[n][1], slab) if c0 % slab == 0]
    pending = list(dict.fromkeys(order))
    slabs = {}

    n_head = wh_ref.shape[1]

    def pump():
        if pending:
            c0 = pending.pop(0)
            w = wh_ref[:, c0:c0 + slab] if c0 < n_head else wt_ref[:, c0 - n_head:c0 - n_head + slab]
            slabs[c0] = _dot(h, w)

    def proj(*names):
        out = []
        for n in names:
            lo, hi = cols[n]
            parts = []
            for c0 in range(lo - lo % slab, hi, slab):
                while c0 not in slabs:
                    pump()
                s_lo, s_hi = max(lo, c0) - c0, min(hi, c0 + slab) - c0
                parts.append(slabs[c0][:, s_lo:s_hi])
            out.append(parts[0] if len(parts) == 1 else jnp.concatenate(parts, axis=1))
        return out

    def out_proj(y, idx):
        return _dot(y.astype(BF16), wo_ref[idx * g:(idx + 1) * g, :])

    c_val, c_gate = proj("c_val", "c_gate")
    cin = c_val * _sigmoid(c_gate)
    cbuf[c_head:c_head + tl, :] = cin
    conv_c = _conv_taps(ccw_ref, cbuf, zbuf, tl, c_head, pump) + ccb_ref[...]
    acc = x + out_proj(_silu(_layernorm(conv_c, lcg_ref[...], lcb_ref[...])), 2)

    a_x, a_b, a_c = proj("a_x", "a_b", "a_c")
    xa = a_c * a_x
    abuf[a_head:a_head + tl, :] = xa
    conv = caw_ref[SC_WIDTH - 1:SC_WIDTH, :] * xa
    for kk in range(SC_WIDTH - 1):
        sh = SC_WIDTH - 1 - kk
        conv = conv + caw_ref[kk:kk + 1, :] * abuf[a_head - sh:a_head - sh + tl, :]
    acc = acc + out_proj(a_b * conv, 0)

    d_u, d_v = proj("d_u", "d_v")
    du = _gelu_tanh(d_u)
    vd = _layernorm(_gelu_tanh(d_v), ldg_ref[...], ldb_ref[...])
    sv_ref[...] = vd[tl - SG_CHUNK:tl]
    vdb = vd.astype(BF16)
    row = lax.broadcasted_iota(jnp.int32, (SG_CHUNK, SG_CHUNK), 0)
    col = lax.broadcasted_iota(jnp.int32, (SG_CHUNK, SG_CHUNK), 1)
    ws = [jnp.where(row >= col, sgw_ref[hh], 0.0).astype(BF16) for hh in range(N_HEADS)]
    sv_rows = []
    for ci in range(tl // SG_CHUNK):
        sl = slice(ci * SG_CHUNK, (ci + 1) * SG_CHUNK)
        sv_rows.append(jnp.concatenate(
            [_dot(ws[hh], vdb[sl, hh * dv:(hh + 1) * dv]) for hh in range(N_HEADS)], axis=1) + sgb_ref[...])
    acc = acc + out_proj(du * jnp.concatenate(sv_rows, axis=0), 3)

    q_all, k_all, v_all, r_all = proj("q", "k", "v", "r")
    la = _gla_gate(_dot(h, wlr_ref[...]), a2_ref, ab_ref)
    la_hi, la_lo = _split_bf16(la)
    tri = tri_ref[...]
    b_all = (_dot(tri, la_hi) + _dot(tri, la_lo)) * math.log2(math.e)
    q_all = q_all * (float(qk // N_HEADS) ** -0.5)
    hm = _head_masks(qk)
    e2 = e2_ref[...]
    s_flat = s_ref[...]
    outs = []
    for ci in range(tl // GLA_CHUNK):
        sl = slice(ci * GLA_CHUNK, (ci + 1) * GLA_CHUNK)
        o_c, s_flat = _gla_chunk(q_all[sl], k_all[sl], v_all[sl], b_all[sl], la_hi[sl], la_lo[sl], s_flat, e2, hm)
        outs.append(o_c)
    s_ref[...] = s_flat
    acc = acc + out_proj(_gla_out(jnp.concatenate(outs, axis=0), gn_ref[...], r_all), 1)
    o_ref[...] = acc
    hn_ref[...] = _rmsnorm(acc, gf_ref[...]).astype(BF16)

    @pl.when(t == last)
    def _():
        na_ref[...] = abuf[a_head + tl - (SC_WIDTH - 1):a_head + tl, :]
        nc_ref[...] = cbuf[c_head + tl - (CF_WIDTH - 1):c_head + tl, :]

    abuf[0:a_head, :] = abuf[tl:tl + a_head, :]
    cbuf[0:c_head, :] = cbuf[tl:tl + c_head, :]


def _full(shape):
    nd = len(shape)
    return pl.BlockSpec(shape, lambda *_: (0,) * nd)


def _resident(a, l=None):
    if l is None:
        return pl.BlockSpec(a.shape, lambda *_: (0,) * a.ndim, pipeline_mode=pl.Buffered(1))
    return pl.BlockSpec((None,) + a.shape[1:], lambda *_: (l,) + (0,) * (a.ndim - 1), pipeline_mode=pl.Buffered(1))


def _front(x, gm, w_head, w_tail, w_lr, w_o, gf, l, batch, seq, wts, tl):
    d = x.shape[1]
    g = wts["gn"].shape[1]
    nt = seq // tl
    row_map = lambda b, t: (b * nt + t, 0)
    names = ("caw", "a2", "ab", "gn", "ccw", "ccb", "lcg", "lcb", "ldg", "ldb", "sgw", "sgb", "tri", "e2")
    consts = [wts[n] for n in names]
    return pl.pallas_call(
        _front_kernel,
        grid=(batch, nt),
        in_specs=[pl.BlockSpec((tl, d), row_map), _full(gm.shape),
                  _resident(w_head, l), _resident(w_tail, l), _resident(w_lr, l)]
                 + [_full(c.shape) for c in consts] + [_resident(w_o), _full(gf.shape)],
        out_specs=[
            pl.BlockSpec((tl, d), row_map),
            pl.BlockSpec((tl, d), row_map),
            pl.BlockSpec((None, SC_WIDTH - 1, g), lambda b, t: (b, 0, 0)),
            pl.BlockSpec((None, g // 2, g // N_HEADS), lambda b, t: (b, 0, 0)),
            pl.BlockSpec((None, CF_WIDTH - 1, g), lambda b, t: (b, 0, 0)),
            pl.BlockSpec((None, SG_CHUNK, g), lambda b, t: (b, 0, 0)),
        ],
        out_shape=[
            jax.ShapeDtypeStruct(x.shape, F32),
            jax.ShapeDtypeStruct(x.shape, BF16),
            jax.ShapeDtypeStruct((batch, SC_WIDTH - 1, g), F32),
            jax.ShapeDtypeStruct((batch, g // 2, g // N_HEADS), F32),
            jax.ShapeDtypeStruct((batch, CF_WIDTH - 1, g), F32),
            jax.ShapeDtypeStruct((batch, SG_CHUNK, g), F32),
        ],
        scratch_shapes=[pltpu.VMEM((8 + tl, g), F32), pltpu.VMEM((32 + tl, g), F32), pltpu.VMEM((8 + tl, g), F32)],
        compiler_params=pltpu.CompilerParams(
            dimension_semantics=("arbitrary", "arbitrary"), vmem_limit_bytes=VMEM_LIMIT),
    )(x, gm, w_head, w_tail, w_lr, *consts, w_o, gf)


def _mixer_sample_kernel(z_ref, zlr_ref, pa_ref, s_ref, pc_ref, caw_ref, a2_ref, ab_ref, gn_ref, ccw_ref, ccb_ref,
                         lcg_ref, lcb_ref, ldg_ref, ldb_ref, sgw0_ref, sgb0_ref,
                         y_ref, na_ref, so_ref, nc_ref, sv_ref, o_scr, *, layer):
    if layer is not None:
        for ll in range(pa_ref.shape[0]):
            if ll != layer:
                na_ref[ll] = pa_ref[ll]
                so_ref[ll] = s_ref[ll]
                nc_ref[ll] = pc_ref[ll]
        pa_ref, s_ref, pc_ref = pa_ref.at[layer], s_ref.at[layer], pc_ref.at[layer]
        na_ref, so_ref, nc_ref = na_ref.at[layer], so_ref.at[layer], nc_ref.at[layer]
    bt = z_ref.shape[0]
    g = gn_ref.shape[1]
    qk = g // 2
    dk = qk // N_HEADS
    dv = g // N_HEADS
    cols, _ = _cols(g)

    def zc(name):
        lo, hi = cols[name]
        return z_ref[:, lo:hi]

    xa = zc("a_c") * zc("a_x")
    conv = caw_ref[SC_WIDTH - 1:SC_WIDTH, :] * xa
    for kk in range(SC_WIDTH - 1):
        conv = conv + caw_ref[kk:kk + 1, :] * pa_ref[:, kk, :]
    y_ref[:, 0:g] = (zc("a_b") * conv).astype(BF16)
    for kk in range(1, SC_WIDTH - 1):
        na_ref[:, kk - 1, :] = pa_ref[:, kk, :]
    na_ref[:, SC_WIDTH - 2, :] = xa

    la = _gla_gate(zlr_ref[...], a2_ref, ab_ref)
    a_t = jnp.exp(la).T
    k_t = zc("k").T
    q_rows = zc("q") * (float(dk) ** -0.5)
    v_rows = zc("v")
    hm = _head_masks(qk)
    hm_rows = jnp.concatenate(hm + [jnp.zeros_like(hm[0])] * (8 - N_HEADS), axis=0)
    for b in range(bt):
        a_col = jnp.broadcast_to(a_t[:, b:b + 1], (qk, dv))
        k_col = jnp.broadcast_to(k_t[:, b:b + 1], (qk, dv))
        v_b = jnp.concatenate(
            [jnp.broadcast_to(v_rows[b:b + 1, h * dv:(h + 1) * dv], (dk, dv)) for h in range(N_HEADS)], axis=0)
        s_new = a_col * s_ref[b] + k_col * v_b
        so_ref[b] = s_new
        q_m = (q_rows[b:b + 1] * hm_rows).astype(BF16)
        o_b = _dot(q_m, s_new.astype(BF16))
        for h in range(N_HEADS):
            o_scr[b:b + 1, h * dv:(h + 1) * dv] = o_b[h:h + 1]
    y_ref[:, g:2 * g] = _gla_out(o_scr[...], gn_ref[...], zc("r")).astype(BF16)

    cin = zc("c_val") * _sigmoid(zc("c_gate"))
    acc = ccb_ref[...] + ccw_ref[CF_WIDTH - 1:CF_WIDTH, :] * cin
    for kk in range(CF_WIDTH - 1):
        acc = acc + ccw_ref[kk:kk + 1, :] * pc_ref[:, kk, :]
    y_ref[:, 2 * g:3 * g] = _silu(_layernorm(acc, lcg_ref[...], lcb_ref[...])).astype(BF16)
    nc_ref[:, 0:CF_WIDTH - 2, :] = pc_ref[:, 1:CF_WIDTH - 1, :]
    nc_ref[:, CF_WIDTH - 2, :] = cin

    du = _gelu_tanh(zc("d_u"))
    vd = _layernorm(_gelu_tanh(zc("d_v")), ldg_ref[...], ldb_ref[...])
    sv_ref[...] = vd
    y_ref[:, 3 * g:4 * g] = (du * (sgw0_ref[...] * vd + sgb0_ref[...])).astype(BF16)


def _mixer_sample(z, zlr, pa, s, pc, wts, l, bt, in_place):
    nb, g = z.shape[0], wts["gn"].shape[1]
    names = ("caw", "a2", "ab", "gn", "ccw", "ccb", "lcg", "lcb", "ldg", "ldb", "sgw0", "sgb0")
    consts = [wts[n] for n in names]
    rows = lambda i: (i, 0)
    if in_place:
        state_spec = lambda a: pl.BlockSpec((None, bt) + a.shape[2:], lambda i: (l, i, 0, 0))
    else:
        state_spec = lambda a: pl.BlockSpec((a.shape[0], bt) + a.shape[2:], lambda i: (0, i, 0, 0))
    return pl.pallas_call(
        functools.partial(_mixer_sample_kernel, layer=None if in_place else l),
        grid=(nb // bt,),
        in_specs=[pl.BlockSpec((bt, z.shape[1]), rows), pl.BlockSpec((bt, LANES), rows),
                  state_spec(pa), state_spec(s), state_spec(pc)] + [_full(c.shape) for c in consts],
        out_specs=[pl.BlockSpec((bt, 4 * g), rows), state_spec(pa), state_spec(s), state_spec(pc),
                   pl.BlockSpec((bt, g), rows)],
        out_shape=[
            jax.ShapeDtypeStruct((nb, 4 * g), BF16),
            jax.ShapeDtypeStruct(pa.shape, F32),
            jax.ShapeDtypeStruct(s.shape, F32),
            jax.ShapeDtypeStruct(pc.shape, F32),
            jax.ShapeDtypeStruct((nb, g), F32),
        ],
        input_output_aliases={2: 1, 3: 2, 4: 3} if in_place else {},
        scratch_shapes=[pltpu.VMEM((bt, g), F32)],
        compiler_params=pltpu.CompilerParams(
            dimension_semantics=("arbitrary",), vmem_limit_bytes=VMEM_LIMIT),
    )(z, zlr, pa, s, pc, *consts)


def _largest_tile(m, cap, mult):
    t = min(m, cap)
    while m % t or t % mult:
        t -= mult
    return t


def _layer_weights(l, g, conv_a_w, gla_a2, gla_a_bias, gla_norm, conv_c_w, conv_c_b,
                   ln_c_g, ln_c_b, ln_d_g, ln_d_b, sg_w, sg_b):
    dv = g // N_HEADS
    row = lambda v: v.reshape(1, -1)
    return {
        "caw": conv_a_w[l],
        "a2": jnp.pad(gla_a2[l], ((0, LANES - GLA_LOWRANK), (0, 0))).astype(BF16),
        "ab": row(gla_a_bias[l]), "gn": row(gla_norm[l]),
        "ccw": conv_c_w[l], "ccb": row(conv_c_b[l]),
        "lcg": row(ln_c_g[l]), "lcb": row(ln_c_b[l]), "ldg": row(ln_d_g[l]), "ldb": row(ln_d_b[l]),
        "sgw": sg_w[l],
        "sgb": jnp.repeat(sg_b[l].T, dv, axis=1),
        "sgw0": row(jnp.repeat(sg_w[l][:, 0, 0], dv)),
        "sgb0": row(jnp.repeat(sg_b[l][:, 0], dv)),
    }


def _gla_constants(g, tl):
    qk, dv = g // 2, g // N_HEADS
    r = jnp.arange(tl)
    tri = ((r[:, None] >= r[None, :]) & (r[:, None] // GLA_CHUNK == r[None, :] // GLA_CHUNK)).astype(BF16)
    e2 = (jnp.arange(qk)[:, None] // (qk // N_HEADS) == jnp.arange(g)[None, :] // dv).astype(BF16)
    return tri, e2


def kernel(x_prompt, x_sample, state_conv_a, state_gla, state_conv_c, norm_mix, w_in, conv_a_w, gla_a2, gla_a_bias,
           gla_norm, conv_c_w, conv_c_b, ln_c_g, ln_c_b, ln_d_g, ln_d_b, sg_w, sg_b, w_o, norm_ffn, w_gate, w_up,
           w_down, norm_final):
    batch, seq, d = x_prompt.shape
    nb, dec_seq, _ = x_sample.shape
    depth = w_in.shape[0]
    g = d // 4
    qk, dv = g // 2, g // N_HEADS
    assert dec_seq == 1 and seq % SG_CHUNK == 0 and g % (N_HEADS * LANES) == 0
    dff = w_gate.shape[2]
    n_main = w_in.shape[2] - GLA_LOWRANK
    tl = _largest_tile(seq, FRONT_ROWS, SG_CHUNK)
    tri, e2 = _gla_constants(g, tl)

    lr0 = 3 * g + 2 * qk + 2 * g
    w_head = w_in[:, :, :lr0].astype(BF16)
    w_tail = w_in[:, :, lr0 + GLA_LOWRANK:].astype(BF16)
    w_lr = jnp.pad(w_in[:, :, lr0:lr0 + GLA_LOWRANK], ((0, 0), (0, 0), (0, LANES - GLA_LOWRANK))).astype(BF16)

    mp = batch * seq
    tm_p = _largest_tile(mp, 1024, 16)
    tm_s = nb
    tn_in = _largest_tile(math.gcd(n_main, lr0), 1024, 256)
    tf_cast = _largest_tile(dff, 256, 256)
    tf = _largest_tile(dff, 512, 256)

    xp = x_prompt.reshape(mp, d)
    xs = x_sample.reshape(nb, d)
    st_a, st_c = state_conv_a, state_conv_c
    st_s = state_gla.reshape(depth, nb, qk, dv)
    outs_p = {k: [] for k in ("a", "s", "c", "v")}
    sv_s = []
    for l in range(depth):
        wts = _layer_weights(l, g, conv_a_w, gla_a2, gla_a_bias, gla_norm, conv_c_w, conv_c_b,
                             ln_c_g, ln_c_b, ln_d_g, ln_d_b, sg_w, sg_b)
        wts["tri"], wts["e2"] = tri, e2
        gm, gf = norm_mix[l].reshape(1, d), norm_ffn[l].reshape(1, d)
        gfin = norm_final.reshape(1, d)
        final = l == depth - 1

        z, zlr = _norm_proj(xs, gm, w_head, w_tail, w_lr, l, tm_s, tn_in)
        y, st_a, st_s, st_c, sv = _mixer_sample(z, zlr, st_a, st_s, st_c, wts, l, 32 if l else 16, l > 0)
        xs, hs, w_ob = _proj_res(y, w_o, xs, gf, l, tn_in)

        xp, hn, na, s_new, nc, svp = _front(xp, gm, w_head, w_tail, w_lr, w_ob, gf, l, batch, seq, wts, tl)
        xp, wg_b, wu_b, wd_b = _ffn(hn, xp, w_gate, w_up, w_down, gfin, l, tm_p, tf_cast, 0, 1, final)
        if mp > tm_p:
            xp, = _ffn(hn, xp, wg_b, wu_b, wd_b, gfin, None, tm_p, tf, 1, mp // tm_p - 1, final)
        outs_p["a"].append(na)
        outs_p["s"].append(s_new.reshape(batch, N_HEADS, qk // N_HEADS, dv))
        outs_p["c"].append(nc)
        outs_p["v"].append(svp)

        xs, = _ffn(hs, xs, wg_b, wu_b, wd_b, gfin, None, tm_s, tf, 0, 1, final)
        sv_s.append(sv.reshape(nb, 1, g))

    st = jnp.stack
    return (xp.reshape(batch, seq, d), xs.reshape(nb, 1, d),
            st(outs_p["a"]), st_a, st(outs_p["s"]), st_s.reshape(depth, nb, N_HEADS, qk // N_HEADS, dv),
            st(outs_p["c"]), st_c, st(outs_p["v"]), st(sv_s))
```

```python
import functools
import math

import jax
import jax.numpy as jnp
from jax import lax
from jax.experimental import pallas as pl
from jax.experimental.pallas import tpu as pltpu

F32 = jnp.float32
BF16 = jnp.bfloat16
EPS = 1e-6

N_HEADS = 4
GLA_LOWRANK = 16
GLA_TAU = 16.0
SC_WIDTH = 3
CF_WIDTH = 31
SG_CHUNK = 128
GLA_CHUNK = 64
GLA_SUB = 16
FRONT_ROWS = 256
PROJ_SLAB = 256
LANES = 128
VMEM_LIMIT = 60 * 1024 * 1024


def _dot(a, b):
    return jnp.dot(a, b, preferred_element_type=F32)


def _dot_nt(a, b):
    return lax.dot_general(a, b, (((1,), (1,)), ((), ())), preferred_element_type=F32)


def _dot_tn(a, b):
    return lax.dot_general(a, b, (((0,), (0,)), ((), ())), preferred_element_type=F32)


def _split_bf16(x):
    hi = x.astype(BF16)
    lo = (x - hi.astype(F32)).astype(BF16)
    return hi, lo


def _rmsnorm(x, g):
    return x * lax.rsqrt(jnp.mean(x * x, axis=-1, keepdims=True) + EPS) * g


def _layernorm(x, g, b):
    mu = jnp.mean(x, axis=-1, keepdims=True)
    xc = x - mu
    return xc * lax.rsqrt(jnp.mean(xc * xc, axis=-1, keepdims=True) + EPS) * g + b


def _sigmoid(x):
    return 1.0 / (1.0 + jnp.exp(-x))


def _silu(x):
    return x * _sigmoid(x)


def _gelu_tanh(x):
    c = math.sqrt(2.0 / math.pi)
    return 0.5 * x * (1.0 + jnp.tanh(c * (x + 0.044715 * (x * x * x))))


def _log_sigmoid(x):
    return jnp.minimum(x, 0.0) - jnp.log(1.0 + jnp.exp(-jnp.abs(x)))


def _norm_proj_kernel(x_ref, g_ref, wh_ref, wt_ref, wlr_ref, z_ref, zlr_ref, h_ref, *, n_head):
    j = pl.program_id(1)

    @pl.when(j == 0)
    def _():
        h = _rmsnorm(x_ref[...], g_ref[...]).astype(BF16)
        h_ref[...] = h
        zlr_ref[...] = _dot(h, wlr_ref[...])

    @pl.when(j < n_head)
    def _():
        z_ref[...] = _dot(h_ref[...], wh_ref[...])

    @pl.when(j >= n_head)
    def _():
        z_ref[...] = _dot(h_ref[...], wt_ref[...])


def _norm_proj(x, g, w_head, w_tail, w_lr, l, tm, tn):
    m, d = x.shape
    n_head, n_tail = w_head.shape[2] // tn, w_tail.shape[2] // tn
    return pl.pallas_call(
        functools.partial(_norm_proj_kernel, n_head=n_head),
        grid=(m // tm, n_head + n_tail),
        in_specs=[
            pl.BlockSpec((tm, d), lambda i, j: (i, 0)),
            pl.BlockSpec((1, d), lambda i, j: (0, 0)),
            pl.BlockSpec((None, d, tn), lambda i, j: (l, 0, jnp.minimum(j, n_head - 1))),
            pl.BlockSpec((None, d, tn), lambda i, j: (l, 0, jnp.maximum(j - n_head, 0))),
            pl.BlockSpec((None, d, LANES), lambda i, j: (l, 0, 0)),
        ],
        out_specs=[
            pl.BlockSpec((tm, tn), lambda i, j: (i, j)),
            pl.BlockSpec((tm, LANES), lambda i, j: (i, 0)),
        ],
        out_shape=[jax.ShapeDtypeStruct((m, (n_head + n_tail) * tn), F32), jax.ShapeDtypeStruct((m, LANES), F32)],
        scratch_shapes=[pltpu.VMEM((tm, d), BF16)],
        compiler_params=pltpu.CompilerParams(
            dimension_semantics=("arbitrary", "arbitrary"), vmem_limit_bytes=VMEM_LIMIT),
    )(x, g, w_head, w_tail, w_lr)


def _proj_res_kernel(y_ref, w_ref, x_ref, g_ref, o_ref, h_ref, wb_ref):
    k = pl.program_id(0)

    @pl.when(k == 0)
    def _():
        o_ref[...] = x_ref[...]

    w = w_ref[...].astype(BF16)
    wb_ref[...] = w
    o_ref[...] += _dot(y_ref[...], w)

    @pl.when(k == pl.num_programs(0) - 1)
    def _():
        h_ref[...] = _rmsnorm(o_ref[...], g_ref[...]).astype(BF16)


def _proj_res(y, w, x, g, l, tk):
    m, k = y.shape
    n = w.shape[2]
    return pl.pallas_call(
        _proj_res_kernel,
        grid=(k // tk,),
        in_specs=[
            pl.BlockSpec((m, tk), lambda i: (0, i)),
            pl.BlockSpec((None, tk, n), lambda i: (l, i, 0)),
            pl.BlockSpec((m, n), lambda i: (0, 0)),
            pl.BlockSpec((1, n), lambda i: (0, 0)),
        ],
        out_specs=[pl.BlockSpec((m, n), lambda i: (0, 0)), pl.BlockSpec((m, n), lambda i: (0, 0)),
                   pl.BlockSpec((tk, n), lambda i: (i, 0))],
        out_shape=[jax.ShapeDtypeStruct((m, n), F32), jax.ShapeDtypeStruct((m, n), BF16),
                   jax.ShapeDtypeStruct((k, n), BF16)],
        compiler_params=pltpu.CompilerParams(
            dimension_semantics=("arbitrary",), vmem_limit_bytes=VMEM_LIMIT),
    )(y, w, x, g)


def _ffn_kernel(*refs, n_groups, final_norm):
    hx = [(refs[2 * k], refs[2 * k + 1]) for k in range(n_groups)]
    wg_ref, wu_ref, wd_ref, g_ref = refs[2 * n_groups:2 * n_groups + 4]
    o_refs = refs[2 * n_groups + 4:3 * n_groups + 4]
    cast_refs = refs[3 * n_groups + 4:]
    j = pl.program_id(1)

    @pl.when(j == 0)
    def _():
        for (_, x_ref), o_ref in zip(hx, o_refs):
            o_ref[...] = x_ref[...]

    wg, wu, wd = wg_ref[...].astype(BF16), wu_ref[...].astype(BF16), wd_ref[...].astype(BF16)
    for ref, w in zip(cast_refs, (wg, wu, wd)):
        ref[...] = w
    for (h_ref, _), o_ref in zip(hx, o_refs):
        h = h_ref[...]
        o_ref[...] += _dot((_silu(_dot(h, wg)) * _dot(h, wu)).astype(BF16), wd)

    if final_norm:
        @pl.when(j == pl.num_programs(1) - 1)
        def _():
            for o_ref in o_refs:
                o_ref[...] = _rmsnorm(o_ref[...], g_ref[...])


def _ffn(h, x, wg, wu, wd, g, l, tm, tf, tile0, n_tiles, final_norm, rider=None):
    m, d = x.shape
    f = wg.shape[-1]
    rows = lambda i, j: (tile0 + i, 0)
    row_mode = pl.Buffered(1) if n_tiles == 1 else None
    row_specs = [pl.BlockSpec((tm, d), rows, pipeline_mode=row_mode)] * 2
    out_specs = [pl.BlockSpec((tm, d), rows)]
    out_shape = [jax.ShapeDtypeStruct((m, d), F32)]
    operands = [h, x]
    aliases = {1: 0}
    if rider is not None:
        assert n_tiles == 1
        whole = pl.BlockSpec(rider[1].shape, lambda i, j: (0, 0))
        row_specs += [whole, whole]
        out_specs.append(whole)
        out_shape.append(jax.ShapeDtypeStruct(rider[1].shape, F32))
        operands += list(rider)
        aliases[3] = 1
    if l is None:
        w_specs = [pl.BlockSpec((d, tf), lambda i, j: (0, j)), pl.BlockSpec((d, tf), lambda i, j: (0, j)),
                   pl.BlockSpec((tf, d), lambda i, j: (j, 0))]
    else:
        w_specs = [pl.BlockSpec((None, d, tf), lambda i, j: (l, 0, j)),
                   pl.BlockSpec((None, d, tf), lambda i, j: (l, 0, j)),
                   pl.BlockSpec((None, tf, d), lambda i, j: (l, j, 0))]
        out_specs += [pl.BlockSpec((d, tf), lambda i, j: (0, j)), pl.BlockSpec((d, tf), lambda i, j: (0, j)),
                      pl.BlockSpec((tf, d), lambda i, j: (j, 0))]
        out_shape += [jax.ShapeDtypeStruct((d, f), BF16), jax.ShapeDtypeStruct((d, f), BF16),
                      jax.ShapeDtypeStruct((f, d), BF16)]
    return pl.pallas_call(
        functools.partial(_ffn_kernel, n_groups=len(operands) // 2, final_norm=final_norm),
        grid=(n_tiles, f // tf),
        in_specs=row_specs + w_specs + [pl.BlockSpec((1, d), lambda i, j: (0, 0))],
        out_specs=out_specs,
        out_shape=out_shape,
        input_output_aliases=aliases,
        compiler_params=pltpu.CompilerParams(
            dimension_semantics=("arbitrary", "arbitrary"), vmem_limit_bytes=VMEM_LIMIT),
    )(*operands, wg, wu, wd, g)


def _gla_gate(zlr, a2_ref, ab_ref):
    xg = _dot(zlr.astype(BF16), a2_ref[...]) + ab_ref[...]
    return _log_sigmoid(xg) * (1.0 / GLA_TAU)


def _head_masks(qk):
    lane_head = lax.broadcasted_iota(jnp.int32, (1, qk), 1) // (qk // N_HEADS)
    return [(lane_head == h).astype(F32) for h in range(N_HEADS)]


def _gla_chunk(qc, kc, vc, bc, la_hi, la_lo, s_flat, e2, hm):
    c, qk = qc.shape
    dv = vc.shape[1] // N_HEADS
    dk = qk // N_HEADS
    nb = c // GLA_SUB
    vcb = vc.astype(BF16)
    sub_row = lax.broadcasted_iota(jnp.int32, (GLA_SUB, 1), 0)
    col_j = lax.broadcasted_iota(jnp.int32, (1, c), 1)

    o_diag = []
    for blk in range(nb):
        r0 = blk * GLA_SUB
        q_b, k_b, b_b = qc[r0:r0 + GLA_SUB], kc[r0:r0 + GLA_SUB], bc[r0:r0 + GLA_SUB]
        v_b = vc[r0:r0 + GLA_SUB]
        ts = []
        for j in range(GLA_SUB):
            dec = jnp.exp2(jnp.where(sub_row >= j, b_b - b_b[j:j + 1], -jnp.inf))
            ts.append(q_b * k_b[j:j + 1] * dec)
        r = _dot(jnp.concatenate(ts, axis=0).astype(BF16), e2)
        acc = r[0:GLA_SUB] * v_b[0:1]
        for j in range(1, GLA_SUB):
            acc = acc + r[j * GLA_SUB:(j + 1) * GLA_SUB] * v_b[j:j + 1]
        o_diag.append(acc)
    o = jnp.concatenate(o_diag, axis=0)

    bref = [None] + [bc[blk * GLA_SUB - 1:blk * GLA_SUB] for blk in range(1, nb)]
    bref_rows = jnp.concatenate(
        [jnp.zeros((GLA_SUB, qk), F32)] + [jnp.broadcast_to(bref[blk], (GLA_SUB, qk)) for blk in range(1, nb)],
        axis=0)
    qt = qc * jnp.exp2(bc - bref_rows)
    att_blk = [None]
    for blk in range(1, nb):
        kt = (kc * jnp.exp2(jnp.minimum(bref[blk] - bc, 0.0))).astype(BF16)
        q_b = qt[blk * GLA_SUB:(blk + 1) * GLA_SUB]
        qm = jnp.concatenate([q_b * hm[h] for h in range(N_HEADS)], axis=0).astype(BF16)
        a = _dot_nt(qm, kt)
        att_blk.append(jnp.where(col_j < blk * GLA_SUB, a, 0.0))
    o_off = []
    for h in range(N_HEADS):
        att = jnp.concatenate(
            [jnp.zeros((GLA_SUB, c), F32)] + [att_blk[blk][h * GLA_SUB:(h + 1) * GLA_SUB] for blk in range(1, nb)],
            axis=0)
        o_off.append(_dot(att.astype(BF16), vcb[:, h * dv:(h + 1) * dv]))
    o = o + jnp.concatenate(o_off, axis=1)

    qe = qc * jnp.exp2(bc)
    qe_m = jnp.concatenate([qe * hm[h] for h in range(N_HEADS)], axis=0).astype(BF16)
    o_int = _dot(qe_m, s_flat.astype(BF16))
    o = o + jnp.concatenate([o_int[h * c:(h + 1) * c] for h in range(N_HEADS)], axis=1)

    kk = (kc * jnp.exp2(bc[c - 1:c] - bc)).astype(BF16)
    kv = _dot_tn(kk, vcb)
    kv_d = jnp.concatenate([kv[h * dk:(h + 1) * dk, h * dv:(h + 1) * dv] for h in range(N_HEADS)], axis=0)
    ones = jnp.ones((c, dv), BF16)
    decay = jnp.exp(_dot_tn(la_hi, ones) + _dot_tn(la_lo, ones))
    return o, decay * s_flat + kv_d


def _gla_out(o, gn, r):
    dv = o.shape[1] // N_HEADS
    parts = []
    for h in range(N_HEADS):
        oh = o[:, h * dv:(h + 1) * dv]
        parts.append(oh * lax.rsqrt(jnp.mean(oh * oh, axis=-1, keepdims=True) + EPS))
    return jnp.concatenate(parts, axis=1) * gn * _silu(r)


def _cols(g):
    qk = g // 2
    c = {}
    off = 0
    for name, width in (("a_x", g), ("a_b", g), ("a_c", g), ("q", qk), ("k", qk), ("v", g), ("r", g),
                        ("c_val", g), ("c_gate", g), ("d_u", g), ("d_v", g)):
        c[name] = (off, off + width)
        off += width
    return c, off


def _conv_taps(ccw_ref, cbuf, zbuf, tl, head, between):
    sub = 8
    base = head - (CF_WIDTH - 1)
    cols = []
    for c0 in range(0, cbuf.shape[1], LANES):
        lanes = slice(c0, c0 + LANES)
        y = None
        for r in range(sub):
            taps = [k for k in range(CF_WIDTH) if (base + k) % sub == r]
            rows = tl if r == 0 else tl + sub
            part = None
            for k in taps:
                off = base + k - r
                term = ccw_ref[k:k + 1, lanes] * cbuf[off:off + rows, lanes]
                part = term if part is None else part + term
            if r == 0:
                shifted = part
            else:
                zbuf[0:rows, lanes] = part
                shifted = zbuf[r:r + tl, lanes]
            y = shifted if y is None else y + shifted
            between()
        cols.append(y)
    return jnp.concatenate(cols, axis=1)


def _front_kernel(x_ref, gm_ref, wh_ref, wt_ref, wlr_ref, caw_ref, a2_ref, ab_ref, gn_ref, ccw_ref, ccb_ref,
                  lcg_ref, lcb_ref, ldg_ref, ldb_ref, sgw_ref, sgb_ref, tri_ref, e2_ref, wo_ref, gf_ref,
                  o_ref, hn_ref, na_ref, s_ref, nc_ref, sv_ref, abuf, cbuf, zbuf):
    t = pl.program_id(1)
    last = pl.num_programs(1) - 1
    tl = x_ref.shape[0]
    g = gn_ref.shape[1]
    qk = g // 2
    dv = g // N_HEADS
    cols, _ = _cols(g)
    a_head, c_head = 8, 32

    @pl.when(t == 0)
    def _():
        abuf[0:a_head, :] = jnp.zeros((a_head, g), F32)
        cbuf[0:c_head, :] = jnp.zeros((c_head, g), F32)
        s_ref[...] = jnp.zeros(s_ref.shape, F32)

    x = x_ref[...]
    h = _rmsnorm(x, gm_ref[...]).astype(BF16)

    slab = PROJ_SLAB
    order = [c0 for n in ("c_val", "c_gate", "a_x", "a_b", "a_c", "d_u", "d_v", "q", "k", "v", "r")
             for c0 in range(cols[n][0], cols[n][1], slab) if c0 % slab == 0]
    pending = list(dict.fromkeys(order))
    slabs = {}

    n_head = wh_ref.shape[1]

    def pump():
        if pending:
            c0 = pending.pop(0)
            w = wh_ref[:, c0:c0 + slab] if c0 < n_head else wt_ref[:, c0 - n_head:c0 - n_head + slab]
            slabs[c0] = _dot(h, w)

    def proj(*names):
        out = []
        for n in names:
            lo, hi = cols[n]
            parts = []
            for c0 in range(lo - lo % slab, hi, slab):
                while c0 not in slabs:
                    pump()
                s_lo, s_hi = max(lo, c0) - c0, min(hi, c0 + slab) - c0
                parts.append(slabs[c0][:, s_lo:s_hi])
            out.append(parts[0] if len(parts) == 1 else jnp.concatenate(parts, axis=1))
        return out

    def out_proj(y, idx):
        return _dot(y.astype(BF16), wo_ref[idx * g:(idx + 1) * g, :])

    c_val, c_gate = proj("c_val", "c_gate")
    cin = c_val * _sigmoid(c_gate)
    cbuf[c_head:c_head + tl, :] = cin
    conv_c = _conv_taps(ccw_ref, cbuf, zbuf, tl, c_head, pump) + ccb_ref[...]
    acc = x + out_proj(_silu(_layernorm(conv_c, lcg_ref[...], lcb_ref[...])), 2)

    a_x, a_b, a_c = proj("a_x", "a_b", "a_c")
    xa = a_c * a_x
    abuf[a_head:a_head + tl, :] = xa
    conv = caw_ref[SC_WIDTH - 1:SC_WIDTH, :] * xa
    for kk in range(SC_WIDTH - 1):
        sh = SC_WIDTH - 1 - kk
        conv = conv + caw_ref[kk:kk + 1, :] * abuf[a_head - sh:a_head - sh + tl, :]
    acc = acc + out_proj(a_b * conv, 0)

    d_u, d_v = proj("d_u", "d_v")
    du = _gelu_tanh(d_u)
    vd = _layernorm(_gelu_tanh(d_v), ldg_ref[...], ldb_ref[...])
    sv_ref[...] = vd[tl - SG_CHUNK:tl]
    vdb = vd.astype(BF16)
    row = lax.broadcasted_iota(jnp.int32, (SG_CHUNK, SG_CHUNK), 0)
    col = lax.broadcasted_iota(jnp.int32, (SG_CHUNK, SG_CHUNK), 1)
    ws = [jnp.where(row >= col, sgw_ref[hh], 0.0).astype(BF16) for hh in range(N_HEADS)]
    sv_rows = []
    for ci in range(tl // SG_CHUNK):
        sl = slice(ci * SG_CHUNK, (ci + 1) * SG_CHUNK)
        sv_rows.append(jnp.concatenate(
            [_dot(ws[hh], vdb[sl, hh * dv:(hh + 1) * dv]) for hh in range(N_HEADS)], axis=1) + sgb_ref[...])
    acc = acc + out_proj(du * jnp.concatenate(sv_rows, axis=0), 3)

    q_all, k_all, v_all, r_all = proj("q", "k", "v", "r")
    la = _gla_gate(_dot(h, wlr_ref[...]), a2_ref, ab_ref)
    la_hi, la_lo = _split_bf16(la)
    tri = tri_ref[...]
    b_all = (_dot(tri, la_hi) + _dot(tri, la_lo)) * math.log2(math.e)
    q_all = q_all * (float(qk // N_HEADS) ** -0.5)
    hm = _head_masks(qk)
    e2 = e2_ref[...]
    s_flat = s_ref[...]
    outs = []
    for ci in range(tl // GLA_CHUNK):
        sl = slice(ci * GLA_CHUNK, (ci + 1) * GLA_CHUNK)
        o_c, s_flat = _gla_chunk(q_all[sl], k_all[sl], v_all[sl], b_all[sl], la_hi[sl], la_lo[sl], s_flat, e2, hm)
        outs.append(o_c)
    s_ref[...] = s_flat
    acc = acc + out_proj(_gla_out(jnp.concatenate(outs, axis=0), gn_ref[...], r_all), 1)
    o_ref[...] = acc
    hn_ref[...] = _rmsnorm(acc, gf_ref[...]).astype(BF16)

    @pl.when(t == last)
    def _():
        na_ref[...] = abuf[a_head + tl - (SC_WIDTH - 1):a_head + tl, :]
        nc_ref[...] = cbuf[c_head + tl - (CF_WIDTH - 1):c_head + tl, :]

    abuf[0:a_head, :] = abuf[tl:tl + a_head, :]
    cbuf[0:c_head, :] = cbuf[tl:tl + c_head, :]


def _full(shape):
    nd = len(shape)
    return pl.BlockSpec(shape, lambda *_: (0,) * nd)


def _resident(a, l=None):
    if l is None:
        return pl.BlockSpec(a.shape, lambda *_: (0,) * a.ndim, pipeline_mode=pl.Buffered(1))
    return pl.BlockSpec((None,) + a.shape[1:], lambda *_: (l,) + (0,) * (a.ndim - 1), pipeline_mode=pl.Buffered(1))


def _front(x, gm, w_head, w_tail, w_lr, w_o, gf, l, batch, seq, wts, tl):
    d = x.shape[1]
    g = wts["gn"].shape[1]
    nt = seq // tl
    row_map = lambda b, t: (b * nt + t, 0)
    names = ("caw", "a2", "ab", "gn", "ccw", "ccb", "lcg", "lcb", "ldg", "ldb", "sgw", "sgb", "tri", "e2")
    consts = [wts[n] for n in names]
    return pl.pallas_call(
        _front_kernel,
        grid=(batch, nt),
        in_specs=[pl.BlockSpec((tl, d), row_map), _full(gm.shape),
                  _resident(w_head, l), _resident(w_tail, l), _resident(w_lr, l)]
                 + [_full(c.shape) for c in consts] + [_resident(w_o), _full(gf.shape)],
        out_specs=[
            pl.BlockSpec((tl, d), row_map),
            pl.BlockSpec((tl, d), row_map),
            pl.BlockSpec((None, SC_WIDTH - 1, g), lambda b, t: (b, 0, 0)),
            pl.BlockSpec((None, g // 2, g // N_HEADS), lambda b, t: (b, 0, 0)),
            pl.BlockSpec((None, CF_WIDTH - 1, g), lambda b, t: (b, 0, 0)),
            pl.BlockSpec((None, SG_CHUNK, g), lambda b, t: (b, 0, 0)),
        ],
        out_shape=[
            jax.ShapeDtypeStruct(x.shape, F32),
            jax.ShapeDtypeStruct(x.shape, BF16),
            jax.ShapeDtypeStruct((batch, SC_WIDTH - 1, g), F32),
            jax.ShapeDtypeStruct((batch, g // 2, g // N_HEADS), F32),
            jax.ShapeDtypeStruct((batch, CF_WIDTH - 1, g), F32),
            jax.ShapeDtypeStruct((batch, SG_CHUNK, g), F32),
        ],
        scratch_shapes=[pltpu.VMEM((8 + tl, g), F32), pltpu.VMEM((32 + tl, g), F32), pltpu.VMEM((8 + tl, g), F32)],
        compiler_params=pltpu.CompilerParams(
            dimension_semantics=("arbitrary", "arbitrary"), vmem_limit_bytes=VMEM_LIMIT),
    )(x, gm, w_head, w_tail, w_lr, *consts, w_o, gf)


def _mixer_sample_kernel(z_ref, zlr_ref, pa_ref, s_ref, pc_ref, caw_ref, a2_ref, ab_ref, gn_ref, ccw_ref, ccb_ref,
                         lcg_ref, lcb_ref, ldg_ref, ldb_ref, sgw0_ref, sgb0_ref,
                         y_ref, na_ref, so_ref, nc_ref, sv_ref, o_scr, *, layer):
    if layer is not None:
        for ll in range(pa_ref.shape[0]):
            if ll != layer:
                na_ref[ll] = pa_ref[ll]
                so_ref[ll] = s_ref[ll]
                nc_ref[ll] = pc_ref[ll]
        pa_ref, s_ref, pc_ref = pa_ref.at[layer], s_ref.at[layer], pc_ref.at[layer]
        na_ref, so_ref, nc_ref = na_ref.at[layer], so_ref.at[layer], nc_ref.at[layer]
    bt = z_ref.shape[0]
    g = gn_ref.shape[1]
    qk = g // 2
    dk = qk // N_HEADS
    dv = g // N_HEADS
    cols, _ = _cols(g)

    def zc(name):
        lo, hi = cols[name]
        return z_ref[:, lo:hi]

    xa = zc("a_c") * zc("a_x")
    conv = caw_ref[SC_WIDTH - 1:SC_WIDTH, :] * xa
    for kk in range(SC_WIDTH - 1):
        conv = conv + caw_ref[kk:kk + 1, :] * pa_ref[:, kk, :]
    y_ref[:, 0:g] = (zc("a_b") * conv).astype(BF16)
    for kk in range(1, SC_WIDTH - 1):
        na_ref[:, kk - 1, :] = pa_ref[:, kk, :]
    na_ref[:, SC_WIDTH - 2, :] = xa

    la = _gla_gate(zlr_ref[...], a2_ref, ab_ref)
    a_t = jnp.exp(la).T
    k_t = zc("k").T
    q_rows = zc("q") * (float(dk) ** -0.5)
    v_rows = zc("v")
    hm = _head_masks(qk)
    hm_rows = jnp.concatenate(hm + [jnp.zeros_like(hm[0])] * (8 - N_HEADS), axis=0)
    for b in range(bt):
        a_col = jnp.broadcast_to(a_t[:, b:b + 1], (qk, dv))
        k_col = jnp.broadcast_to(k_t[:, b:b + 1], (qk, dv))
        v_b = jnp.concatenate(
            [jnp.broadcast_to(v_rows[b:b + 1, h * dv:(h + 1) * dv], (dk, dv)) for h in range(N_HEADS)], axis=0)
        s_new = a_col * s_ref[b] + k_col * v_b
        so_ref[b] = s_new
        q_m = (q_rows[b:b + 1] * hm_rows).astype(BF16)
        o_b = _dot(q_m, s_new.astype(BF16))
        for h in range(N_HEADS):
            o_scr[b:b + 1, h * dv:(h + 1) * dv] = o_b[h:h + 1]
    y_ref[:, g:2 * g] = _gla_out(o_scr[...], gn_ref[...], zc("r")).astype(BF16)

    cin = zc("c_val") * _sigmoid(zc("c_gate"))
    acc = ccb_ref[...] + ccw_ref[CF_WIDTH - 1:CF_WIDTH, :] * cin
    for kk in range(CF_WIDTH - 1):
        acc = acc + ccw_ref[kk:kk + 1, :] * pc_ref[:, kk, :]
    y_ref[:, 2 * g:3 * g] = _silu(_layernorm(acc, lcg_ref[...], lcb_ref[...])).astype(BF16)
    nc_ref[:, 0:CF_WIDTH - 2, :] = pc_ref[:, 1:CF_WIDTH - 1, :]
    nc_ref[:, CF_WIDTH - 2, :] = cin

    du = _gelu_tanh(zc("d_u"))
    vd = _layernorm(_gelu_tanh(zc("d_v")), ldg_ref[...], ldb_ref[...])
    sv_ref[...] = vd
    y_ref[:, 3 * g:4 * g] = (du * (sgw0_ref[...] * vd + sgb0_ref[...])).astype(BF16)


def _mixer_sample(z, zlr, pa, s, pc, wts, l, bt, in_place):
    nb, g = z.shape[0], wts["gn"].shape[1]
    names = ("caw", "a2", "ab", "gn", "ccw", "ccb", "lcg", "lcb", "ldg", "ldb", "sgw0", "sgb0")
    consts = [wts[n] for n in names]
    rows = lambda i: (i, 0)
    if in_place:
        state_spec = lambda a: pl.BlockSpec((None, bt) + a.shape[2:], lambda i: (l, i, 0, 0))
    else:
        state_spec = lambda a: pl.BlockSpec((a.shape[0], bt) + a.shape[2:], lambda i: (0, i, 0, 0))
    return pl.pallas_call(
        functools.partial(_mixer_sample_kernel, layer=None if in_place else l),
        grid=(nb // bt,),
        in_specs=[pl.BlockSpec((bt, z.shape[1]), rows), pl.BlockSpec((bt, LANES), rows),
                  state_spec(pa), state_spec(s), state_spec(pc)] + [_full(c.shape) for c in consts],
        out_specs=[pl.BlockSpec((bt, 4 * g), rows), state_spec(pa), state_spec(s), state_spec(pc),
                   pl.BlockSpec((bt, g), rows)],
        out_shape=[
            jax.ShapeDtypeStruct((nb, 4 * g), BF16),
            jax.ShapeDtypeStruct(pa.shape, F32),
            jax.ShapeDtypeStruct(s.shape, F32),
            jax.ShapeDtypeStruct(pc.shape, F32),
            jax.ShapeDtypeStruct((nb, g), F32),
        ],
        input_output_aliases={2: 1, 3: 2, 4: 3} if in_place else {},
        scratch_shapes=[pltpu.VMEM((bt, g), F32)],
        compiler_params=pltpu.CompilerParams(
            dimension_semantics=("arbitrary",), vmem_limit_bytes=VMEM_LIMIT),
    )(z, zlr, pa, s, pc, *consts)


def _largest_tile(m, cap, mult):
    t = min(m, cap)
    while m % t or t % mult:
        t -= mult
    return t


def _layer_weights(l, g, conv_a_w, gla_a2, gla_a_bias, gla_norm, conv_c_w, conv_c_b,
                   ln_c_g, ln_c_b, ln_d_g, ln_d_b, sg_w, sg_b):
    dv = g // N_HEADS
    row = lambda v: v.reshape(1, -1)
    return {
        "caw": conv_a_w[l],
        "a2": jnp.pad(gla_a2[l], ((0, LANES - GLA_LOWRANK), (0, 0))).astype(BF16),
        "ab": row(gla_a_bias[l]), "gn": row(gla_norm[l]),
        "ccw": conv_c_w[l], "ccb": row(conv_c_b[l]),
        "lcg": row(ln_c_g[l]), "lcb": row(ln_c_b[l]), "ldg": row(ln_d_g[l]), "ldb": row(ln_d_b[l]),
        "sgw": sg_w[l],
        "sgb": jnp.repeat(sg_b[l].T, dv, axis=1),
        "sgw0": row(jnp.repeat(sg_w[l][:, 0, 0], dv)),
        "sgb0": row(jnp.repeat(sg_b[l][:, 0], dv)),
    }


def _gla_constants(g, tl):
    qk, dv = g // 2, g // N_HEADS
    r = jnp.arange(tl)
    tri = ((r[:, None] >= r[None, :]) & (r[:, None] // GLA_CHUNK == r[None, :] // GLA_CHUNK)).astype(BF16)
    e2 = (jnp.arange(qk)[:, None] // (qk // N_HEADS) == jnp.arange(g)[None, :] // dv).astype(BF16)
    return tri, e2


def kernel(x_prompt, x_sample, state_conv_a, state_gla, state_conv_c, norm_mix, w_in, conv_a_w, gla_a2, gla_a_bias,
           gla_norm, conv_c_w, conv_c_b, ln_c_g, ln_c_b, ln_d_g, ln_d_b, sg_w, sg_b, w_o, norm_ffn, w_gate, w_up,
           w_down, norm_final):
    batch, seq, d = x_prompt.shape
    nb, dec_seq, _ = x_sample.shape
    depth = w_in.shape[0]
    g = d // 4
    qk, dv = g // 2, g // N_HEADS
    assert dec_seq == 1 and seq % SG_CHUNK == 0 and g % (N_HEADS * LANES) == 0
    dff = w_gate.shape[2]
    n_main = w_in.shape[2] - GLA_LOWRANK
    tl = _largest_tile(seq, FRONT_ROWS, SG_CHUNK)
    tri, e2 = _gla_constants(g, tl)

    lr0 = 3 * g + 2 * qk + 2 * g
    w_head = w_in[:, :, :lr0].astype(BF16)
    w_tail = w_in[:, :, lr0 + GLA_LOWRANK:].astype(BF16)
    w_lr = jnp.pad(w_in[:, :, lr0:lr0 + GLA_LOWRANK], ((0, 0), (0, 0), (0, LANES - GLA_LOWRANK))).astype(BF16)

    mp = batch * seq
    tm_p = _largest_tile(mp, 1024, 16)
    tm_s = nb
    tn_in = _largest_tile(math.gcd(n_main, lr0), 1024, 256)
    tf_cast = _largest_tile(dff, 256, 256)
    tf = _largest_tile(dff, 512, 256)

    xp = x_prompt.reshape(mp, d)
    xs = x_sample.reshape(nb, d)
    st_a, st_c = state_conv_a, state_conv_c
    st_s = state_gla.reshape(depth, nb, qk, dv)
    outs_p = {k: [] for k in ("a", "s", "c", "v")}
    sv_s = []
    for l in range(depth):
        wts = _layer_weights(l, g, conv_a_w, gla_a2, gla_a_bias, gla_norm, conv_c_w, conv_c_b,
                             ln_c_g, ln_c_b, ln_d_g, ln_d_b, sg_w, sg_b)
        wts["tri"], wts["e2"] = tri, e2
        gm, gf = norm_mix[l].reshape(1, d), norm_ffn[l].reshape(1, d)
        gfin = norm_final.reshape(1, d)
        final = l == depth - 1

        z, zlr = _norm_proj(xs, gm, w_head, w_tail, w_lr, l, tm_s, tn_in)
        y, st_a, st_s, st_c, sv = _mixer_sample(z, zlr, st_a, st_s, st_c, wts, l, 32 if l else 16, l > 0)
        xs, hs, w_ob = _proj_res(y, w_o, xs, gf, l, tn_in)

        xp, hn, na, s_new, nc, svp = _front(xp, gm, w_head, w_tail, w_lr, w_ob, gf, l, batch, seq, wts, tl)
        xp, xs, wg_b, wu_b, wd_b = _ffn(hn, xp, w_gate, w_up, w_down, gfin, l, tm_p, tf_cast, 0, 1, final,
                                        rider=(hs, xs))
        if mp > tm_p:
            xp, = _ffn(hn, xp, wg_b, wu_b, wd_b, gfin, None, tm_p, tf, 1, mp // tm_p - 1, final)
        outs_p["a"].append(na)
        outs_p["s"].append(s_new.reshape(batch, N_HEADS, qk // N_HEADS, dv))
        outs_p["c"].append(nc)
        outs_p["v"].append(svp)

        sv_s.append(sv.reshape(nb, 1, g))

    st = jnp.stack
    return (xp.reshape(batch, seq, d), xs.reshape(nb, 1, d),
            st(outs_p["a"]), st_a, st(outs_p["s"]), st_s.reshape(depth, nb, N_HEADS, qk // N_HEADS, dv),
            st(outs_p["c"]), st_c, st(outs_p["v"]), st(sv_s))
```

```python
import functools
import math

import jax
import jax.numpy as jnp
from jax import lax
from jax.experimental import pallas as pl
from jax.experimental.pallas import tpu as pltpu

F32 = jnp.float32
BF16 = jnp.bfloat16
EPS = 1e-6

N_HEADS = 4
GLA_LOWRANK = 16
GLA_TAU = 16.0
SC_WIDTH = 3
CF_WIDTH = 31
SG_CHUNK = 128
GLA_CHUNK = 64
GLA_SUB = 16
FRONT_ROWS = 256
PROJ_SLAB = 256
LANES = 128
VMEM_LIMIT = 60 * 1024 * 1024


def _dot(a, b):
    return jnp.dot(a, b, preferred_element_type=F32)


def _dot_nt(a, b):
    return lax.dot_general(a, b, (((1,), (1,)), ((), ())), preferred_element_type=F32)


def _dot_tn(a, b):
    return lax.dot_general(a, b, (((0,), (0,)), ((), ())), preferred_element_type=F32)


def _split_bf16(x):
    hi = x.astype(BF16)
    lo = (x - hi.astype(F32)).astype(BF16)
    return hi, lo


def _rmsnorm(x, g):
    return x * lax.rsqrt(jnp.mean(x * x, axis=-1, keepdims=True) + EPS) * g


def _layernorm(x, g, b):
    mu = jnp.mean(x, axis=-1, keepdims=True)
    xc = x - mu
    return xc * lax.rsqrt(jnp.mean(xc * xc, axis=-1, keepdims=True) + EPS) * g + b


def _sigmoid(x):
    return 1.0 / (1.0 + jnp.exp(-x))


def _silu(x):
    return x * _sigmoid(x)


def _gelu_tanh(x):
    c = math.sqrt(2.0 / math.pi)
    return 0.5 * x * (1.0 + jnp.tanh(c * (x + 0.044715 * (x * x * x))))


def _log_sigmoid(x):
    return jnp.minimum(x, 0.0) - jnp.log(1.0 + jnp.exp(-jnp.abs(x)))


def _norm_proj_kernel(x_ref, g_ref, wh_ref, wt_ref, wlr_ref, z_ref, zlr_ref, h_ref, *, n_head):
    j = pl.program_id(1)

    @pl.when(j == 0)
    def _():
        h = _rmsnorm(x_ref[...], g_ref[...]).astype(BF16)
        h_ref[...] = h
        zlr_ref[...] = _dot(h, wlr_ref[...])

    @pl.when(j < n_head)
    def _():
        z_ref[...] = _dot(h_ref[...], wh_ref[...])

    @pl.when(j >= n_head)
    def _():
        z_ref[...] = _dot(h_ref[...], wt_ref[...])


def _norm_proj(x, g, w_head, w_tail, w_lr, l, tm, tn):
    m, d = x.shape
    n_head, n_tail = w_head.shape[2] // tn, w_tail.shape[2] // tn
    return pl.pallas_call(
        functools.partial(_norm_proj_kernel, n_head=n_head),
        grid=(m // tm, n_head + n_tail),
        in_specs=[
            pl.BlockSpec((tm, d), lambda i, j: (i, 0)),
            pl.BlockSpec((1, d), lambda i, j: (0, 0)),
            pl.BlockSpec((None, d, tn), lambda i, j: (l, 0, jnp.minimum(j, n_head - 1))),
            pl.BlockSpec((None, d, tn), lambda i, j: (l, 0, jnp.maximum(j - n_head, 0))),
            pl.BlockSpec((None, d, LANES), lambda i, j: (l, 0, 0)),
        ],
        out_specs=[
            pl.BlockSpec((tm, tn), lambda i, j: (i, j)),
            pl.BlockSpec((tm, LANES), lambda i, j: (i, 0)),
        ],
        out_shape=[jax.ShapeDtypeStruct((m, (n_head + n_tail) * tn), F32), jax.ShapeDtypeStruct((m, LANES), F32)],
        scratch_shapes=[pltpu.VMEM((tm, d), BF16)],
        compiler_params=pltpu.CompilerParams(
            dimension_semantics=("arbitrary", "arbitrary"), vmem_limit_bytes=VMEM_LIMIT),
    )(x, g, w_head, w_tail, w_lr)


def _proj_res_kernel(y_ref, w_ref, x_ref, g_ref, o_ref, h_ref, wb_ref):
    k = pl.program_id(0)

    @pl.when(k == 0)
    def _():
        o_ref[...] = x_ref[...]

    w = w_ref[...].astype(BF16)
    wb_ref[...] = w
    o_ref[...] += _dot(y_ref[...], w)

    @pl.when(k == pl.num_programs(0) - 1)
    def _():
        h_ref[...] = _rmsnorm(o_ref[...], g_ref[...]).astype(BF16)


def _proj_res(y, w, x, g, l, tk):
    m, k = y.shape
    n = w.shape[2]
    return pl.pallas_call(
        _proj_res_kernel,
        grid=(k // tk,),
        in_specs=[
            pl.BlockSpec((m, tk), lambda i: (0, i)),
            pl.BlockSpec((None, tk, n), lambda i: (l, i, 0)),
            pl.BlockSpec((m, n), lambda i: (0, 0)),
            pl.BlockSpec((1, n), lambda i: (0, 0)),
        ],
        out_specs=[pl.BlockSpec((m, n), lambda i: (0, 0)), pl.BlockSpec((m, n), lambda i: (0, 0)),
                   pl.BlockSpec((tk, n), lambda i: (i, 0))],
        out_shape=[jax.ShapeDtypeStruct((m, n), F32), jax.ShapeDtypeStruct((m, n), BF16),
                   jax.ShapeDtypeStruct((k, n), BF16)],
        compiler_params=pltpu.CompilerParams(
            dimension_semantics=("arbitrary",), vmem_limit_bytes=VMEM_LIMIT),
    )(y, w, x, g)


def _ffn_kernel(*refs, n_groups, final_norm):
    hx = [(refs[2 * k], refs[2 * k + 1]) for k in range(n_groups)]
    wg_ref, wu_ref, wd_ref, g_ref = refs[2 * n_groups:2 * n_groups + 4]
    o_refs = refs[2 * n_groups + 4:3 * n_groups + 4]
    cast_refs = refs[3 * n_groups + 4:]
    j = pl.program_id(1)

    @pl.when(j == 0)
    def _():
        for (_, x_ref), o_ref in zip(hx, o_refs):
            o_ref[...] = x_ref[...]

    wg, wu, wd = wg_ref[...].astype(BF16), wu_ref[...].astype(BF16), wd_ref[...].astype(BF16)
    for ref, w in zip(cast_refs, (wg, wu, wd)):
        ref[...] = w
    for (h_ref, _), o_ref in zip(hx, o_refs):
        h = h_ref[...]
        o_ref[...] += _dot((_silu(_dot(h, wg)) * _dot(h, wu)).astype(BF16), wd)

    if final_norm:
        @pl.when(j == pl.num_programs(1) - 1)
        def _():
            for o_ref in o_refs:
                o_ref[...] = _rmsnorm(o_ref[...], g_ref[...])


def _ffn(h, x, wg, wu, wd, g, l, tm, tf, tile0, n_tiles, final_norm, rider=None):
    m, d = x.shape
    f = wg.shape[-1]
    rows = lambda i, j: (tile0 + i, 0)
    row_mode = pl.Buffered(1) if n_tiles == 1 else None
    row_specs = [pl.BlockSpec((tm, d), rows, pipeline_mode=row_mode)] * 2
    out_specs = [pl.BlockSpec((tm, d), rows)]
    out_shape = [jax.ShapeDtypeStruct((m, d), F32)]
    operands = [h, x]
    aliases = {1: 0}
    if rider is not None:
        assert n_tiles == 1
        whole = pl.BlockSpec(rider[1].shape, lambda i, j: (0, 0))
        row_specs += [whole, whole]
        out_specs.append(whole)
        out_shape.append(jax.ShapeDtypeStruct(rider[1].shape, F32))
        operands += list(rider)
        aliases[3] = 1
    if l is None:
        w_specs = [pl.BlockSpec((d, tf), lambda i, j: (0, j)), pl.BlockSpec((d, tf), lambda i, j: (0, j)),
                   pl.BlockSpec((tf, d), lambda i, j: (j, 0))]
    else:
        w_specs = [pl.BlockSpec((None, d, tf), lambda i, j: (l, 0, j)),
                   pl.BlockSpec((None, d, tf), lambda i, j: (l, 0, j)),
                   pl.BlockSpec((None, tf, d), lambda i, j: (l, j, 0))]
        out_specs += [pl.BlockSpec((d, tf), lambda i, j: (0, j)), pl.BlockSpec((d, tf), lambda i, j: (0, j)),
                      pl.BlockSpec((tf, d), lambda i, j: (j, 0))]
        out_shape += [jax.ShapeDtypeStruct((d, f), BF16), jax.ShapeDtypeStruct((d, f), BF16),
                      jax.ShapeDtypeStruct((f, d), BF16)]
    return pl.pallas_call(
        functools.partial(_ffn_kernel, n_groups=len(operands) // 2, final_norm=final_norm),
        grid=(n_tiles, f // tf),
        in_specs=row_specs + w_specs + [pl.BlockSpec((1, d), lambda i, j: (0, 0))],
        out_specs=out_specs,
        out_shape=out_shape,
        input_output_aliases=aliases,
        compiler_params=pltpu.CompilerParams(
            dimension_semantics=("arbitrary", "arbitrary"), vmem_limit_bytes=VMEM_LIMIT),
    )(*operands, wg, wu, wd, g)


def _gla_gate(zlr, a2_ref, ab_ref):
    xg = _dot(zlr.astype(BF16), a2_ref[...]) + ab_ref[...]
    return _log_sigmoid(xg) * (1.0 / GLA_TAU)


def _head_masks(qk):
    lane_head = lax.broadcasted_iota(jnp.int32, (1, qk), 1) // (qk // N_HEADS)
    return [(lane_head == h).astype(F32) for h in range(N_HEADS)]


def _gla_chunk(qc, kc, vc, bc, la_hi, la_lo, s_flat, e2, hm):
    c, qk = qc.shape
    dv = vc.shape[1] // N_HEADS
    dk = qk // N_HEADS
    nb = c // GLA_SUB
    vcb = vc.astype(BF16)
    sub_row = lax.broadcasted_iota(jnp.int32, (GLA_SUB, 1), 0)
    col_j = lax.broadcasted_iota(jnp.int32, (1, c), 1)

    sg = 8
    o_diag = []
    for blk in range(nb):
        r0 = blk * GLA_SUB
        q_b, k_b, b_b = qc[r0:r0 + GLA_SUB], kc[r0:r0 + GLA_SUB], bc[r0:r0 + GLA_SUB]
        v_b = vc[r0:r0 + GLA_SUB]
        ts, first = [], []
        for j in range(GLA_SUB):
            i0 = j // sg * sg
            dec = jnp.exp2(jnp.where(sub_row[i0:] >= j, b_b[i0:] - b_b[j:j + 1], -jnp.inf))
            ts.append(q_b[i0:] * k_b[j:j + 1] * dec)
            first.append(i0)
        r = _dot(jnp.concatenate(ts, axis=0).astype(BF16), e2)
        acc = [None] * (GLA_SUB // sg)
        off = 0
        for j in range(GLA_SUB):
            for i0 in range(first[j], GLA_SUB, sg):
                term = r[off:off + sg] * v_b[j:j + 1]
                acc[i0 // sg] = term if acc[i0 // sg] is None else acc[i0 // sg] + term
                off += sg
        o_diag.extend(acc)
    o = jnp.concatenate(o_diag, axis=0)

    bref = [None] + [bc[blk * GLA_SUB - 1:blk * GLA_SUB] for blk in range(1, nb)]
    bref_rows = jnp.concatenate(
        [jnp.zeros((GLA_SUB, qk), F32)] + [jnp.broadcast_to(bref[blk], (GLA_SUB, qk)) for blk in range(1, nb)],
        axis=0)
    qt = qc * jnp.exp2(bc - bref_rows)
    att_blk = [None]
    for blk in range(1, nb):
        kt = (kc * jnp.exp2(jnp.minimum(bref[blk] - bc, 0.0))).astype(BF16)
        q_b = qt[blk * GLA_SUB:(blk + 1) * GLA_SUB]
        qm = jnp.concatenate([q_b * hm[h] for h in range(N_HEADS)], axis=0).astype(BF16)
        a = _dot_nt(qm, kt)
        att_blk.append(jnp.where(col_j < blk * GLA_SUB, a, 0.0))
    o_off = []
    for h in range(N_HEADS):
        att = jnp.concatenate(
            [jnp.zeros((GLA_SUB, c), F32)] + [att_blk[blk][h * GLA_SUB:(h + 1) * GLA_SUB] for blk in range(1, nb)],
            axis=0)
        o_off.append(_dot(att.astype(BF16), vcb[:, h * dv:(h + 1) * dv]))
    o = o + jnp.concatenate(o_off, axis=1)

    qe = qc * jnp.exp2(bc)
    qe_m = jnp.concatenate([qe * hm[h] for h in range(N_HEADS)], axis=0).astype(BF16)
    o_int = _dot(qe_m, s_flat.astype(BF16))
    o = o + jnp.concatenate([o_int[h * c:(h + 1) * c] for h in range(N_HEADS)], axis=1)

    kk = (kc * jnp.exp2(bc[c - 1:c] - bc)).astype(BF16)
    kv = _dot_tn(kk, vcb)
    kv_d = jnp.concatenate([kv[h * dk:(h + 1) * dk, h * dv:(h + 1) * dv] for h in range(N_HEADS)], axis=0)
    ones = jnp.ones((c, dv), BF16)
    decay = jnp.exp(_dot_tn(la_hi, ones) + _dot_tn(la_lo, ones))
    return o, decay * s_flat + kv_d


def _gla_out(o, gn, r):
    dv = o.shape[1] // N_HEADS
    parts = []
    for h in range(N_HEADS):
        oh = o[:, h * dv:(h + 1) * dv]
        parts.append(oh * lax.rsqrt(jnp.mean(oh * oh, axis=-1, keepdims=True) + EPS))
    return jnp.concatenate(parts, axis=1) * gn * _silu(r)


def _cols(g):
    qk = g // 2
    c = {}
    off = 0
    for name, width in (("a_x", g), ("a_b", g), ("a_c", g), ("q", qk), ("k", qk), ("v", g), ("r", g),
                        ("c_val", g), ("c_gate", g), ("d_u", g), ("d_v", g)):
        c[name] = (off, off + width)
        off += width
    return c, off


def _conv_taps(ccw_ref, cbuf, zbuf, tl, head, between):
    sub = 8
    base = head - (CF_WIDTH - 1)
    cols = []
    for c0 in range(0, cbuf.shape[1], LANES):
        lanes = slice(c0, c0 + LANES)
        y = None
        for r in range(sub):
            taps = [k for k in range(CF_WIDTH) if (base + k) % sub == r]
            rows = tl if r == 0 else tl + sub
            part = None
            for k in taps:
                off = base + k - r
                term = ccw_ref[k:k + 1, lanes] * cbuf[off:off + rows, lanes]
                part = term if part is None else part + term
            if r == 0:
                shifted = part
            else:
                zbuf[0:rows, lanes] = part
                shifted = zbuf[r:r + tl, lanes]
            y = shifted if y is None else y + shifted
            between()
        cols.append(y)
    return jnp.concatenate(cols, axis=1)


def _front_kernel(x_ref, gm_ref, wh_ref, wt_ref, wlr_ref, caw_ref, a2_ref, ab_ref, gn_ref, ccw_ref, ccb_ref,
                  lcg_ref, lcb_ref, ldg_ref, ldb_ref, sgw_ref, sgb_ref, tri_ref, e2_ref, wo_ref, gf_ref,
                  o_ref, hn_ref, na_ref, s_ref, nc_ref, sv_ref, abuf, cbuf, zbuf):
    t = pl.program_id(1)
    last = pl.num_programs(1) - 1
    tl = x_ref.shape[0]
    g = gn_ref.shape[1]
    qk = g // 2
    dv = g // N_HEADS
    cols, _ = _cols(g)
    a_head, c_head = 8, 32

    @pl.when(t == 0)
    def _():
        abuf[0:a_head, :] = jnp.zeros((a_head, g), F32)
        cbuf[0:c_head, :] = jnp.zeros((c_head, g), F32)
        s_ref[...] = jnp.zeros(s_ref.shape, F32)

    x = x_ref[...]
    h = _rmsnorm(x, gm_ref[...]).astype(BF16)

    slab = PROJ_SLAB
    order = [c0 for n in ("c_val", "c_gate", "a_x", "a_b", "a_c", "d_u", "d_v", "q", "k", "v", "r")
             for c0 in range(cols[n][0], cols[n][1], slab) if c0 % slab == 0]
    pending = list(dict.fromkeys(order))
    slabs = {}

    n_head = wh_ref.shape[1]

    def pump():
        if pending:
            c0 = pending.pop(0)
            w = wh_ref[:, c0:c0 + slab] if c0 < n_head else wt_ref[:, c0 - n_head:c0 - n_head + slab]
            slabs[c0] = _dot(h, w)

    def proj(*names):
        out = []
        for n in names:
            lo, hi = cols[n]
            parts = []
            for c0 in range(lo - lo % slab, hi, slab):
                while c0 not in slabs:
                    pump()
                s_lo, s_hi = max(lo, c0) - c0, min(hi, c0 + slab) - c0
                parts.append(slabs[c0][:, s_lo:s_hi])
            out.append(parts[0] if len(parts) == 1 else jnp.concatenate(parts, axis=1))
        return out

    def out_proj(y, idx):
        return _dot(y.astype(BF16), wo_ref[idx * g:(idx + 1) * g, :])

    c_val, c_gate = proj("c_val", "c_gate")
    cin = c_val * _sigmoid(c_gate)
    cbuf[c_head:c_head + tl, :] = cin
    conv_c = _conv_taps(ccw_ref, cbuf, zbuf, tl, c_head, pump) + ccb_ref[...]
    acc = x + out_proj(_silu(_layernorm(conv_c, lcg_ref[...], lcb_ref[...])), 2)

    a_x, a_b, a_c = proj("a_x", "a_b", "a_c")
    xa = a_c * a_x
    abuf[a_head:a_head + tl, :] = xa
    conv = caw_ref[SC_WIDTH - 1:SC_WIDTH, :] * xa
    for kk in range(SC_WIDTH - 1):
        sh = SC_WIDTH - 1 - kk
        conv = conv + caw_ref[kk:kk + 1, :] * abuf[a_head - sh:a_head - sh + tl, :]
    acc = acc + out_proj(a_b * conv, 0)

    d_u, d_v = proj("d_u", "d_v")
    du = _gelu_tanh(d_u)
    vd = _layernorm(_gelu_tanh(d_v), ldg_ref[...], ldb_ref[...])
    sv_ref[...] = vd[tl - SG_CHUNK:tl]
    vdb = vd.astype(BF16)
    row = lax.broadcasted_iota(jnp.int32, (SG_CHUNK, SG_CHUNK), 0)
    col = lax.broadcasted_iota(jnp.int32, (SG_CHUNK, SG_CHUNK), 1)
    ws = [jnp.where(row >= col, sgw_ref[hh], 0.0).astype(BF16) for hh in range(N_HEADS)]
    sv_rows = []
    for ci in range(tl // SG_CHUNK):
        sl = slice(ci * SG_CHUNK, (ci + 1) * SG_CHUNK)
        sv_rows.append(jnp.concatenate(
            [_dot(ws[hh], vdb[sl, hh * dv:(hh + 1) * dv]) for hh in range(N_HEADS)], axis=1) + sgb_ref[...])
    acc = acc + out_proj(du * jnp.concatenate(sv_rows, axis=0), 3)

    q_all, k_all, v_all, r_all = proj("q", "k", "v", "r")
    la = _gla_gate(_dot(h, wlr_ref[...]), a2_ref, ab_ref)
    la_hi, la_lo = _split_bf16(la)
    tri = tri_ref[...]
    b_all = (_dot(tri, la_hi) + _dot(tri, la_lo)) * math.log2(math.e)
    q_all = q_all * (float(qk // N_HEADS) ** -0.5)
    hm = _head_masks(qk)
    e2 = e2_ref[...]
    s_flat = s_ref[...]
    outs = []
    for ci in range(tl // GLA_CHUNK):
        sl = slice(ci * GLA_CHUNK, (ci + 1) * GLA_CHUNK)
        o_c, s_flat = _gla_chunk(q_all[sl], k_all[sl], v_all[sl], b_all[sl], la_hi[sl], la_lo[sl], s_flat, e2, hm)
        outs.append(o_c)
    s_ref[...] = s_flat
    acc = acc + out_proj(_gla_out(jnp.concatenate(outs, axis=0), gn_ref[...], r_all), 1)
    o_ref[...] = acc
    hn_ref[...] = _rmsnorm(acc, gf_ref[...]).astype(BF16)

    @pl.when(t == last)
    def _():
        na_ref[...] = abuf[a_head + tl - (SC_WIDTH - 1):a_head + tl, :]
        nc_ref[...] = cbuf[c_head + tl - (CF_WIDTH - 1):c_head + tl, :]

    abuf[0:a_head, :] = abuf[tl:tl + a_head, :]
    cbuf[0:c_head, :] = cbuf[tl:tl + c_head, :]


def _full(shape):
    nd = len(shape)
    return pl.BlockSpec(shape, lambda *_: (0,) * nd)


def _resident(a, l=None):
    if l is None:
        return pl.BlockSpec(a.shape, lambda *_: (0,) * a.ndim, pipeline_mode=pl.Buffered(1))
    return pl.BlockSpec((None,) + a.shape[1:], lambda *_: (l,) + (0,) * (a.ndim - 1), pipeline_mode=pl.Buffered(1))


def _front(x, gm, w_head, w_tail, w_lr, w_o, gf, l, batch, seq, wts, tl):
    d = x.shape[1]
    g = wts["gn"].shape[1]
    nt = seq // tl
    row_map = lambda b, t: (b * nt + t, 0)
    names = ("caw", "a2", "ab", "gn", "ccw", "ccb", "lcg", "lcb", "ldg", "ldb", "sgw", "sgb", "tri", "e2")
    consts = [wts[n] for n in names]
    return pl.pallas_call(
        _front_kernel,
        grid=(batch, nt),
        in_specs=[pl.BlockSpec((tl, d), row_map), _full(gm.shape),
                  _resident(w_head, l), _resident(w_tail, l), _resident(w_lr, l)]
                 + [_full(c.shape) for c in consts] + [_resident(w_o), _full(gf.shape)],
        out_specs=[
            pl.BlockSpec((tl, d), row_map),
            pl.BlockSpec((tl, d), row_map),
            pl.BlockSpec((None, SC_WIDTH - 1, g), lambda b, t: (b, 0, 0)),
            pl.BlockSpec((None, g // 2, g // N_HEADS), lambda b, t: (b, 0, 0)),
            pl.BlockSpec((None, CF_WIDTH - 1, g), lambda b, t: (b, 0, 0)),
            pl.BlockSpec((None, SG_CHUNK, g), lambda b, t: (b, 0, 0)),
        ],
        out_shape=[
            jax.ShapeDtypeStruct(x.shape, F32),
            jax.ShapeDtypeStruct(x.shape, BF16),
            jax.ShapeDtypeStruct((batch, SC_WIDTH - 1, g), F32),
            jax.ShapeDtypeStruct((batch, g // 2, g // N_HEADS), F32),
            jax.ShapeDtypeStruct((batch, CF_WIDTH - 1, g), F32),
            jax.ShapeDtypeStruct((batch, SG_CHUNK, g), F32),
        ],
        scratch_shapes=[pltpu.VMEM((8 + tl, g), F32), pltpu.VMEM((32 + tl, g), F32), pltpu.VMEM((8 + tl, g), F32)],
        compiler_params=pltpu.CompilerParams(
            dimension_semantics=("arbitrary", "arbitrary"), vmem_limit_bytes=VMEM_LIMIT),
    )(x, gm, w_head, w_tail, w_lr, *consts, w_o, gf)


def _mixer_sample_kernel(z_ref, zlr_ref, pa_ref, s_ref, pc_ref, caw_ref, a2_ref, ab_ref, gn_ref, ccw_ref, ccb_ref,
                         lcg_ref, lcb_ref, ldg_ref, ldb_ref, sgw0_ref, sgb0_ref,
                         y_ref, na_ref, so_ref, nc_ref, sv_ref, o_scr, *, layer):
    if layer is not None:
        for ll in range(pa_ref.shape[0]):
            if ll != layer:
                na_ref[ll] = pa_ref[ll]
                so_ref[ll] = s_ref[ll]
                nc_ref[ll] = pc_ref[ll]
        pa_ref, s_ref, pc_ref = pa_ref.at[layer], s_ref.at[layer], pc_ref.at[layer]
        na_ref, so_ref, nc_ref = na_ref.at[layer], so_ref.at[layer], nc_ref.at[layer]
    bt = z_ref.shape[0]
    g = gn_ref.shape[1]
    qk = g // 2
    dk = qk // N_HEADS
    dv = g // N_HEADS
    cols, _ = _cols(g)

    def zc(name):
        lo, hi = cols[name]
        return z_ref[:, lo:hi]

    xa = zc("a_c") * zc("a_x")
    conv = caw_ref[SC_WIDTH - 1:SC_WIDTH, :] * xa
    for kk in range(SC_WIDTH - 1):
        conv = conv + caw_ref[kk:kk + 1, :] * pa_ref[:, kk, :]
    y_ref[:, 0:g] = (zc("a_b") * conv).astype(BF16)
    for kk in range(1, SC_WIDTH - 1):
        na_ref[:, kk - 1, :] = pa_ref[:, kk, :]
    na_ref[:, SC_WIDTH - 2, :] = xa

    la = _gla_gate(zlr_ref[...], a2_ref, ab_ref)
    a_t = jnp.exp(la).T
    k_t = zc("k").T
    q_rows = zc("q") * (float(dk) ** -0.5)
    v_rows = zc("v")
    hm = _head_masks(qk)
    hm_rows = jnp.concatenate(hm + [jnp.zeros_like(hm[0])] * (8 - N_HEADS), axis=0)
    for b in range(bt):
        a_col = jnp.broadcast_to(a_t[:, b:b + 1], (qk, dv))
        k_col = jnp.broadcast_to(k_t[:, b:b + 1], (qk, dv))
        v_b = jnp.concatenate(
            [jnp.broadcast_to(v_rows[b:b + 1, h * dv:(h + 1) * dv], (dk, dv)) for h in range(N_HEADS)], axis=0)
        s_new = a_col * s_ref[b] + k_col * v_b
        so_ref[b] = s_new
        q_m = (q_rows[b:b + 1] * hm_rows).astype(BF16)
        o_b = _dot(q_m, s_new.astype(BF16))
        for h in range(N_HEADS):
            o_scr[b:b + 1, h * dv:(h + 1) * dv] = o_b[h:h + 1]
    y_ref[:, g:2 * g] = _gla_out(o_scr[...], gn_ref[...], zc("r")).astype(BF16)

    cin = zc("c_val") * _sigmoid(zc("c_gate"))
    acc = ccb_ref[...] + ccw_ref[CF_WIDTH - 1:CF_WIDTH, :] * cin
    for kk in range(CF_WIDTH - 1):
        acc = acc + ccw_ref[kk:kk + 1, :] * pc_ref[:, kk, :]
    y_ref[:, 2 * g:3 * g] = _silu(_layernorm(acc, lcg_ref[...], lcb_ref[...])).astype(BF16)
    nc_ref[:, 0:CF_WIDTH - 2, :] = pc_ref[:, 1:CF_WIDTH - 1, :]
    nc_ref[:, CF_WIDTH - 2, :] = cin

    du = _gelu_tanh(zc("d_u"))
    vd = _layernorm(_gelu_tanh(zc("d_v")), ldg_ref[...], ldb_ref[...])
    sv_ref[...] = vd
    y_ref[:, 3 * g:4 * g] = (du * (sgw0_ref[...] * vd + sgb0_ref[...])).astype(BF16)


def _mixer_sample(z, zlr, pa, s, pc, wts, l, bt, in_place):
    nb, g = z.shape[0], wts["gn"].shape[1]
    names = ("caw", "a2", "ab", "gn", "ccw", "ccb", "lcg", "lcb", "ldg", "ldb", "sgw0", "sgb0")
    consts = [wts[n] for n in names]
    rows = lambda i: (i, 0)
    if in_place:
        state_spec = lambda a: pl.BlockSpec((None, bt) + a.shape[2:], lambda i: (l, i, 0, 0))
    else:
        state_spec = lambda a: pl.BlockSpec((a.shape[0], bt) + a.shape[2:], lambda i: (0, i, 0, 0))
    return pl.pallas_call(
        functools.partial(_mixer_sample_kernel, layer=None if in_place else l),
        grid=(nb // bt,),
        in_specs=[pl.BlockSpec((bt, z.shape[1]), rows), pl.BlockSpec((bt, LANES), rows),
                  state_spec(pa), state_spec(s), state_spec(pc)] + [_full(c.shape) for c in consts],
        out_specs=[pl.BlockSpec((bt, 4 * g), rows), state_spec(pa), state_spec(s), state_spec(pc),
                   pl.BlockSpec((bt, g), rows)],
        out_shape=[
            jax.ShapeDtypeStruct((nb, 4 * g), BF16),
            jax.ShapeDtypeStruct(pa.shape, F32),
            jax.ShapeDtypeStruct(s.shape, F32),
            jax.ShapeDtypeStruct(pc.shape, F32),
            jax.ShapeDtypeStruct((nb, g), F32),
        ],
        input_output_aliases={2: 1, 3: 2, 4: 3} if in_place else {},
        scratch_shapes=[pltpu.VMEM((bt, g), F32)],
        compiler_params=pltpu.CompilerParams(
            dimension_semantics=("arbitrary",), vmem_limit_bytes=VMEM_LIMIT),
    )(z, zlr, pa, s, pc, *consts)


def _largest_tile(m, cap, mult):
    t = min(m, cap)
    while m % t or t % mult:
        t -= mult
    return t


def _layer_weights(l, g, conv_a_w, gla_a2, gla_a_bias, gla_norm, conv_c_w, conv_c_b,
                   ln_c_g, ln_c_b, ln_d_g, ln_d_b, sg_w, sg_b):
    dv = g // N_HEADS
    row = lambda v: v.reshape(1, -1)
    return {
        "caw": conv_a_w[l],
        "a2": jnp.pad(gla_a2[l], ((0, LANES - GLA_LOWRANK), (0, 0))).astype(BF16),
        "ab": row(gla_a_bias[l]), "gn": row(gla_norm[l]),
        "ccw": conv_c_w[l], "ccb": row(conv_c_b[l]),
        "lcg": row(ln_c_g[l]), "lcb": row(ln_c_b[l]), "ldg": row(ln_d_g[l]), "ldb": row(ln_d_b[l]),
        "sgw": sg_w[l],
        "sgb": jnp.repeat(sg_b[l].T, dv, axis=1),
        "sgw0": row(jnp.repeat(sg_w[l][:, 0, 0], dv)),
        "sgb0": row(jnp.repeat(sg_b[l][:, 0], dv)),
    }


def _gla_constants(g, tl):
    qk, dv = g // 2, g // N_HEADS
    r = jnp.arange(tl)
    tri = ((r[:, None] >= r[None, :]) & (r[:, None] // GLA_CHUNK == r[None, :] // GLA_CHUNK)).astype(BF16)
    e2 = (jnp.arange(qk)[:, None] // (qk // N_HEADS) == jnp.arange(g)[None, :] // dv).astype(BF16)
    return tri, e2


def kernel(x_prompt, x_sample, state_conv_a, state_gla, state_conv_c, norm_mix, w_in, conv_a_w, gla_a2, gla_a_bias,
           gla_norm, conv_c_w, conv_c_b, ln_c_g, ln_c_b, ln_d_g, ln_d_b, sg_w, sg_b, w_o, norm_ffn, w_gate, w_up,
           w_down, norm_final):
    batch, seq, d = x_prompt.shape
    nb, dec_seq, _ = x_sample.shape
    depth = w_in.shape[0]
    g = d // 4
    qk, dv = g // 2, g // N_HEADS
    assert dec_seq == 1 and seq % SG_CHUNK == 0 and g % (N_HEADS * LANES) == 0
    dff = w_gate.shape[2]
    n_main = w_in.shape[2] - GLA_LOWRANK
    tl = _largest_tile(seq, FRONT_ROWS, SG_CHUNK)
    tri, e2 = _gla_constants(g, tl)

    lr0 = 3 * g + 2 * qk + 2 * g
    w_head = w_in[:, :, :lr0].astype(BF16)
    w_tail = w_in[:, :, lr0 + GLA_LOWRANK:].astype(BF16)
    w_lr = jnp.pad(w_in[:, :, lr0:lr0 + GLA_LOWRANK], ((0, 0), (0, 0), (0, LANES - GLA_LOWRANK))).astype(BF16)

    mp = batch * seq
    tm_p = _largest_tile(mp, 1024, 16)
    tm_s = nb
    tn_in = _largest_tile(math.gcd(n_main, lr0), 1024, 256)
    tf_cast = _largest_tile(dff, 256, 256)
    tf = _largest_tile(dff, 512, 256)

    xp = x_prompt.reshape(mp, d)
    xs = x_sample.reshape(nb, d)
    st_a, st_c = state_conv_a, state_conv_c
    st_s = state_gla.reshape(depth, nb, qk, dv)
    outs_p = {k: [] for k in ("a", "s", "c", "v")}
    sv_s = []
    for l in range(depth):
        wts = _layer_weights(l, g, conv_a_w, gla_a2, gla_a_bias, gla_norm, conv_c_w, conv_c_b,
                             ln_c_g, ln_c_b, ln_d_g, ln_d_b, sg_w, sg_b)
        wts["tri"], wts["e2"] = tri, e2
        gm, gf = norm_mix[l].reshape(1, d), norm_ffn[l].reshape(1, d)
        gfin = norm_final.reshape(1, d)
        final = l == depth - 1

        z, zlr = _norm_proj(xs, gm, w_head, w_tail, w_lr, l, tm_s, tn_in)
        y, st_a, st_s, st_c, sv = _mixer_sample(z, zlr, st_a, st_s, st_c, wts, l, 32 if l else 16, l > 0)
        xs, hs, w_ob = _proj_res(y, w_o, xs, gf, l, tn_in)

        xp, hn, na, s_new, nc, svp = _front(xp, gm, w_head, w_tail, w_lr, w_ob, gf, l, batch, seq, wts, tl)
        xp, xs, wg_b, wu_b, wd_b = _ffn(hn, xp, w_gate, w_up, w_down, gfin, l, tm_p, tf_cast, 0, 1, final,
                                        rider=(hs, xs))
        if mp > tm_p:
            xp, = _ffn(hn, xp, wg_b, wu_b, wd_b, gfin, None, tm_p, tf, 1, mp // tm_p - 1, final)
        outs_p["a"].append(na)
        outs_p["s"].append(s_new.reshape(batch, N_HEADS, qk // N_HEADS, dv))
        outs_p["c"].append(nc)
        outs_p["v"].append(svp)

        sv_s.append(sv.reshape(nb, 1, g))

    st = jnp.stack
    return (xp.reshape(batch, seq, d), xs.reshape(nb, 1, d),
            st(outs_p["a"]), st_a, st(outs_p["s"]), st_s.reshape(depth, nb, N_HEADS, qk // N_HEADS, dv),
            st(outs_p["c"]), st_c, st(outs_p["v"]), st(sv_s))
```

```python
import functools
import math

import jax
import jax.numpy as jnp
from jax import lax
from jax.experimental import pallas as pl
from jax.experimental.pallas import tpu as pltpu

F32 = jnp.float32
BF16 = jnp.bfloat16
EPS = 1e-6

N_HEADS = 4
GLA_LOWRANK = 16
GLA_TAU = 16.0
SC_WIDTH = 3
CF_WIDTH = 31
SG_CHUNK = 128
GLA_CHUNK = 64
GLA_SUB = 16
FRONT_ROWS = 256
PROJ_SLAB = 256
LANES = 128
VMEM_LIMIT = 60 * 1024 * 1024


def _dot(a, b):
    return jnp.dot(a, b, preferred_element_type=F32)


def _dot_nt(a, b):
    return lax.dot_general(a, b, (((1,), (1,)), ((), ())), preferred_element_type=F32)


def _dot_tn(a, b):
    return lax.dot_general(a, b, (((0,), (0,)), ((), ())), preferred_element_type=F32)


def _split_bf16(x):
    hi = x.astype(BF16)
    lo = (x - hi.astype(F32)).astype(BF16)
    return hi, lo


def _rmsnorm(x, g):
    return x * lax.rsqrt(jnp.mean(x * x, axis=-1, keepdims=True) + EPS) * g


def _layernorm(x, g, b):
    mu = jnp.mean(x, axis=-1, keepdims=True)
    xc = x - mu
    return xc * lax.rsqrt(jnp.mean(xc * xc, axis=-1, keepdims=True) + EPS) * g + b


def _sigmoid(x):
    return 1.0 / (1.0 + jnp.exp(-x))


def _silu(x):
    return x * _sigmoid(x)


def _gelu_tanh(x):
    c = math.sqrt(2.0 / math.pi)
    return 0.5 * x * (1.0 + jnp.tanh(c * (x + 0.044715 * (x * x * x))))


def _log_sigmoid(x):
    return jnp.minimum(x, 0.0) - jnp.log(1.0 + jnp.exp(-jnp.abs(x)))


def _norm_proj_kernel(x_ref, g_ref, wh_ref, wt_ref, wlr_ref, z_ref, zlr_ref, h_ref, *, n_head):
    j = pl.program_id(1)

    @pl.when(j == 0)
    def _():
        h = _rmsnorm(x_ref[...], g_ref[...]).astype(BF16)
        h_ref[...] = h
        zlr_ref[...] = _dot(h, wlr_ref[...])

    @pl.when(j < n_head)
    def _():
        z_ref[...] = _dot(h_ref[...], wh_ref[...])

    @pl.when(j >= n_head)
    def _():
        z_ref[...] = _dot(h_ref[...], wt_ref[...])


def _norm_proj(x, g, w_head, w_tail, w_lr, l, tm, tn):
    m, d = x.shape
    n_head, n_tail = w_head.shape[2] // tn, w_tail.shape[2] // tn
    return pl.pallas_call(
        functools.partial(_norm_proj_kernel, n_head=n_head),
        grid=(m // tm, n_head + n_tail),
        in_specs=[
            pl.BlockSpec((tm, d), lambda i, j: (i, 0)),
            pl.BlockSpec((1, d), lambda i, j: (0, 0)),
            pl.BlockSpec((None, d, tn), lambda i, j: (l, 0, jnp.minimum(j, n_head - 1))),
            pl.BlockSpec((None, d, tn), lambda i, j: (l, 0, jnp.maximum(j - n_head, 0))),
            pl.BlockSpec((None, d, LANES), lambda i, j: (l, 0, 0)),
        ],
        out_specs=[
            pl.BlockSpec((tm, tn), lambda i, j: (i, j)),
            pl.BlockSpec((tm, LANES), lambda i, j: (i, 0)),
        ],
        out_shape=[jax.ShapeDtypeStruct((m, (n_head + n_tail) * tn), F32), jax.ShapeDtypeStruct((m, LANES), F32)],
        scratch_shapes=[pltpu.VMEM((tm, d), BF16)],
        compiler_params=pltpu.CompilerParams(
            dimension_semantics=("arbitrary", "arbitrary"), vmem_limit_bytes=VMEM_LIMIT),
    )(x, g, w_head, w_tail, w_lr)


def _proj_res_kernel(y_ref, w_ref, x_ref, g_ref, o_ref, h_ref, wb_ref):
    k = pl.program_id(0)

    @pl.when(k == 0)
    def _():
        o_ref[...] = x_ref[...]

    w = w_ref[...].astype(BF16)
    wb_ref[...] = w
    o_ref[...] += _dot(y_ref[...], w)

    @pl.when(k == pl.num_programs(0) - 1)
    def _():
        h_ref[...] = _rmsnorm(o_ref[...], g_ref[...]).astype(BF16)


def _proj_res(y, w, x, g, l, tk):
    m, k = y.shape
    n = w.shape[2]
    return pl.pallas_call(
        _proj_res_kernel,
        grid=(k // tk,),
        in_specs=[
            pl.BlockSpec((m, tk), lambda i: (0, i)),
            pl.BlockSpec((None, tk, n), lambda i: (l, i, 0)),
            pl.BlockSpec((m, n), lambda i: (0, 0)),
            pl.BlockSpec((1, n), lambda i: (0, 0)),
        ],
        out_specs=[pl.BlockSpec((m, n), lambda i: (0, 0)), pl.BlockSpec((m, n), lambda i: (0, 0)),
                   pl.BlockSpec((tk, n), lambda i: (i, 0))],
        out_shape=[jax.ShapeDtypeStruct((m, n), F32), jax.ShapeDtypeStruct((m, n), BF16),
                   jax.ShapeDtypeStruct((k, n), BF16)],
        compiler_params=pltpu.CompilerParams(
            dimension_semantics=("arbitrary",), vmem_limit_bytes=VMEM_LIMIT),
    )(y, w, x, g)


def _ffn_kernel(*refs, n_groups, final_norm):
    hx = [(refs[2 * k], refs[2 * k + 1]) for k in range(n_groups)]
    wg_ref, wu_ref, wd_ref, g_ref = refs[2 * n_groups:2 * n_groups + 4]
    o_refs = refs[2 * n_groups + 4:3 * n_groups + 4]
    cast_refs = refs[3 * n_groups + 4:]
    j = pl.program_id(1)

    @pl.when(j == 0)
    def _():
        for (_, x_ref), o_ref in zip(hx, o_refs):
            o_ref[...] = x_ref[...]

    wg, wu, wd = wg_ref[...].astype(BF16), wu_ref[...].astype(BF16), wd_ref[...].astype(BF16)
    for ref, w in zip(cast_refs, (wg, wu, wd)):
        ref[...] = w
    for (h_ref, _), o_ref in zip(hx, o_refs):
        h = h_ref[...]
        o_ref[...] += _dot((_silu(_dot(h, wg)) * _dot(h, wu)).astype(BF16), wd)

    if final_norm:
        @pl.when(j == pl.num_programs(1) - 1)
        def _():
            for o_ref in o_refs:
                o_ref[...] = _rmsnorm(o_ref[...], g_ref[...])


def _ffn(h, x, wg, wu, wd, g, l, tm, tf, tile0, n_tiles, final_norm, rider=None):
    m, d = x.shape
    f = wg.shape[-1]
    rows = lambda i, j: (tile0 + i, 0)
    row_mode = pl.Buffered(1) if n_tiles == 1 else None
    row_specs = [pl.BlockSpec((tm, d), rows, pipeline_mode=row_mode)] * 2
    out_specs = [pl.BlockSpec((tm, d), rows)]
    out_shape = [jax.ShapeDtypeStruct((m, d), F32)]
    operands = [h, x]
    aliases = {1: 0}
    if rider is not None:
        assert n_tiles == 1
        whole = pl.BlockSpec(rider[1].shape, lambda i, j: (0, 0))
        row_specs += [whole, whole]
        out_specs.append(whole)
        out_shape.append(jax.ShapeDtypeStruct(rider[1].shape, F32))
        operands += list(rider)
        aliases[3] = 1
    if l is None:
        w_specs = [pl.BlockSpec((d, tf), lambda i, j: (0, j)), pl.BlockSpec((d, tf), lambda i, j: (0, j)),
                   pl.BlockSpec((tf, d), lambda i, j: (j, 0))]
    else:
        w_specs = [pl.BlockSpec((None, d, tf), lambda i, j: (l, 0, j)),
                   pl.BlockSpec((None, d, tf), lambda i, j: (l, 0, j)),
                   pl.BlockSpec((None, tf, d), lambda i, j: (l, j, 0))]
        out_specs += [pl.BlockSpec((d, tf), lambda i, j: (0, j)), pl.BlockSpec((d, tf), lambda i, j: (0, j)),
                      pl.BlockSpec((tf, d), lambda i, j: (j, 0))]
        out_shape += [jax.ShapeDtypeStruct((d, f), BF16), jax.ShapeDtypeStruct((d, f), BF16),
                      jax.ShapeDtypeStruct((f, d), BF16)]
    return pl.pallas_call(
        functools.partial(_ffn_kernel, n_groups=len(operands) // 2, final_norm=final_norm),
        grid=(n_tiles, f // tf),
        in_specs=row_specs + w_specs + [pl.BlockSpec((1, d), lambda i, j: (0, 0))],
        out_specs=out_specs,
        out_shape=out_shape,
        input_output_aliases=aliases,
        compiler_params=pltpu.CompilerParams(
            dimension_semantics=("arbitrary", "arbitrary"), vmem_limit_bytes=VMEM_LIMIT),
    )(*operands, wg, wu, wd, g)


def _gla_gate(zlr, a2_ref, ab_ref):
    xg = _dot(zlr.astype(BF16), a2_ref[...]) + ab_ref[...]
    return _log_sigmoid(xg) * (1.0 / GLA_TAU)


def _head_masks(qk):
    lane_head = lax.broadcasted_iota(jnp.int32, (1, qk), 1) // (qk // N_HEADS)
    return [(lane_head == h).astype(F32) for h in range(N_HEADS)]


def _gla_chunk(qc, kc, vc, bc, la_hi, la_lo, s_flat, e2, hm):
    c, qk = qc.shape
    dv = vc.shape[1] // N_HEADS
    dk = qk // N_HEADS
    nb = c // GLA_SUB
    vcb = vc.astype(BF16)
    sub_row = lax.broadcasted_iota(jnp.int32, (GLA_SUB, 1), 0)
    col_j = lax.broadcasted_iota(jnp.int32, (1, c), 1)

    sg = 8
    ts = []
    for blk in range(nb):
        r0 = blk * GLA_SUB
        q_b, k_b, b_b = qc[r0:r0 + GLA_SUB], kc[r0:r0 + GLA_SUB], bc[r0:r0 + GLA_SUB]
        for j in range(GLA_SUB):
            i0 = j // sg * sg
            dec = jnp.exp2(jnp.where(sub_row[i0:] >= j, b_b[i0:] - b_b[j:j + 1], -jnp.inf))
            ts.append(q_b[i0:] * k_b[j:j + 1] * dec)
    r = _dot(jnp.concatenate(ts, axis=0).astype(BF16), e2)
    o_diag = []
    off = 0
    for blk in range(nb):
        v_b = vc[blk * GLA_SUB:(blk + 1) * GLA_SUB]
        acc = [None] * (GLA_SUB // sg)
        for j in range(GLA_SUB):
            for i0 in range(j // sg * sg, GLA_SUB, sg):
                term = r[off:off + sg] * v_b[j:j + 1]
                acc[i0 // sg] = term if acc[i0 // sg] is None else acc[i0 // sg] + term
                off += sg
        o_diag.extend(acc)
    o = jnp.concatenate(o_diag, axis=0)

    bref = [None] + [bc[blk * GLA_SUB - 1:blk * GLA_SUB] for blk in range(1, nb)]
    bref_rows = jnp.concatenate(
        [jnp.zeros((GLA_SUB, qk), F32)] + [jnp.broadcast_to(bref[blk], (GLA_SUB, qk)) for blk in range(1, nb)],
        axis=0)
    qt = qc * jnp.exp2(bc - bref_rows)
    att_blk = [None]
    for blk in range(1, nb):
        kt = (kc * jnp.exp2(jnp.minimum(bref[blk] - bc, 0.0))).astype(BF16)
        q_b = qt[blk * GLA_SUB:(blk + 1) * GLA_SUB]
        qm = jnp.concatenate([q_b * hm[h] for h in range(N_HEADS)], axis=0).astype(BF16)
        a = _dot_nt(qm, kt)
        att_blk.append(jnp.where(col_j < blk * GLA_SUB, a, 0.0))
    o_off = []
    for h in range(N_HEADS):
        att = jnp.concatenate(
            [jnp.zeros((GLA_SUB, c), F32)] + [att_blk[blk][h * GLA_SUB:(h + 1) * GLA_SUB] for blk in range(1, nb)],
            axis=0)
        o_off.append(_dot(att.astype(BF16), vcb[:, h * dv:(h + 1) * dv]))
    o = o + jnp.concatenate(o_off, axis=1)

    qe = qc * jnp.exp2(bc)
    qe_m = jnp.concatenate([qe * hm[h] for h in range(N_HEADS)], axis=0).astype(BF16)
    o_int = _dot(qe_m, s_flat.astype(BF16))
    o = o + jnp.concatenate([o_int[h * c:(h + 1) * c] for h in range(N_HEADS)], axis=1)

    kk = (kc * jnp.exp2(bc[c - 1:c] - bc)).astype(BF16)
    kv = _dot_tn(kk, vcb)
    kv_d = jnp.concatenate([kv[h * dk:(h + 1) * dk, h * dv:(h + 1) * dv] for h in range(N_HEADS)], axis=0)
    ones = jnp.ones((c, dv), BF16)
    decay = jnp.exp(_dot_tn(la_hi, ones) + _dot_tn(la_lo, ones))
    return o, decay * s_flat + kv_d


def _gla_out(o, gn, r):
    dv = o.shape[1] // N_HEADS
    parts = []
    for h in range(N_HEADS):
        oh = o[:, h * dv:(h + 1) * dv]
        parts.append(oh * lax.rsqrt(jnp.mean(oh * oh, axis=-1, keepdims=True) + EPS))
    return jnp.concatenate(parts, axis=1) * gn * _silu(r)


def _cols(g):
    qk = g // 2
    c = {}
    off = 0
    for name, width in (("a_x", g), ("a_b", g), ("a_c", g), ("q", qk), ("k", qk), ("v", g), ("r", g),
                        ("c_val", g), ("c_gate", g), ("d_u", g), ("d_v", g)):
        c[name] = (off, off + width)
        off += width
    return c, off


def _conv_taps(ccw_ref, cbuf, zbuf, tl, head, between):
    sub = 8
    base = head - (CF_WIDTH - 1)
    cols = []
    for c0 in range(0, cbuf.shape[1], LANES):
        lanes = slice(c0, c0 + LANES)
        y = None
        for r in range(sub):
            taps = [k for k in range(CF_WIDTH) if (base + k) % sub == r]
            rows = tl if r == 0 else tl + sub
            part = None
            for k in taps:
                off = base + k - r
                term = ccw_ref[k:k + 1, lanes] * cbuf[off:off + rows, lanes]
                part = term if part is None else part + term
            if r == 0:
                shifted = part
            else:
                zbuf[0:rows, lanes] = part
                shifted = zbuf[r:r + tl, lanes]
            y = shifted if y is None else y + shifted
            between()
        cols.append(y)
    return jnp.concatenate(cols, axis=1)


def _front_kernel(x_ref, gm_ref, wh_ref, wt_ref, wlr_ref, caw_ref, a2_ref, ab_ref, gn_ref, ccw_ref, ccb_ref,
                  lcg_ref, lcb_ref, ldg_ref, ldb_ref, sgw_ref, sgb_ref, tri_ref, e2_ref, wo_ref, gf_ref,
                  o_ref, hn_ref, na_ref, s_ref, nc_ref, sv_ref, abuf, cbuf, zbuf):
    t = pl.program_id(1)
    last = pl.num_programs(1) - 1
    tl = x_ref.shape[0]
    g = gn_ref.shape[1]
    qk = g // 2
    dv = g // N_HEADS
    cols, _ = _cols(g)
    a_head, c_head = 8, 32

    @pl.when(t == 0)
    def _():
        abuf[0:a_head, :] = jnp.zeros((a_head, g), F32)
        cbuf[0:c_head, :] = jnp.zeros((c_head, g), F32)
        s_ref[...] = jnp.zeros(s_ref.shape, F32)

    x = x_ref[...]
    h = _rmsnorm(x, gm_ref[...]).astype(BF16)

    slab = PROJ_SLAB
    order = [c0 for n in ("c_val", "c_gate", "a_x", "a_b", "a_c", "d_u", "d_v", "q", "k", "v", "r")
             for c0 in range(cols[n][0], cols[n][1], slab) if c0 % slab == 0]
    pending = list(dict.fromkeys(order))
    slabs = {}

    n_head = wh_ref.shape[1]

    def pump():
        if pending:
            c0 = pending.pop(0)
            w = wh_ref[:, c0:c0 + slab] if c0 < n_head else wt_ref[:, c0 - n_head:c0 - n_head + slab]
            slabs[c0] = _dot(h, w)

    def proj(*names):
        out = []
        for n in names:
            lo, hi = cols[n]
            parts = []
            for c0 in range(lo - lo % slab, hi, slab):
                while c0 not in slabs:
                    pump()
                s_lo, s_hi = max(lo, c0) - c0, min(hi, c0 + slab) - c0
                parts.append(slabs[c0][:, s_lo:s_hi])
            out.append(parts[0] if len(parts) == 1 else jnp.concatenate(parts, axis=1))
        return out

    def out_proj(y, idx):
        return _dot(y.astype(BF16), wo_ref[idx * g:(idx + 1) * g, :])

    c_val, c_gate = proj("c_val", "c_gate")
    cin = c_val * _sigmoid(c_gate)
    cbuf[c_head:c_head + tl, :] = cin
    conv_c = _conv_taps(ccw_ref, cbuf, zbuf, tl, c_head, pump) + ccb_ref[...]
    acc = x + out_proj(_silu(_layernorm(conv_c, lcg_ref[...], lcb_ref[...])), 2)

    a_x, a_b, a_c = proj("a_x", "a_b", "a_c")
    xa = a_c * a_x
    abuf[a_head:a_head + tl, :] = xa
    conv = caw_ref[SC_WIDTH - 1:SC_WIDTH, :] * xa
    for kk in range(SC_WIDTH - 1):
        sh = SC_WIDTH - 1 - kk
        conv = conv + caw_ref[kk:kk + 1, :] * abuf[a_head - sh:a_head - sh + tl, :]
    acc = acc + out_proj(a_b * conv, 0)

    d_u, d_v = proj("d_u", "d_v")
    du = _gelu_tanh(d_u)
    vd = _layernorm(_gelu_tanh(d_v), ldg_ref[...], ldb_ref[...])
    sv_ref[...] = vd[tl - SG_CHUNK:tl]
    vdb = vd.astype(BF16)
    row = lax.broadcasted_iota(jnp.int32, (SG_CHUNK, SG_CHUNK), 0)
    col = lax.broadcasted_iota(jnp.int32, (SG_CHUNK, SG_CHUNK), 1)
    ws = [jnp.where(row >= col, sgw_ref[hh], 0.0).astype(BF16) for hh in range(N_HEADS)]
    sv_rows = []
    for ci in range(tl // SG_CHUNK):
        sl = slice(ci * SG_CHUNK, (ci + 1) * SG_CHUNK)
        sv_rows.append(jnp.concatenate(
            [_dot(ws[hh], vdb[sl, hh * dv:(hh + 1) * dv]) for hh in range(N_HEADS)], axis=1) + sgb_ref[...])
    acc = acc + out_proj(du * jnp.concatenate(sv_rows, axis=0), 3)

    q_all, k_all, v_all, r_all = proj("q", "k", "v", "r")
    la = _gla_gate(_dot(h, wlr_ref[...]), a2_ref, ab_ref)
    la_hi, la_lo = _split_bf16(la)
    tri = tri_ref[...]
    b_all = (_dot(tri, la_hi) + _dot(tri, la_lo)) * math.log2(math.e)
    q_all = q_all * (float(qk // N_HEADS) ** -0.5)
    hm = _head_masks(qk)
    e2 = e2_ref[...]
    s_flat = s_ref[...]
    outs = []
    for ci in range(tl // GLA_CHUNK):
        sl = slice(ci * GLA_CHUNK, (ci + 1) * GLA_CHUNK)
        o_c, s_flat = _gla_chunk(q_all[sl], k_all[sl], v_all[sl], b_all[sl], la_hi[sl], la_lo[sl], s_flat, e2, hm)
        outs.append(o_c)
    s_ref[...] = s_flat
    acc = acc + out_proj(_gla_out(jnp.concatenate(outs, axis=0), gn_ref[...], r_all), 1)
    o_ref[...] = acc
    hn_ref[...] = _rmsnorm(acc, gf_ref[...]).astype(BF16)

    @pl.when(t == last)
    def _():
        na_ref[...] = abuf[a_head + tl - (SC_WIDTH - 1):a_head + tl, :]
        nc_ref[...] = cbuf[c_head + tl - (CF_WIDTH - 1):c_head + tl, :]

    abuf[0:a_head, :] = abuf[tl:tl + a_head, :]
    cbuf[0:c_head, :] = cbuf[tl:tl + c_head, :]


def _full(shape):
    nd = len(shape)
    return pl.BlockSpec(shape, lambda *_: (0,) * nd)


def _resident(a, l=None):
    if l is None:
        return pl.BlockSpec(a.shape, lambda *_: (0,) * a.ndim, pipeline_mode=pl.Buffered(1))
    return pl.BlockSpec((None,) + a.shape[1:], lambda *_: (l,) + (0,) * (a.ndim - 1), pipeline_mode=pl.Buffered(1))


def _front(x, gm, w_head, w_tail, w_lr, w_o, gf, l, batch, seq, wts, tl):
    d = x.shape[1]
    g = wts["gn"].shape[1]
    nt = seq // tl
    row_map = lambda b, t: (b * nt + t, 0)
    names = ("caw", "a2", "ab", "gn", "ccw", "ccb", "lcg", "lcb", "ldg", "ldb", "sgw", "sgb", "tri", "e2")
    consts = [wts[n] for n in names]
    return pl.pallas_call(
        _front_kernel,
        grid=(batch, nt),
        in_specs=[pl.BlockSpec((tl, d), row_map), _full(gm.shape),
                  _resident(w_head, l), _resident(w_tail, l), _resident(w_lr, l)]
                 + [_full(c.shape) for c in consts] + [_resident(w_o), _full(gf.shape)],
        out_specs=[
            pl.BlockSpec((tl, d), row_map),
            pl.BlockSpec((tl, d), row_map),
            pl.BlockSpec((None, SC_WIDTH - 1, g), lambda b, t: (b, 0, 0)),
            pl.BlockSpec((None, g // 2, g // N_HEADS), lambda b, t: (b, 0, 0)),
            pl.BlockSpec((None, CF_WIDTH - 1, g), lambda b, t: (b, 0, 0)),
            pl.BlockSpec((None, SG_CHUNK, g), lambda b, t: (b, 0, 0)),
        ],
        out_shape=[
            jax.ShapeDtypeStruct(x.shape, F32),
            jax.ShapeDtypeStruct(x.shape, BF16),
            jax.ShapeDtypeStruct((batch, SC_WIDTH - 1, g), F32),
            jax.ShapeDtypeStruct((batch, g // 2, g // N_HEADS), F32),
            jax.ShapeDtypeStruct((batch, CF_WIDTH - 1, g), F32),
            jax.ShapeDtypeStruct((batch, SG_CHUNK, g), F32),
        ],
        scratch_shapes=[pltpu.VMEM((8 + tl, g), F32), pltpu.VMEM((32 + tl, g), F32), pltpu.VMEM((8 + tl, g), F32)],
        compiler_params=pltpu.CompilerParams(
            dimension_semantics=("arbitrary", "arbitrary"), vmem_limit_bytes=VMEM_LIMIT),
    )(x, gm, w_head, w_tail, w_lr, *consts, w_o, gf)


def _mixer_sample_kernel(z_ref, zlr_ref, pa_ref, s_ref, pc_ref, caw_ref, a2_ref, ab_ref, gn_ref, ccw_ref, ccb_ref,
                         lcg_ref, lcb_ref, ldg_ref, ldb_ref, sgw0_ref, sgb0_ref,
                         y_ref, na_ref, so_ref, nc_ref, sv_ref, o_scr, *, layer):
    if layer is not None:
        for ll in range(pa_ref.shape[0]):
            if ll != layer:
                na_ref[ll] = pa_ref[ll]
                so_ref[ll] = s_ref[ll]
                nc_ref[ll] = pc_ref[ll]
        pa_ref, s_ref, pc_ref = pa_ref.at[layer], s_ref.at[layer], pc_ref.at[layer]
        na_ref, so_ref, nc_ref = na_ref.at[layer], so_ref.at[layer], nc_ref.at[layer]
    bt = z_ref.shape[0]
    g = gn_ref.shape[1]
    qk = g // 2
    dk = qk // N_HEADS
    dv = g // N_HEADS
    cols, _ = _cols(g)

    def zc(name):
        lo, hi = cols[name]
        return z_ref[:, lo:hi]

    xa = zc("a_c") * zc("a_x")
    conv = caw_ref[SC_WIDTH - 1:SC_WIDTH, :] * xa
    for kk in range(SC_WIDTH - 1):
        conv = conv + caw_ref[kk:kk + 1, :] * pa_ref[:, kk, :]
    y_ref[:, 0:g] = (zc("a_b") * conv).astype(BF16)
    for kk in range(1, SC_WIDTH - 1):
        na_ref[:, kk - 1, :] = pa_ref[:, kk, :]
    na_ref[:, SC_WIDTH - 2, :] = xa

    la = _gla_gate(zlr_ref[...], a2_ref, ab_ref)
    a_t = jnp.exp(la).T
    k_t = zc("k").T
    q_rows = zc("q") * (float(dk) ** -0.5)
    v_rows = zc("v")
    hm = _head_masks(qk)
    hm_rows = jnp.concatenate(hm + [jnp.zeros_like(hm[0])] * (8 - N_HEADS), axis=0)
    for b in range(bt):
        a_col = jnp.broadcast_to(a_t[:, b:b + 1], (qk, dv))
        k_col = jnp.broadcast_to(k_t[:, b:b + 1], (qk, dv))
        v_b = jnp.concatenate(
            [jnp.broadcast_to(v_rows[b:b + 1, h * dv:(h + 1) * dv], (dk, dv)) for h in range(N_HEADS)], axis=0)
        s_new = a_col * s_ref[b] + k_col * v_b
        so_ref[b] = s_new
        q_m = (q_rows[b:b + 1] * hm_rows).astype(BF16)
        o_b = _dot(q_m, s_new.astype(BF16))
        for h in range(N_HEADS):
            o_scr[b:b + 1, h * dv:(h + 1) * dv] = o_b[h:h + 1]
    y_ref[:, g:2 * g] = _gla_out(o_scr[...], gn_ref[...], zc("r")).astype(BF16)

    cin = zc("c_val") * _sigmoid(zc("c_gate"))
    acc = ccb_ref[...] + ccw_ref[CF_WIDTH - 1:CF_WIDTH, :] * cin
    for kk in range(CF_WIDTH - 1):
        acc = acc + ccw_ref[kk:kk + 1, :] * pc_ref[:, kk, :]
    y_ref[:, 2 * g:3 * g] = _silu(_layernorm(acc, lcg_ref[...], lcb_ref[...])).astype(BF16)
    nc_ref[:, 0:CF_WIDTH - 2, :] = pc_ref[:, 1:CF_WIDTH - 1, :]
    nc_ref[:, CF_WIDTH - 2, :] = cin

    du = _gelu_tanh(zc("d_u"))
    vd = _layernorm(_gelu_tanh(zc("d_v")), ldg_ref[...], ldb_ref[...])
    sv_ref[...] = vd
    y_ref[:, 3 * g:4 * g] = (du * (sgw0_ref[...] * vd + sgb0_ref[...])).astype(BF16)


def _mixer_sample(z, zlr, pa, s, pc, wts, l, bt, in_place):
    nb, g = z.shape[0], wts["gn"].shape[1]
    names = ("caw", "a2", "ab", "gn", "ccw", "ccb", "lcg", "lcb", "ldg", "ldb", "sgw0", "sgb0")
    consts = [wts[n] for n in names]
    rows = lambda i: (i, 0)
    if in_place:
        state_spec = lambda a: pl.BlockSpec((None, bt) + a.shape[2:], lambda i: (l, i, 0, 0))
    else:
        state_spec = lambda a: pl.BlockSpec((a.shape[0], bt) + a.shape[2:], lambda i: (0, i, 0, 0))
    return pl.pallas_call(
        functools.partial(_mixer_sample_kernel, layer=None if in_place else l),
        grid=(nb // bt,),
        in_specs=[pl.BlockSpec((bt, z.shape[1]), rows), pl.BlockSpec((bt, LANES), rows),
                  state_spec(pa), state_spec(s), state_spec(pc)] + [_full(c.shape) for c in consts],
        out_specs=[pl.BlockSpec((bt, 4 * g), rows), state_spec(pa), state_spec(s), state_spec(pc),
                   pl.BlockSpec((bt, g), rows)],
        out_shape=[
            jax.ShapeDtypeStruct((nb, 4 * g), BF16),
            jax.ShapeDtypeStruct(pa.shape, F32),
            jax.ShapeDtypeStruct(s.shape, F32),
            jax.ShapeDtypeStruct(pc.shape, F32),
            jax.ShapeDtypeStruct((nb, g), F32),
        ],
        input_output_aliases={2: 1, 3: 2, 4: 3} if in_place else {},
        scratch_shapes=[pltpu.VMEM((bt, g), F32)],
        compiler_params=pltpu.CompilerParams(
            dimension_semantics=("arbitrary",), vmem_limit_bytes=VMEM_LIMIT),
    )(z, zlr, pa, s, pc, *consts)


def _largest_tile(m, cap, mult):
    t = min(m, cap)
    while m % t or t % mult:
        t -= mult
    return t


def _layer_weights(l, g, conv_a_w, gla_a2, gla_a_bias, gla_norm, conv_c_w, conv_c_b,
                   ln_c_g, ln_c_b, ln_d_g, ln_d_b, sg_w, sg_b):
    dv = g // N_HEADS
    row = lambda v: v.reshape(1, -1)
    return {
        "caw": conv_a_w[l],
        "a2": jnp.pad(gla_a2[l], ((0, LANES - GLA_LOWRANK), (0, 0))).astype(BF16),
        "ab": row(gla_a_bias[l]), "gn": row(gla_norm[l]),
        "ccw": conv_c_w[l], "ccb": row(conv_c_b[l]),
        "lcg": row(ln_c_g[l]), "lcb": row(ln_c_b[l]), "ldg": row(ln_d_g[l]), "ldb": row(ln_d_b[l]),
        "sgw": sg_w[l],
        "sgb": jnp.repeat(sg_b[l].T, dv, axis=1),
        "sgw0": row(jnp.repeat(sg_w[l][:, 0, 0], dv)),
        "sgb0": row(jnp.repeat(sg_b[l][:, 0], dv)),
    }


def _gla_constants(g, tl):
    qk, dv = g // 2, g // N_HEADS
    r = jnp.arange(tl)
    tri = ((r[:, None] >= r[None, :]) & (r[:, None] // GLA_CHUNK == r[None, :] // GLA_CHUNK)).astype(BF16)
    e2 = (jnp.arange(qk)[:, None] // (qk // N_HEADS) == jnp.arange(g)[None, :] // dv).astype(BF16)
    return tri, e2


def kernel(x_prompt, x_sample, state_conv_a, state_gla, state_conv_c, norm_mix, w_in, conv_a_w, gla_a2, gla_a_bias,
           gla_norm, conv_c_w, conv_c_b, ln_c_g, ln_c_b, ln_d_g, ln_d_b, sg_w, sg_b, w_o, norm_ffn, w_gate, w_up,
           w_down, norm_final):
    batch, seq, d = x_prompt.shape
    nb, dec_seq, _ = x_sample.shape
    depth = w_in.shape[0]
    g = d // 4
    qk, dv = g // 2, g // N_HEADS
    assert dec_seq == 1 and seq % SG_CHUNK == 0 and g % (N_HEADS * LANES) == 0
    dff = w_gate.shape[2]
    n_main = w_in.shape[2] - GLA_LOWRANK
    tl = _largest_tile(seq, FRONT_ROWS, SG_CHUNK)
    tri, e2 = _gla_constants(g, tl)

    lr0 = 3 * g + 2 * qk + 2 * g
    w_head = w_in[:, :, :lr0].astype(BF16)
    w_tail = w_in[:, :, lr0 + GLA_LOWRANK:].astype(BF16)
    w_lr = jnp.pad(w_in[:, :, lr0:lr0 + GLA_LOWRANK], ((0, 0), (0, 0), (0, LANES - GLA_LOWRANK))).astype(BF16)

    mp = batch * seq
    tm_p = _largest_tile(mp, 1024, 16)
    tm_s = nb
    tn_in = _largest_tile(math.gcd(n_main, lr0), 1024, 256)
    tf_cast = _largest_tile(dff, 256, 256)
    tf = _largest_tile(dff, 512, 256)

    xp = x_prompt.reshape(mp, d)
    xs = x_sample.reshape(nb, d)
    st_a, st_c = state_conv_a, state_conv_c
    st_s = state_gla.reshape(depth, nb, qk, dv)
    outs_p = {k: [] for k in ("a", "s", "c", "v")}
    sv_s = []
    for l in range(depth):
        wts = _layer_weights(l, g, conv_a_w, gla_a2, gla_a_bias, gla_norm, conv_c_w, conv_c_b,
                             ln_c_g, ln_c_b, ln_d_g, ln_d_b, sg_w, sg_b)
        wts["tri"], wts["e2"] = tri, e2
        gm, gf = norm_mix[l].reshape(1, d), norm_ffn[l].reshape(1, d)
        gfin = norm_final.reshape(1, d)
        final = l == depth - 1

        z, zlr = _norm_proj(xs, gm, w_head, w_tail, w_lr, l, tm_s, tn_in)
        y, st_a, st_s, st_c, sv = _mixer_sample(z, zlr, st_a, st_s, st_c, wts, l, 32 if l else 16, l > 0)
        xs, hs, w_ob = _proj_res(y, w_o, xs, gf, l, tn_in)

        xp, hn, na, s_new, nc, svp = _front(xp, gm, w_head, w_tail, w_lr, w_ob, gf, l, batch, seq, wts, tl)
        xp, xs, wg_b, wu_b, wd_b = _ffn(hn, xp, w_gate, w_up, w_down, gfin, l, tm_p, tf_cast, 0, 1, final,
                                        rider=(hs, xs))
        if mp > tm_p:
            xp, = _ffn(hn, xp, wg_b, wu_b, wd_b, gfin, None, tm_p, tf, 1, mp // tm_p - 1, final)
        outs_p["a"].append(na)
        outs_p["s"].append(s_new.reshape(batch, N_HEADS, qk // N_HEADS, dv))
        outs_p["c"].append(nc)
        outs_p["v"].append(svp)

        sv_s.append(sv.reshape(nb, 1, g))

    st = jnp.stack
    return (xp.reshape(batch, seq, d), xs.reshape(nb, 1, d),
            st(outs_p["a"]), st_a, st(outs_p["s"]), st_s.reshape(depth, nb, N_HEADS, qk // N_HEADS, dv),
            st(outs_p["c"]), st_c, st(outs_p["v"]), st(sv_s))
```

```python
import functools
import math

import jax
import jax.numpy as jnp
from jax import lax
from jax.experimental import pallas as pl
from jax.experimental.pallas import tpu as pltpu

F32 = jnp.float32
BF16 = jnp.bfloat16
EPS = 1e-6

N_HEADS = 4
GLA_LOWRANK = 16
GLA_TAU = 16.0
SC_WIDTH = 3
CF_WIDTH = 31
SG_CHUNK = 128
GLA_CHUNK = 64
GLA_SUB = 16
FRONT_ROWS = 256
PROJ_SLAB = 256
LANES = 128
SUBLANES = 8
MXU_DIM = 256
BF16_ROWS = 16
FFN_ROWS = 1024
FFN_COLS_BF16 = 512
FFN_COLS_F32 = 256
SAMPLE_SEQS = 32
VMEM_LIMIT = 60 * 1024 * 1024


def _dot(a, b):
    return jnp.dot(a, b, preferred_element_type=F32)


def _dot_nt(a, b):
    return lax.dot_general(a, b, (((1,), (1,)), ((), ())), preferred_element_type=F32)


def _dot_tn(a, b):
    return lax.dot_general(a, b, (((0,), (0,)), ((), ())), preferred_element_type=F32)


def _split_bf16(x):
    hi = x.astype(BF16)
    lo = (x - hi.astype(F32)).astype(BF16)
    return hi, lo


def _rmsnorm(x, g):
    return x * lax.rsqrt(jnp.mean(x * x, axis=-1, keepdims=True) + EPS) * g


def _layernorm(x, g, b):
    mu = jnp.mean(x, axis=-1, keepdims=True)
    xc = x - mu
    return xc * lax.rsqrt(jnp.mean(xc * xc, axis=-1, keepdims=True) + EPS) * g + b


def _sigmoid(x):
    return 1.0 / (1.0 + jnp.exp(-x))


def _silu(x):
    return x * _sigmoid(x)


def _gelu_tanh(x):
    c = math.sqrt(2.0 / math.pi)
    return 0.5 * x * (1.0 + jnp.tanh(c * (x + 0.044715 * (x * x * x))))


def _log_sigmoid(x):
    return jnp.minimum(x, 0.0) - jnp.log(1.0 + jnp.exp(-jnp.abs(x)))


def _norm_proj_kernel(x_ref, g_ref, wh_ref, wt_ref, wlr_ref, z_ref, zlr_ref, h_ref, *, n_head):
    j = pl.program_id(1)

    @pl.when(j == 0)
    def _():
        h = _rmsnorm(x_ref[...], g_ref[...]).astype(BF16)
        h_ref[...] = h
        zlr_ref[...] = _dot(h, wlr_ref[...])

    @pl.when(j < n_head)
    def _():
        z_ref[...] = _dot(h_ref[...], wh_ref[...])

    @pl.when(j >= n_head)
    def _():
        z_ref[...] = _dot(h_ref[...], wt_ref[...])


def _norm_proj(x, g, w_head, w_tail, w_lr, l, tm, tn):
    m, d = x.shape
    n_head, n_tail = w_head.shape[2] // tn, w_tail.shape[2] // tn
    return pl.pallas_call(
        functools.partial(_norm_proj_kernel, n_head=n_head),
        grid=(m // tm, n_head + n_tail),
        in_specs=[
            pl.BlockSpec((tm, d), lambda i, j: (i, 0)),
            pl.BlockSpec((1, d), lambda i, j: (0, 0)),
            pl.BlockSpec((None, d, tn), lambda i, j: (l, 0, jnp.minimum(j, n_head - 1))),
            pl.BlockSpec((None, d, tn), lambda i, j: (l, 0, jnp.maximum(j - n_head, 0))),
            pl.BlockSpec((None, d, LANES), lambda i, j: (l, 0, 0)),
        ],
        out_specs=[
            pl.BlockSpec((tm, tn), lambda i, j: (i, j)),
            pl.BlockSpec((tm, LANES), lambda i, j: (i, 0)),
        ],
        out_shape=[jax.ShapeDtypeStruct((m, (n_head + n_tail) * tn), F32), jax.ShapeDtypeStruct((m, LANES), F32)],
        scratch_shapes=[pltpu.VMEM((tm, d), BF16)],
        compiler_params=pltpu.CompilerParams(
            dimension_semantics=("arbitrary", "arbitrary"), vmem_limit_bytes=VMEM_LIMIT),
    )(x, g, w_head, w_tail, w_lr)


def _proj_res_kernel(y_ref, w_ref, x_ref, g_ref, o_ref, h_ref, wb_ref):
    k = pl.program_id(0)

    @pl.when(k == 0)
    def _():
        o_ref[...] = x_ref[...]

    w = w_ref[...].astype(BF16)
    wb_ref[...] = w
    o_ref[...] += _dot(y_ref[...], w)

    @pl.when(k == pl.num_programs(0) - 1)
    def _():
        h_ref[...] = _rmsnorm(o_ref[...], g_ref[...]).astype(BF16)


def _proj_res(y, w, x, g, l, tk):
    m, k = y.shape
    n = w.shape[2]
    return pl.pallas_call(
        _proj_res_kernel,
        grid=(k // tk,),
        in_specs=[
            pl.BlockSpec((m, tk), lambda i: (0, i)),
            pl.BlockSpec((None, tk, n), lambda i: (l, i, 0)),
            pl.BlockSpec((m, n), lambda i: (0, 0)),
            pl.BlockSpec((1, n), lambda i: (0, 0)),
        ],
        out_specs=[pl.BlockSpec((m, n), lambda i: (0, 0)), pl.BlockSpec((m, n), lambda i: (0, 0)),
                   pl.BlockSpec((tk, n), lambda i: (i, 0))],
        out_shape=[jax.ShapeDtypeStruct((m, n), F32), jax.ShapeDtypeStruct((m, n), BF16),
                   jax.ShapeDtypeStruct((k, n), BF16)],
        compiler_params=pltpu.CompilerParams(
            dimension_semantics=("arbitrary",), vmem_limit_bytes=VMEM_LIMIT),
    )(y, w, x, g)


def _ffn_kernel(*refs, n_groups, final_norm):
    hx = [(refs[2 * k], refs[2 * k + 1]) for k in range(n_groups)]
    wg_ref, wu_ref, wd_ref, g_ref = refs[2 * n_groups:2 * n_groups + 4]
    o_refs = refs[2 * n_groups + 4:3 * n_groups + 4]
    cast_refs = refs[3 * n_groups + 4:]
    j = pl.program_id(1)

    @pl.when(j == 0)
    def _():
        for (_, x_ref), o_ref in zip(hx, o_refs):
            o_ref[...] = x_ref[...]

    wg, wu, wd = wg_ref[...].astype(BF16), wu_ref[...].astype(BF16), wd_ref[...].astype(BF16)
    for ref, w in zip(cast_refs, (wg, wu, wd)):
        ref[...] = w
    for (h_ref, _), o_ref in zip(hx, o_refs):
        h = h_ref[...]
        o_ref[...] += _dot((_silu(_dot(h, wg)) * _dot(h, wu)).astype(BF16), wd)

    if final_norm:
        @pl.when(j == pl.num_programs(1) - 1)
        def _():
            for o_ref in o_refs:
                o_ref[...] = _rmsnorm(o_ref[...], g_ref[...])


def _ffn(h, x, wg, wu, wd, g, l, tm, tf, tile0, n_tiles, final_norm, rider=None):
    m, d = x.shape
    f = wg.shape[-1]
    rows = lambda i, j: (tile0 + i, 0)
    row_mode = pl.Buffered(1) if n_tiles == 1 else None
    row_specs = [pl.BlockSpec((tm, d), rows, pipeline_mode=row_mode)] * 2
    out_specs = [pl.BlockSpec((tm, d), rows)]
    out_shape = [jax.ShapeDtypeStruct((m, d), F32)]
    operands = [h, x]
    aliases = {1: 0}
    if rider is not None:
        assert n_tiles == 1
        whole = pl.BlockSpec(rider[1].shape, lambda i, j: (0, 0))
        row_specs += [whole, whole]
        out_specs.append(whole)
        out_shape.append(jax.ShapeDtypeStruct(rider[1].shape, F32))
        operands += list(rider)
        aliases[3] = 1
    if l is None:
        w_specs = [pl.BlockSpec((d, tf), lambda i, j: (0, j)), pl.BlockSpec((d, tf), lambda i, j: (0, j)),
                   pl.BlockSpec((tf, d), lambda i, j: (j, 0))]
    else:
        w_specs = [pl.BlockSpec((None, d, tf), lambda i, j: (l, 0, j)),
                   pl.BlockSpec((None, d, tf), lambda i, j: (l, 0, j)),
                   pl.BlockSpec((None, tf, d), lambda i, j: (l, j, 0))]
        out_specs += [pl.BlockSpec((d, tf), lambda i, j: (0, j)), pl.BlockSpec((d, tf), lambda i, j: (0, j)),
                      pl.BlockSpec((tf, d), lambda i, j: (j, 0))]
        out_shape += [jax.ShapeDtypeStruct((d, f), BF16), jax.ShapeDtypeStruct((d, f), BF16),
                      jax.ShapeDtypeStruct((f, d), BF16)]
    return pl.pallas_call(
        functools.partial(_ffn_kernel, n_groups=len(operands) // 2, final_norm=final_norm),
        grid=(n_tiles, f // tf),
        in_specs=row_specs + w_specs + [pl.BlockSpec((1, d), lambda i, j: (0, 0))],
        out_specs=out_specs,
        out_shape=out_shape,
        input_output_aliases=aliases,
        compiler_params=pltpu.CompilerParams(
            dimension_semantics=("arbitrary", "arbitrary"), vmem_limit_bytes=VMEM_LIMIT),
    )(*operands, wg, wu, wd, g)


def _gla_gate(zlr, a2_ref, ab_ref):
    xg = _dot(zlr.astype(BF16), a2_ref[...]) + ab_ref[...]
    return _log_sigmoid(xg) * (1.0 / GLA_TAU)


def _head_masks(qk):
    lane_head = lax.broadcasted_iota(jnp.int32, (1, qk), 1) // (qk // N_HEADS)
    return [(lane_head == h).astype(F32) for h in range(N_HEADS)]


def _gla_chunk(qc, kc, vc, bc, la_hi, la_lo, s_flat, e2, hm):
    c, qk = qc.shape
    dv = vc.shape[1] // N_HEADS
    dk = qk // N_HEADS
    nb = c // GLA_SUB
    vcb = vc.astype(BF16)
    sub_row = lax.broadcasted_iota(jnp.int32, (GLA_SUB, 1), 0)
    col_j = lax.broadcasted_iota(jnp.int32, (1, c), 1)

    sg = SUBLANES
    ts = []
    for blk in range(nb):
        r0 = blk * GLA_SUB
        q_b, k_b, b_b = qc[r0:r0 + GLA_SUB], kc[r0:r0 + GLA_SUB], bc[r0:r0 + GLA_SUB]
        for j in range(GLA_SUB):
            i0 = j // sg * sg
            dec = jnp.exp2(jnp.where(sub_row[i0:] >= j, b_b[i0:] - b_b[j:j + 1], -jnp.inf))
            ts.append(q_b[i0:] * k_b[j:j + 1] * dec)
    r = _dot(jnp.concatenate(ts, axis=0).astype(BF16), e2)
    o_diag = []
    off = 0
    for blk in range(nb):
        v_b = vc[blk * GLA_SUB:(blk + 1) * GLA_SUB]
        acc = [None] * (GLA_SUB // sg)
        for j in range(GLA_SUB):
            for i0 in range(j // sg * sg, GLA_SUB, sg):
                term = r[off:off + sg] * v_b[j:j + 1]
                acc[i0 // sg] = term if acc[i0 // sg] is None else acc[i0 // sg] + term
                off += sg
        o_diag.extend(acc)
    o = jnp.concatenate(o_diag, axis=0)

    bref = [None] + [bc[blk * GLA_SUB - 1:blk * GLA_SUB] for blk in range(1, nb)]
    bref_rows = jnp.concatenate(
        [jnp.zeros((GLA_SUB, qk), F32)] + [jnp.broadcast_to(bref[blk], (GLA_SUB, qk)) for blk in range(1, nb)],
        axis=0)
    qt = qc * jnp.exp2(bc - bref_rows)
    att_blk = [None]
    for blk in range(1, nb):
        kt = (kc * jnp.exp2(jnp.minimum(bref[blk] - bc, 0.0))).astype(BF16)
        q_b = qt[blk * GLA_SUB:(blk + 1) * GLA_SUB]
        qm = jnp.concatenate([q_b * hm[h] for h in range(N_HEADS)], axis=0).astype(BF16)
        a = _dot_nt(qm, kt)
        att_blk.append(jnp.where(col_j < blk * GLA_SUB, a, 0.0))
    o_off = []
    for h in range(N_HEADS):
        att = jnp.concatenate(
            [jnp.zeros((GLA_SUB, c), F32)] + [att_blk[blk][h * GLA_SUB:(h + 1) * GLA_SUB] for blk in range(1, nb)],
            axis=0)
        o_off.append(_dot(att.astype(BF16), vcb[:, h * dv:(h + 1) * dv]))
    o = o + jnp.concatenate(o_off, axis=1)

    qe = qc * jnp.exp2(bc)
    qe_m = jnp.concatenate([qe * hm[h] for h in range(N_HEADS)], axis=0).astype(BF16)
    o_int = _dot(qe_m, s_flat.astype(BF16))
    o = o + jnp.concatenate([o_int[h * c:(h + 1) * c] for h in range(N_HEADS)], axis=1)

    kk = (kc * jnp.exp2(bc[c - 1:c] - bc)).astype(BF16)
    kv = _dot_tn(kk, vcb)
    kv_d = jnp.concatenate([kv[h * dk:(h + 1) * dk, h * dv:(h + 1) * dv] for h in range(N_HEADS)], axis=0)
    ones = jnp.ones((c, dv), BF16)
    decay = jnp.exp(_dot_tn(la_hi, ones) + _dot_tn(la_lo, ones))
    return o, decay * s_flat + kv_d


def _gla_out(o, gn, r):
    dv = o.shape[1] // N_HEADS
    parts = []
    for h in range(N_HEADS):
        oh = o[:, h * dv:(h + 1) * dv]
        parts.append(oh * lax.rsqrt(jnp.mean(oh * oh, axis=-1, keepdims=True) + EPS))
    return jnp.concatenate(parts, axis=1) * gn * _silu(r)


def _cols(g):
    qk = g // 2
    c = {}
    off = 0
    for name, width in (("a_x", g), ("a_b", g), ("a_c", g), ("q", qk), ("k", qk), ("v", g), ("r", g),
                        ("c_val", g), ("c_gate", g), ("d_u", g), ("d_v", g)):
        c[name] = (off, off + width)
        off += width
    return c, off


def _conv_taps(ccw_ref, cbuf, zbuf, tl, head, between):
    sub = SUBLANES
    base = head - (CF_WIDTH - 1)
    cols = []
    for c0 in range(0, cbuf.shape[1], LANES):
        lanes = slice(c0, c0 + LANES)
        y = None
        for r in range(sub):
            taps = [k for k in range(CF_WIDTH) if (base + k) % sub == r]
            rows = tl if r == 0 else tl + sub
            part = None
            for k in taps:
                off = base + k - r
                term = ccw_ref[k:k + 1, lanes] * cbuf[off:off + rows, lanes]
                part = term if part is None else part + term
            if r == 0:
                shifted = part
            else:
                zbuf[0:rows, lanes] = part
                shifted = zbuf[r:r + tl, lanes]
            y = shifted if y is None else y + shifted
            between()
        cols.append(y)
    return jnp.concatenate(cols, axis=1)


def _front_kernel(x_ref, gm_ref, wh_ref, wt_ref, wlr_ref, caw_ref, a2_ref, ab_ref, gn_ref, ccw_ref, ccb_ref,
                  lcg_ref, lcb_ref, ldg_ref, ldb_ref, sgw_ref, sgb_ref, tri_ref, e2_ref, wo_ref, gf_ref,
                  o_ref, hn_ref, na_ref, s_ref, nc_ref, sv_ref, abuf, cbuf, zbuf):
    t = pl.program_id(1)
    last = pl.num_programs(1) - 1
    tl = x_ref.shape[0]
    g = gn_ref.shape[1]
    qk = g // 2
    dv = g // N_HEADS
    cols, _ = _cols(g)
    a_head, c_head = SUBLANES, 4 * SUBLANES

    @pl.when(t == 0)
    def _():
        abuf[0:a_head, :] = jnp.zeros((a_head, g), F32)
        cbuf[0:c_head, :] = jnp.zeros((c_head, g), F32)
        s_ref[...] = jnp.zeros(s_ref.shape, F32)

    x = x_ref[...]
    h = _rmsnorm(x, gm_ref[...]).astype(BF16)

    slab = PROJ_SLAB
    order = [c0 for n in ("c_val", "c_gate", "a_x", "a_b", "a_c", "d_u", "d_v", "q", "k", "v", "r")
             for c0 in range(cols[n][0], cols[n][1], slab) if c0 % slab == 0]
    pending = list(dict.fromkeys(order))
    slabs = {}

    n_head = wh_ref.shape[1]

    def pump():
        if pending:
            c0 = pending.pop(0)
            w = wh_ref[:, c0:c0 + slab] if c0 < n_head else wt_ref[:, c0 - n_head:c0 - n_head + slab]
            slabs[c0] = _dot(h, w)

    def proj(*names):
        out = []
        for n in names:
            lo, hi = cols[n]
            parts = []
            for c0 in range(lo - lo % slab, hi, slab):
                while c0 not in slabs:
                    pump()
                s_lo, s_hi = max(lo, c0) - c0, min(hi, c0 + slab) - c0
                parts.append(slabs[c0][:, s_lo:s_hi])
            out.append(parts[0] if len(parts) == 1 else jnp.concatenate(parts, axis=1))
        return out

    def out_proj(y, idx):
        return _dot(y.astype(BF16), wo_ref[idx * g:(idx + 1) * g, :])

    c_val, c_gate = proj("c_val", "c_gate")
    cin = c_val * _sigmoid(c_gate)
    cbuf[c_head:c_head + tl, :] = cin
    conv_c = _conv_taps(ccw_ref, cbuf, zbuf, tl, c_head, pump) + ccb_ref[...]
    acc = x + out_proj(_silu(_layernorm(conv_c, lcg_ref[...], lcb_ref[...])), 2)

    a_x, a_b, a_c = proj("a_x", "a_b", "a_c")
    xa = a_c * a_x
    abuf[a_head:a_head + tl, :] = xa
    conv = caw_ref[SC_WIDTH - 1:SC_WIDTH, :] * xa
    for kk in range(SC_WIDTH - 1):
        sh = SC_WIDTH - 1 - kk
        conv = conv + caw_ref[kk:kk + 1, :] * abuf[a_head - sh:a_head - sh + tl, :]
    acc = acc + out_proj(a_b * conv, 0)

    d_u, d_v = proj("d_u", "d_v")
    du = _gelu_tanh(d_u)
    vd = _layernorm(_gelu_tanh(d_v), ldg_ref[...], ldb_ref[...])
    sv_ref[...] = vd[tl - SG_CHUNK:tl]
    vdb = vd.astype(BF16)
    row = lax.broadcasted_iota(jnp.int32, (SG_CHUNK, SG_CHUNK), 0)
    col = lax.broadcasted_iota(jnp.int32, (SG_CHUNK, SG_CHUNK), 1)
    ws = [jnp.where(row >= col, sgw_ref[hh], 0.0).astype(BF16) for hh in range(N_HEADS)]
    sv_rows = []
    for ci in range(tl // SG_CHUNK):
        sl = slice(ci * SG_CHUNK, (ci + 1) * SG_CHUNK)
        sv_rows.append(jnp.concatenate(
            [_dot(ws[hh], vdb[sl, hh * dv:(hh + 1) * dv]) for hh in range(N_HEADS)], axis=1) + sgb_ref[...])
    acc = acc + out_proj(du * jnp.concatenate(sv_rows, axis=0), 3)

    q_all, k_all, v_all, r_all = proj("q", "k", "v", "r")
    la = _gla_gate(_dot(h, wlr_ref[...]), a2_ref, ab_ref)
    la_hi, la_lo = _split_bf16(la)
    tri = tri_ref[...]
    b_all = (_dot(tri, la_hi) + _dot(tri, la_lo)) * math.log2(math.e)
    q_all = q_all * (float(qk // N_HEADS) ** -0.5)
    hm = _head_masks(qk)
    e2 = e2_ref[...]
    s_flat = s_ref[...]
    outs = []
    for ci in range(tl // GLA_CHUNK):
        sl = slice(ci * GLA_CHUNK, (ci + 1) * GLA_CHUNK)
        o_c, s_flat = _gla_chunk(q_all[sl], k_all[sl], v_all[sl], b_all[sl], la_hi[sl], la_lo[sl], s_flat, e2, hm)
        outs.append(o_c)
    s_ref[...] = s_flat
    acc = acc + out_proj(_gla_out(jnp.concatenate(outs, axis=0), gn_ref[...], r_all), 1)
    o_ref[...] = acc
    hn_ref[...] = _rmsnorm(acc, gf_ref[...]).astype(BF16)

    @pl.when(t == last)
    def _():
        na_ref[...] = abuf[a_head + tl - (SC_WIDTH - 1):a_head + tl, :]
        nc_ref[...] = cbuf[c_head + tl - (CF_WIDTH - 1):c_head + tl, :]

    abuf[0:a_head, :] = abuf[tl:tl + a_head, :]
    cbuf[0:c_head, :] = cbuf[tl:tl + c_head, :]


def _full(shape):
    nd = len(shape)
    return pl.BlockSpec(shape, lambda *_: (0,) * nd)


def _resident(a, l=None):
    if l is None:
        return pl.BlockSpec(a.shape, lambda *_: (0,) * a.ndim, pipeline_mode=pl.Buffered(1))
    return pl.BlockSpec((None,) + a.shape[1:], lambda *_: (l,) + (0,) * (a.ndim - 1), pipeline_mode=pl.Buffered(1))


def _front(x, gm, w_head, w_tail, w_lr, w_o, gf, l, batch, seq, wts, tl):
    d = x.shape[1]
    g = wts["gn"].shape[1]
    nt = seq // tl
    row_map = lambda b, t: (b * nt + t, 0)
    names = ("caw", "a2", "ab", "gn", "ccw", "ccb", "lcg", "lcb", "ldg", "ldb", "sgw", "sgb", "tri", "e2")
    consts = [wts[n] for n in names]
    return pl.pallas_call(
        _front_kernel,
        grid=(batch, nt),
        in_specs=[pl.BlockSpec((tl, d), row_map), _full(gm.shape),
                  _resident(w_head, l), _resident(w_tail, l), _resident(w_lr, l)]
                 + [_full(c.shape) for c in consts] + [_resident(w_o), _full(gf.shape)],
        out_specs=[
            pl.BlockSpec((tl, d), row_map),
            pl.BlockSpec((tl, d), row_map),
            pl.BlockSpec((None, SC_WIDTH - 1, g), lambda b, t: (b, 0, 0)),
            pl.BlockSpec((None, g // 2, g // N_HEADS), lambda b, t: (b, 0, 0)),
            pl.BlockSpec((None, CF_WIDTH - 1, g), lambda b, t: (b, 0, 0)),
            pl.BlockSpec((None, SG_CHUNK, g), lambda b, t: (b, 0, 0)),
        ],
        out_shape=[
            jax.ShapeDtypeStruct(x.shape, F32),
            jax.ShapeDtypeStruct(x.shape, BF16),
            jax.ShapeDtypeStruct((batch, SC_WIDTH - 1, g), F32),
            jax.ShapeDtypeStruct((batch, g // 2, g // N_HEADS), F32),
            jax.ShapeDtypeStruct((batch, CF_WIDTH - 1, g), F32),
            jax.ShapeDtypeStruct((batch, SG_CHUNK, g), F32),
        ],
        scratch_shapes=[pltpu.VMEM((SUBLANES + tl, g), F32), pltpu.VMEM((4 * SUBLANES + tl, g), F32),
                        pltpu.VMEM((SUBLANES + tl, g), F32)],
        compiler_params=pltpu.CompilerParams(
            dimension_semantics=("arbitrary", "arbitrary"), vmem_limit_bytes=VMEM_LIMIT),
    )(x, gm, w_head, w_tail, w_lr, *consts, w_o, gf)


def _mixer_sample_kernel(z_ref, zlr_ref, pa_ref, s_ref, pc_ref, caw_ref, a2_ref, ab_ref, gn_ref, ccw_ref, ccb_ref,
                         lcg_ref, lcb_ref, ldg_ref, ldb_ref, sgw0_ref, sgb0_ref,
                         y_ref, na_ref, so_ref, nc_ref, sv_ref, o_scr, *, layer):
    if layer is not None:
        for ll in range(pa_ref.shape[0]):
            if ll != layer:
                na_ref[ll] = pa_ref[ll]
                so_ref[ll] = s_ref[ll]
                nc_ref[ll] = pc_ref[ll]
        pa_ref, s_ref, pc_ref = pa_ref.at[layer], s_ref.at[layer], pc_ref.at[layer]
        na_ref, so_ref, nc_ref = na_ref.at[layer], so_ref.at[layer], nc_ref.at[layer]
    bt = z_ref.shape[0]
    g = gn_ref.shape[1]
    qk = g // 2
    dk = qk // N_HEADS
    dv = g // N_HEADS
    cols, _ = _cols(g)

    def zc(name):
        lo, hi = cols[name]
        return z_ref[:, lo:hi]

    xa = zc("a_c") * zc("a_x")
    conv = caw_ref[SC_WIDTH - 1:SC_WIDTH, :] * xa
    for kk in range(SC_WIDTH - 1):
        conv = conv + caw_ref[kk:kk + 1, :] * pa_ref[:, kk, :]
    y_ref[:, 0:g] = (zc("a_b") * conv).astype(BF16)
    for kk in range(1, SC_WIDTH - 1):
        na_ref[:, kk - 1, :] = pa_ref[:, kk, :]
    na_ref[:, SC_WIDTH - 2, :] = xa

    la = _gla_gate(zlr_ref[...], a2_ref, ab_ref)
    a_t = jnp.exp(la).T
    k_t = zc("k").T
    q_rows = zc("q") * (float(dk) ** -0.5)
    v_rows = zc("v")
    hm = _head_masks(qk)
    hm_rows = jnp.concatenate(hm + [jnp.zeros_like(hm[0])] * (SUBLANES - N_HEADS), axis=0)
    for b in range(bt):
        a_col = jnp.broadcast_to(a_t[:, b:b + 1], (qk, dv))
        k_col = jnp.broadcast_to(k_t[:, b:b + 1], (qk, dv))
        v_b = jnp.concatenate(
            [jnp.broadcast_to(v_rows[b:b + 1, h * dv:(h + 1) * dv], (dk, dv)) for h in range(N_HEADS)], axis=0)
        s_new = a_col * s_ref[b] + k_col * v_b
        so_ref[b] = s_new
        q_m = (q_rows[b:b + 1] * hm_rows).astype(BF16)
        o_b = _dot(q_m, s_new.astype(BF16))
        for h in range(N_HEADS):
            o_scr[b:b + 1, h * dv:(h + 1) * dv] = o_b[h:h + 1]
    y_ref[:, g:2 * g] = _gla_out(o_scr[...], gn_ref[...], zc("r")).astype(BF16)

    cin = zc("c_val") * _sigmoid(zc("c_gate"))
    acc = ccb_ref[...] + ccw_ref[CF_WIDTH - 1:CF_WIDTH, :] * cin
    for kk in range(CF_WIDTH - 1):
        acc = acc + ccw_ref[kk:kk + 1, :] * pc_ref[:, kk, :]
    y_ref[:, 2 * g:3 * g] = _silu(_layernorm(acc, lcg_ref[...], lcb_ref[...])).astype(BF16)
    nc_ref[:, 0:CF_WIDTH - 2, :] = pc_ref[:, 1:CF_WIDTH - 1, :]
    nc_ref[:, CF_WIDTH - 2, :] = cin

    du = _gelu_tanh(zc("d_u"))
    vd = _layernorm(_gelu_tanh(zc("d_v")), ldg_ref[...], ldb_ref[...])
    sv_ref[...] = vd
    y_ref[:, 3 * g:4 * g] = (du * (sgw0_ref[...] * vd + sgb0_ref[...])).astype(BF16)


def _mixer_sample(z, zlr, pa, s, pc, wts, l, bt, in_place):
    nb, g = z.shape[0], wts["gn"].shape[1]
    names = ("caw", "a2", "ab", "gn", "ccw", "ccb", "lcg", "lcb", "ldg", "ldb", "sgw0", "sgb0")
    consts = [wts[n] for n in names]
    rows = lambda i: (i, 0)
    if in_place:
        state_spec = lambda a: pl.BlockSpec((None, bt) + a.shape[2:], lambda i: (l, i, 0, 0))
    else:
        state_spec = lambda a: pl.BlockSpec((a.shape[0], bt) + a.shape[2:], lambda i: (0, i, 0, 0))
    return pl.pallas_call(
        functools.partial(_mixer_sample_kernel, layer=None if in_place else l),
        grid=(nb // bt,),
        in_specs=[pl.BlockSpec((bt, z.shape[1]), rows), pl.BlockSpec((bt, LANES), rows),
                  state_spec(pa), state_spec(s), state_spec(pc)] + [_full(c.shape) for c in consts],
        out_specs=[pl.BlockSpec((bt, 4 * g), rows), state_spec(pa), state_spec(s), state_spec(pc),
                   pl.BlockSpec((bt, g), rows)],
        out_shape=[
            jax.ShapeDtypeStruct((nb, 4 * g), BF16),
            jax.ShapeDtypeStruct(pa.shape, F32),
            jax.ShapeDtypeStruct(s.shape, F32),
            jax.ShapeDtypeStruct(pc.shape, F32),
            jax.ShapeDtypeStruct((nb, g), F32),
        ],
        input_output_aliases={2: 1, 3: 2, 4: 3} if in_place else {},
        scratch_shapes=[pltpu.VMEM((bt, g), F32)],
        compiler_params=pltpu.CompilerParams(
            dimension_semantics=("arbitrary",), vmem_limit_bytes=VMEM_LIMIT),
    )(z, zlr, pa, s, pc, *consts)


def _largest_tile(m, cap, mult):
    t = min(m, cap)
    while m % t or t % mult:
        t -= mult
    return t


def _layer_weights(l, g, conv_a_w, gla_a2, gla_a_bias, gla_norm, conv_c_w, conv_c_b,
                   ln_c_g, ln_c_b, ln_d_g, ln_d_b, sg_w, sg_b):
    dv = g // N_HEADS
    row = lambda v: v.reshape(1, -1)
    return {
        "caw": conv_a_w[l],
        "a2": jnp.pad(gla_a2[l], ((0, LANES - GLA_LOWRANK), (0, 0))).astype(BF16),
        "ab": row(gla_a_bias[l]), "gn": row(gla_norm[l]),
        "ccw": conv_c_w[l], "ccb": row(conv_c_b[l]),
        "lcg": row(ln_c_g[l]), "lcb": row(ln_c_b[l]), "ldg": row(ln_d_g[l]), "ldb": row(ln_d_b[l]),
        "sgw": sg_w[l],
        "sgb": jnp.repeat(sg_b[l].T, dv, axis=1),
        "sgw0": row(jnp.repeat(sg_w[l][:, 0, 0], dv)),
        "sgb0": row(jnp.repeat(sg_b[l][:, 0], dv)),
    }


def _gla_constants(g, tl):
    qk, dv = g // 2, g // N_HEADS
    r = jnp.arange(tl)
    tri = ((r[:, None] >= r[None, :]) & (r[:, None] // GLA_CHUNK == r[None, :] // GLA_CHUNK)).astype(BF16)
    e2 = (jnp.arange(qk)[:, None] // (qk // N_HEADS) == jnp.arange(g)[None, :] // dv).astype(BF16)
    return tri, e2


def kernel(x_prompt, x_sample, state_conv_a, state_gla, state_conv_c, norm_mix, w_in, conv_a_w, gla_a2, gla_a_bias,
           gla_norm, conv_c_w, conv_c_b, ln_c_g, ln_c_b, ln_d_g, ln_d_b, sg_w, sg_b, w_o, norm_ffn, w_gate, w_up,
           w_down, norm_final):
    batch, seq, d = x_prompt.shape
    nb, dec_seq, _ = x_sample.shape
    depth = w_in.shape[0]
    g = d // 4
    qk, dv = g // 2, g // N_HEADS
    assert dec_seq == 1 and seq % SG_CHUNK == 0 and g % (N_HEADS * LANES) == 0
    dff = w_gate.shape[2]
    n_main = w_in.shape[2] - GLA_LOWRANK
    tl = _largest_tile(seq, FRONT_ROWS, SG_CHUNK)
    tri, e2 = _gla_constants(g, tl)

    lr0 = 3 * g + 2 * qk + 2 * g
    w_head = w_in[:, :, :lr0].astype(BF16)
    w_tail = w_in[:, :, lr0 + GLA_LOWRANK:].astype(BF16)
    w_lr = jnp.pad(w_in[:, :, lr0:lr0 + GLA_LOWRANK], ((0, 0), (0, 0), (0, LANES - GLA_LOWRANK))).astype(BF16)

    mp = batch * seq
    tm_p = _largest_tile(mp, FFN_ROWS, BF16_ROWS)
    tm_s = nb
    tn_in = _largest_tile(math.gcd(n_main, lr0), 4 * MXU_DIM, MXU_DIM)
    tf_cast = _largest_tile(dff, FFN_COLS_F32, MXU_DIM)
    tf = _largest_tile(dff, FFN_COLS_BF16, MXU_DIM)

    xp = x_prompt.reshape(mp, d)
    xs = x_sample.reshape(nb, d)
    st_a, st_c = state_conv_a, state_conv_c
    st_s = state_gla.reshape(depth, nb, qk, dv)
    outs_p = {k: [] for k in ("a", "s", "c", "v")}
    sv_s = []
    for l in range(depth):
        wts = _layer_weights(l, g, conv_a_w, gla_a2, gla_a_bias, gla_norm, conv_c_w, conv_c_b,
                             ln_c_g, ln_c_b, ln_d_g, ln_d_b, sg_w, sg_b)
        wts["tri"], wts["e2"] = tri, e2
        gm, gf = norm_mix[l].reshape(1, d), norm_ffn[l].reshape(1, d)
        gfin = norm_final.reshape(1, d)
        final = l == depth - 1

        z, zlr = _norm_proj(xs, gm, w_head, w_tail, w_lr, l, tm_s, tn_in)
        y, st_a, st_s, st_c, sv = _mixer_sample(z, zlr, st_a, st_s, st_c, wts, l,
                                                SAMPLE_SEQS if l else SAMPLE_SEQS // 2, l > 0)
        xs, hs, w_ob = _proj_res(y, w_o, xs, gf, l, tn_in)

        xp, hn, na, s_new, nc, svp = _front(xp, gm, w_head, w_tail, w_lr, w_ob, gf, l, batch, seq, wts, tl)
        xp, xs, wg_b, wu_b, wd_b = _ffn(hn, xp, w_gate, w_up, w_down, gfin, l, tm_p, tf_cast, 0, 1, final,
                                        rider=(hs, xs))
        if mp > tm_p:
            xp, = _ffn(hn, xp, wg_b, wu_b, wd_b, gfin, None, tm_p, tf, 1, mp // tm_p - 1, final)
        outs_p["a"].append(na)
        outs_p["s"].append(s_new.reshape(batch, N_HEADS, qk // N_HEADS, dv))
        outs_p["c"].append(nc)
        outs_p["v"].append(svp)

        sv_s.append(sv.reshape(nb, 1, g))

    st = jnp.stack
    return (xp.reshape(batch, seq, d), xs.reshape(nb, 1, d),
            st(outs_p["a"]), st_a, st(outs_p["s"]), st_s.reshape(depth, nb, N_HEADS, qk // N_HEADS, dv),
            st(outs_p["c"]), st_c, st(outs_p["v"]), st(sv_s))
```

```python
import functools
import math

import jax
import jax.numpy as jnp
from jax import lax
from jax.experimental import pallas as pl
from jax.experimental.pallas import tpu as pltpu

F32 = jnp.float32
BF16 = jnp.bfloat16
EPS = 1e-6

N_HEADS = 4
GLA_LOWRANK = 16
GLA_TAU = 16.0
SC_WIDTH = 3
CF_WIDTH = 31
SG_CHUNK = 128
GLA_CHUNK = 64
GLA_SUB = 16
FRONT_ROWS = 256
PROJ_SLAB = 256
LANES = 128
SUBLANES = 8
MXU_DIM = 256
BF16_ROWS = 16
FFN_ROWS = 1024
FFN_COLS_BF16 = 512
FFN_COLS_F32 = 256
SAMPLE_SEQS = 32
VMEM_LIMIT = 60 * 1024 * 1024


def _dot(a, b):
    return jnp.dot(a, b, preferred_element_type=F32)


def _dot_nt(a, b):
    return lax.dot_general(a, b, (((1,), (1,)), ((), ())), preferred_element_type=F32)


def _dot_tn(a, b):
    return lax.dot_general(a, b, (((0,), (0,)), ((), ())), preferred_element_type=F32)


def _split_bf16(x):
    hi = x.astype(BF16)
    lo = (x - hi.astype(F32)).astype(BF16)
    return hi, lo


def _rmsnorm(x, g):
    return x * lax.rsqrt(jnp.mean(x * x, axis=-1, keepdims=True) + EPS) * g


def _layernorm(x, g, b):
    mu = jnp.mean(x, axis=-1, keepdims=True)
    xc = x - mu
    return xc * lax.rsqrt(jnp.mean(xc * xc, axis=-1, keepdims=True) + EPS) * g + b


def _sigmoid(x):
    return 1.0 / (1.0 + jnp.exp(-x))


def _silu(x):
    return x * _sigmoid(x)


def _gelu_tanh(x):
    c = math.sqrt(2.0 / math.pi)
    return 0.5 * x * (1.0 + jnp.tanh(c * (x + 0.044715 * (x * x * x))))


def _log_sigmoid(x):
    return jnp.minimum(x, 0.0) - jnp.log(1.0 + jnp.exp(-jnp.abs(x)))


def _norm_proj_kernel(x_ref, g_ref, wh_ref, wt_ref, wlr_ref, z_ref, zlr_ref, h_ref, *, n_head):
    j = pl.program_id(1)

    @pl.when(j == 0)
    def _():
        h = _rmsnorm(x_ref[...], g_ref[...]).astype(BF16)
        h_ref[...] = h
        zlr_ref[...] = _dot(h, wlr_ref[...])

    @pl.when(j < n_head)
    def _():
        z_ref[...] = _dot(h_ref[...], wh_ref[...])

    @pl.when(j >= n_head)
    def _():
        z_ref[...] = _dot(h_ref[...], wt_ref[...])


def _norm_proj(x, g, w_head, w_tail, w_lr, l, tm, tn):
    m, d = x.shape
    n_head, n_tail = w_head.shape[2] // tn, w_tail.shape[2] // tn
    return pl.pallas_call(
        functools.partial(_norm_proj_kernel, n_head=n_head),
        grid=(m // tm, n_head + n_tail),
        in_specs=[
            pl.BlockSpec((tm, d), lambda i, j: (i, 0)),
            pl.BlockSpec((1, d), lambda i, j: (0, 0)),
            pl.BlockSpec((None, d, tn), lambda i, j: (l, 0, jnp.minimum(j, n_head - 1))),
            pl.BlockSpec((None, d, tn), lambda i, j: (l, 0, jnp.maximum(j - n_head, 0))),
            pl.BlockSpec((None, d, LANES), lambda i, j: (l, 0, 0)),
        ],
        out_specs=[
            pl.BlockSpec((tm, tn), lambda i, j: (i, j)),
            pl.BlockSpec((tm, LANES), lambda i, j: (i, 0)),
        ],
        out_shape=[jax.ShapeDtypeStruct((m, (n_head + n_tail) * tn), F32), jax.ShapeDtypeStruct((m, LANES), F32)],
        scratch_shapes=[pltpu.VMEM((tm, d), BF16)],
        compiler_params=pltpu.CompilerParams(
            dimension_semantics=("arbitrary", "arbitrary"), vmem_limit_bytes=VMEM_LIMIT),
    )(x, g, w_head, w_tail, w_lr)


def _proj_res_kernel(y_ref, w_ref, x_ref, g_ref, o_ref, h_ref, wb_ref):
    k = pl.program_id(0)

    @pl.when(k == 0)
    def _():
        o_ref[...] = x_ref[...]

    w = w_ref[...].astype(BF16)
    wb_ref[...] = w
    o_ref[...] += _dot(y_ref[...], w)

    @pl.when(k == pl.num_programs(0) - 1)
    def _():
        h_ref[...] = _rmsnorm(o_ref[...], g_ref[...]).astype(BF16)


def _proj_res(y, w, x, g, l, tk):
    m, k = y.shape
    n = w.shape[2]
    return pl.pallas_call(
        _proj_res_kernel,
        grid=(k // tk,),
        in_specs=[
            pl.BlockSpec((m, tk), lambda i: (0, i)),
            pl.BlockSpec((None, tk, n), lambda i: (l, i, 0)),
            pl.BlockSpec((m, n), lambda i: (0, 0)),
            pl.BlockSpec((1, n), lambda i: (0, 0)),
        ],
        out_specs=[pl.BlockSpec((m, n), lambda i: (0, 0)), pl.BlockSpec((m, n), lambda i: (0, 0)),
                   pl.BlockSpec((tk, n), lambda i: (i, 0))],
        out_shape=[jax.ShapeDtypeStruct((m, n), F32), jax.ShapeDtypeStruct((m, n), BF16),
                   jax.ShapeDtypeStruct((k, n), BF16)],
        compiler_params=pltpu.CompilerParams(
            dimension_semantics=("arbitrary",), vmem_limit_bytes=VMEM_LIMIT),
    )(y, w, x, g)


def _ffn_kernel(*refs, n_groups, final_norm):
    hx = [(refs[2 * k], refs[2 * k + 1]) for k in range(n_groups)]
    wg_ref, wu_ref, wd_ref, g_ref = refs[2 * n_groups:2 * n_groups + 4]
    o_refs = refs[2 * n_groups + 4:3 * n_groups + 4]
    cast_refs = refs[3 * n_groups + 4:]
    j = pl.program_id(1)

    @pl.when(j == 0)
    def _():
        for (_, x_ref), o_ref in zip(hx, o_refs):
            o_ref[...] = x_ref[...]

    wg, wu, wd = wg_ref[...].astype(BF16), wu_ref[...].astype(BF16), wd_ref[...].astype(BF16)
    for ref, w in zip(cast_refs, (wg, wu, wd)):
        ref[...] = w
    for (h_ref, _), o_ref in zip(hx, o_refs):
        h = h_ref[...]
        o_ref[...] += _dot((_silu(_dot(h, wg)) * _dot(h, wu)).astype(BF16), wd)

    if final_norm:
        @pl.when(j == pl.num_programs(1) - 1)
        def _():
            for o_ref in o_refs:
                o_ref[...] = _rmsnorm(o_ref[...], g_ref[...])


def _ffn(h, x, wg, wu, wd, g, l, tm, tf, tile0, n_tiles, final_norm, rider=None):
    m, d = x.shape
    f = wg.shape[-1]
    rows = lambda i, j: (tile0 + i, 0)
    row_mode = pl.Buffered(1) if n_tiles == 1 else None
    row_specs = [pl.BlockSpec((tm, d), rows, pipeline_mode=row_mode)] * 2
    out_specs = [pl.BlockSpec((tm, d), rows)]
    out_shape = [jax.ShapeDtypeStruct((m, d), F32)]
    operands = [h, x]
    aliases = {1: 0}
    if rider is not None:
        assert n_tiles == 1
        whole = pl.BlockSpec(rider[1].shape, lambda i, j: (0, 0))
        row_specs += [whole, whole]
        out_specs.append(whole)
        out_shape.append(jax.ShapeDtypeStruct(rider[1].shape, F32))
        operands += list(rider)
        aliases[3] = 1
    if l is None:
        w_specs = [pl.BlockSpec((d, tf), lambda i, j: (0, j)), pl.BlockSpec((d, tf), lambda i, j: (0, j)),
                   pl.BlockSpec((tf, d), lambda i, j: (j, 0))]
    else:
        w_specs = [pl.BlockSpec((None, d, tf), lambda i, j: (l, 0, j)),
                   pl.BlockSpec((None, d, tf), lambda i, j: (l, 0, j)),
                   pl.BlockSpec((None, tf, d), lambda i, j: (l, j, 0))]
        out_specs += [pl.BlockSpec((d, tf), lambda i, j: (0, j)), pl.BlockSpec((d, tf), lambda i, j: (0, j)),
                      pl.BlockSpec((tf, d), lambda i, j: (j, 0))]
        out_shape += [jax.ShapeDtypeStruct((d, f), BF16), jax.ShapeDtypeStruct((d, f), BF16),
                      jax.ShapeDtypeStruct((f, d), BF16)]
    return pl.pallas_call(
        functools.partial(_ffn_kernel, n_groups=len(operands) // 2, final_norm=final_norm),
        grid=(n_tiles, f // tf),
        in_specs=row_specs + w_specs + [pl.BlockSpec((1, d), lambda i, j: (0, 0))],
        out_specs=out_specs,
        out_shape=out_shape,
        input_output_aliases=aliases,
        compiler_params=pltpu.CompilerParams(
            dimension_semantics=("arbitrary", "arbitrary"), vmem_limit_bytes=VMEM_LIMIT),
    )(*operands, wg, wu, wd, g)


def _gla_gate(zlr, a2_ref, ab_ref):
    xg = _dot(zlr.astype(BF16), a2_ref[...]) + ab_ref[...]
    return _log_sigmoid(xg) * (1.0 / GLA_TAU)


def _head_masks(qk):
    lane_head = lax.broadcasted_iota(jnp.int32, (1, qk), 1) // (qk // N_HEADS)
    return [(lane_head == h).astype(F32) for h in range(N_HEADS)]


def _gla_chunk(qc, kc, vc, bc, la_hi, la_lo, s_flat, e2, hm):
    c, qk = qc.shape
    dv = vc.shape[1] // N_HEADS
    dk = qk // N_HEADS
    nb = c // GLA_SUB
    vcb = vc.astype(BF16)
    sub_row = lax.broadcasted_iota(jnp.int32, (GLA_SUB, 1), 0)
    col_j = lax.broadcasted_iota(jnp.int32, (1, c), 1)

    sg = SUBLANES
    ts = []
    for blk in range(nb):
        r0 = blk * GLA_SUB
        q_b, k_b, b_b = qc[r0:r0 + GLA_SUB], kc[r0:r0 + GLA_SUB], bc[r0:r0 + GLA_SUB]
        for j in range(GLA_SUB):
            i0 = j // sg * sg
            dec = jnp.exp2(jnp.where(sub_row[i0:] >= j, b_b[i0:] - b_b[j:j + 1], -jnp.inf))
            ts.append(q_b[i0:] * k_b[j:j + 1] * dec)
    r = _dot(jnp.concatenate(ts, axis=0).astype(BF16), e2)
    o_diag = []
    off = 0
    for blk in range(nb):
        v_b = vc[blk * GLA_SUB:(blk + 1) * GLA_SUB]
        acc = [None] * (GLA_SUB // sg)
        for j in range(GLA_SUB):
            for i0 in range(j // sg * sg, GLA_SUB, sg):
                term = r[off:off + sg] * v_b[j:j + 1]
                acc[i0 // sg] = term if acc[i0 // sg] is None else acc[i0 // sg] + term
                off += sg
        o_diag.extend(acc)
    o = jnp.concatenate(o_diag, axis=0)

    bref = [None] + [bc[blk * GLA_SUB - 1:blk * GLA_SUB] for blk in range(1, nb)]
    bref_rows = jnp.concatenate(
        [jnp.zeros((GLA_SUB, qk), F32)] + [jnp.broadcast_to(bref[blk], (GLA_SUB, qk)) for blk in range(1, nb)],
        axis=0)
    qt = qc * jnp.exp2(bc - bref_rows)
    att_blk = [None]
    for blk in range(1, nb):
        kt = (kc * jnp.exp2(jnp.minimum(bref[blk] - bc, 0.0))).astype(BF16)
        q_b = qt[blk * GLA_SUB:(blk + 1) * GLA_SUB]
        qm = jnp.concatenate([q_b * hm[h] for h in range(N_HEADS)], axis=0).astype(BF16)
        a = _dot_nt(qm, kt)
        att_blk.append(jnp.where(col_j < blk * GLA_SUB, a, 0.0))
    o_off = []
    for h in range(N_HEADS):
        att = jnp.concatenate(
            [jnp.zeros((GLA_SUB, c), F32)] + [att_blk[blk][h * GLA_SUB:(h + 1) * GLA_SUB] for blk in range(1, nb)],
            axis=0)
        o_off.append(_dot(att.astype(BF16), vcb[:, h * dv:(h + 1) * dv]))
    o = o + jnp.concatenate(o_off, axis=1)

    qe = qc * jnp.exp2(bc)
    qe_m = jnp.concatenate([qe * hm[h] for h in range(N_HEADS)], axis=0).astype(BF16)
    o_int = _dot(qe_m, s_flat.astype(BF16))
    o = o + jnp.concatenate([o_int[h * c:(h + 1) * c] for h in range(N_HEADS)], axis=1)

    kk = (kc * jnp.exp2(bc[c - 1:c] - bc)).astype(BF16)
    kv = _dot_tn(kk, vcb)
    kv_d = jnp.concatenate([kv[h * dk:(h + 1) * dk, h * dv:(h + 1) * dv] for h in range(N_HEADS)], axis=0)
    ones = jnp.ones((c, dv), BF16)
    decay = jnp.exp(_dot_tn(la_hi, ones) + _dot_tn(la_lo, ones))
    return o, decay * s_flat + kv_d


def _gla_out(o, gn, r):
    dv = o.shape[1] // N_HEADS
    parts = []
    for h in range(N_HEADS):
        oh = o[:, h * dv:(h + 1) * dv]
        parts.append(oh * lax.rsqrt(jnp.mean(oh * oh, axis=-1, keepdims=True) + EPS))
    return jnp.concatenate(parts, axis=1) * gn * _silu(r)


def _cols(g):
    qk = g // 2
    c = {}
    off = 0
    for name, width in (("a_x", g), ("a_b", g), ("a_c", g), ("q", qk), ("k", qk), ("v", g), ("r", g),
                        ("c_val", g), ("c_gate", g), ("d_u", g), ("d_v", g)):
        c[name] = (off, off + width)
        off += width
    return c, off


def _conv_taps(ccw_ref, cbuf, zbuf, tl, head, between):
    sub = SUBLANES
    base = head - (CF_WIDTH - 1)
    cols = []
    for c0 in range(0, cbuf.shape[1], LANES):
        lanes = slice(c0, c0 + LANES)
        y = None
        for r in range(sub):
            taps = [k for k in range(CF_WIDTH) if (base + k) % sub == r]
            rows = tl if r == 0 else tl + sub
            part = None
            for k in taps:
                off = base + k - r
                term = ccw_ref[k:k + 1, lanes] * cbuf[off:off + rows, lanes]
                part = term if part is None else part + term
            if r == 0:
                shifted = part
            else:
                zbuf[0:rows, lanes] = part
                shifted = zbuf[r:r + tl, lanes]
            y = shifted if y is None else y + shifted
            between()
        cols.append(y)
    return jnp.concatenate(cols, axis=1)


def _front_kernel(x_ref, gm_ref, wh_ref, wt_ref, wlr_ref, caw_ref, a2_ref, ab_ref, gn_ref, ccw_ref, ccb_ref,
                  lcg_ref, lcb_ref, ldg_ref, ldb_ref, sgw_ref, sgb_ref, tri_ref, e2_ref, wo_ref, gf_ref,
                  o_ref, hn_ref, na_ref, s_ref, nc_ref, sv_ref, abuf, cbuf, zbuf):
    t = pl.program_id(1)
    last = pl.num_programs(1) - 1
    tl = x_ref.shape[0]
    g = gn_ref.shape[1]
    qk = g // 2
    dv = g // N_HEADS
    cols, _ = _cols(g)
    a_head, c_head = SUBLANES, 4 * SUBLANES

    @pl.when(t == 0)
    def _():
        abuf[0:a_head, :] = jnp.zeros((a_head, g), F32)
        cbuf[0:c_head, :] = jnp.zeros((c_head, g), F32)
        s_ref[...] = jnp.zeros(s_ref.shape, F32)

    x = x_ref[...]
    h = _rmsnorm(x, gm_ref[...]).astype(BF16)

    slab = PROJ_SLAB
    order = [c0 for n in ("c_val", "c_gate", "a_x", "a_b", "a_c", "d_u", "d_v", "q", "k", "v", "r")
             for c0 in range(cols[n][0], cols[n][1], slab) if c0 % slab == 0]
    pending = list(dict.fromkeys(order))
    slabs = {}

    n_head = wh_ref.shape[1]

    def pump():
        if pending:
            c0 = pending.pop(0)
            w = wh_ref[:, c0:c0 + slab] if c0 < n_head else wt_ref[:, c0 - n_head:c0 - n_head + slab]
            slabs[c0] = _dot(h, w)

    def proj(*names):
        out = []
        for n in names:
            lo, hi = cols[n]
            parts = []
            for c0 in range(lo - lo % slab, hi, slab):
                while c0 not in slabs:
                    pump()
                s_lo, s_hi = max(lo, c0) - c0, min(hi, c0 + slab) - c0
                parts.append(slabs[c0][:, s_lo:s_hi])
            out.append(parts[0] if len(parts) == 1 else jnp.concatenate(parts, axis=1))
        return out

    def out_proj(y, idx):
        return _dot(y.astype(BF16), wo_ref[idx * g:(idx + 1) * g, :])

    c_val, c_gate = proj("c_val", "c_gate")
    cin = c_val * _sigmoid(c_gate)
    cbuf[c_head:c_head + tl, :] = cin
    acc = x

    a_x, a_b, a_c = proj("a_x", "a_b", "a_c")
    xa = a_c * a_x
    abuf[a_head:a_head + tl, :] = xa
    conv = caw_ref[SC_WIDTH - 1:SC_WIDTH, :] * xa
    for kk in range(SC_WIDTH - 1):
        sh = SC_WIDTH - 1 - kk
        conv = conv + caw_ref[kk:kk + 1, :] * abuf[a_head - sh:a_head - sh + tl, :]
    y_a = a_b * conv

    d_u, d_v = proj("d_u", "d_v")
    du = _gelu_tanh(d_u)
    vd = _layernorm(_gelu_tanh(d_v), ldg_ref[...], ldb_ref[...])
    sv_ref[...] = vd[tl - SG_CHUNK:tl]
    vdb = vd.astype(BF16)
    row = lax.broadcasted_iota(jnp.int32, (SG_CHUNK, SG_CHUNK), 0)
    col = lax.broadcasted_iota(jnp.int32, (SG_CHUNK, SG_CHUNK), 1)
    ws = [jnp.where(row >= col, sgw_ref[hh], 0.0).astype(BF16) for hh in range(N_HEADS)]
    sv_rows = []
    for ci in range(tl // SG_CHUNK):
        sl = slice(ci * SG_CHUNK, (ci + 1) * SG_CHUNK)
        sv_rows.append(jnp.concatenate(
            [_dot(ws[hh], vdb[sl, hh * dv:(hh + 1) * dv]) for hh in range(N_HEADS)], axis=1) + sgb_ref[...])
    y_d = du * jnp.concatenate(sv_rows, axis=0)

    terms = []
    jobs = [lambda: terms.append(out_proj(y_a, 0)), lambda: terms.append(out_proj(y_d, 3))] + [pump] * len(pending)
    n_calls = (g // LANES) * SUBLANES
    progress = {"calls": 0, "done": 0}

    def between():
        progress["calls"] += 1
        while progress["done"] * n_calls < progress["calls"] * len(jobs):
            jobs[progress["done"]]()
            progress["done"] += 1

    conv_c = _conv_taps(ccw_ref, cbuf, zbuf, tl, c_head, between) + ccb_ref[...]
    assert progress["done"] == len(jobs)
    acc = acc + terms[0] + terms[1] + out_proj(_silu(_layernorm(conv_c, lcg_ref[...], lcb_ref[...])), 2)

    q_all, k_all, v_all, r_all = proj("q", "k", "v", "r")
    la = _gla_gate(_dot(h, wlr_ref[...]), a2_ref, ab_ref)
    la_hi, la_lo = _split_bf16(la)
    tri = tri_ref[...]
    b_all = (_dot(tri, la_hi) + _dot(tri, la_lo)) * math.log2(math.e)
    q_all = q_all * (float(qk // N_HEADS) ** -0.5)
    hm = _head_masks(qk)
    e2 = e2_ref[...]
    s_flat = s_ref[...]
    outs = []
    for ci in range(tl // GLA_CHUNK):
        sl = slice(ci * GLA_CHUNK, (ci + 1) * GLA_CHUNK)
        o_c, s_flat = _gla_chunk(q_all[sl], k_all[sl], v_all[sl], b_all[sl], la_hi[sl], la_lo[sl], s_flat, e2, hm)
        outs.append(o_c)
    s_ref[...] = s_flat
    acc = acc + out_proj(_gla_out(jnp.concatenate(outs, axis=0), gn_ref[...], r_all), 1)
    o_ref[...] = acc
    hn_ref[...] = _rmsnorm(acc, gf_ref[...]).astype(BF16)

    @pl.when(t == last)
    def _():
        na_ref[...] = abuf[a_head + tl - (SC_WIDTH - 1):a_head + tl, :]
        nc_ref[...] = cbuf[c_head + tl - (CF_WIDTH - 1):c_head + tl, :]

    abuf[0:a_head, :] = abuf[tl:tl + a_head, :]
    cbuf[0:c_head, :] = cbuf[tl:tl + c_head, :]


def _full(shape):
    nd = len(shape)
    return pl.BlockSpec(shape, lambda *_: (0,) * nd)


def _resident(a, l=None):
    if l is None:
        return pl.BlockSpec(a.shape, lambda *_: (0,) * a.ndim, pipeline_mode=pl.Buffered(1))
    return pl.BlockSpec((None,) + a.shape[1:], lambda *_: (l,) + (0,) * (a.ndim - 1), pipeline_mode=pl.Buffered(1))


def _front(x, gm, w_head, w_tail, w_lr, w_o, gf, l, batch, seq, wts, tl):
    d = x.shape[1]
    g = wts["gn"].shape[1]
    nt = seq // tl
    row_map = lambda b, t: (b * nt + t, 0)
    names = ("caw", "a2", "ab", "gn", "ccw", "ccb", "lcg", "lcb", "ldg", "ldb", "sgw", "sgb", "tri", "e2")
    consts = [wts[n] for n in names]
    return pl.pallas_call(
        _front_kernel,
        grid=(batch, nt),
        in_specs=[pl.BlockSpec((tl, d), row_map), _full(gm.shape),
                  _resident(w_head, l), _resident(w_tail, l), _resident(w_lr, l)]
                 + [_full(c.shape) for c in consts] + [_resident(w_o), _full(gf.shape)],
        out_specs=[
            pl.BlockSpec((tl, d), row_map),
            pl.BlockSpec((tl, d), row_map),
            pl.BlockSpec((None, SC_WIDTH - 1, g), lambda b, t: (b, 0, 0)),
            pl.BlockSpec((None, g // 2, g // N_HEADS), lambda b, t: (b, 0, 0)),
            pl.BlockSpec((None, CF_WIDTH - 1, g), lambda b, t: (b, 0, 0)),
            pl.BlockSpec((None, SG_CHUNK, g), lambda b, t: (b, 0, 0)),
        ],
        out_shape=[
            jax.ShapeDtypeStruct(x.shape, F32),
            jax.ShapeDtypeStruct(x.shape, BF16),
            jax.ShapeDtypeStruct((batch, SC_WIDTH - 1, g), F32),
            jax.ShapeDtypeStruct((batch, g // 2, g // N_HEADS), F32),
            jax.ShapeDtypeStruct((batch, CF_WIDTH - 1, g), F32),
            jax.ShapeDtypeStruct((batch, SG_CHUNK, g), F32),
        ],
        scratch_shapes=[pltpu.VMEM((SUBLANES + tl, g), F32), pltpu.VMEM((4 * SUBLANES + tl, g), F32),
                        pltpu.VMEM((SUBLANES + tl, g), F32)],
        compiler_params=pltpu.CompilerParams(
            dimension_semantics=("arbitrary", "arbitrary"), vmem_limit_bytes=VMEM_LIMIT),
    )(x, gm, w_head, w_tail, w_lr, *consts, w_o, gf)


def _mixer_sample_kernel(z_ref, zlr_ref, pa_ref, s_ref, pc_ref, caw_ref, a2_ref, ab_ref, gn_ref, ccw_ref, ccb_ref,
                         lcg_ref, lcb_ref, ldg_ref, ldb_ref, sgw0_ref, sgb0_ref,
                         y_ref, na_ref, so_ref, nc_ref, sv_ref, o_scr, *, layer):
    if layer is not None:
        for ll in range(pa_ref.shape[0]):
            if ll != layer:
                na_ref[ll] = pa_ref[ll]
                so_ref[ll] = s_ref[ll]
                nc_ref[ll] = pc_ref[ll]
        pa_ref, s_ref, pc_ref = pa_ref.at[layer], s_ref.at[layer], pc_ref.at[layer]
        na_ref, so_ref, nc_ref = na_ref.at[layer], so_ref.at[layer], nc_ref.at[layer]
    bt = z_ref.shape[0]
    g = gn_ref.shape[1]
    qk = g // 2
    dk = qk // N_HEADS
    dv = g // N_HEADS
    cols, _ = _cols(g)

    def zc(name):
        lo, hi = cols[name]
        return z_ref[:, lo:hi]

    xa = zc("a_c") * zc("a_x")
    conv = caw_ref[SC_WIDTH - 1:SC_WIDTH, :] * xa
    for kk in range(SC_WIDTH - 1):
        conv = conv + caw_ref[kk:kk + 1, :] * pa_ref[:, kk, :]
    y_ref[:, 0:g] = (zc("a_b") * conv).astype(BF16)
    for kk in range(1, SC_WIDTH - 1):
        na_ref[:, kk - 1, :] = pa_ref[:, kk, :]
    na_ref[:, SC_WIDTH - 2, :] = xa

    la = _gla_gate(zlr_ref[...], a2_ref, ab_ref)
    a_t = jnp.exp(la).T
    k_t = zc("k").T
    q_rows = zc("q") * (float(dk) ** -0.5)
    v_rows = zc("v")
    hm = _head_masks(qk)
    hm_rows = jnp.concatenate(hm + [jnp.zeros_like(hm[0])] * (SUBLANES - N_HEADS), axis=0)
    for b in range(bt):
        a_col = jnp.broadcast_to(a_t[:, b:b + 1], (qk, dv))
        k_col = jnp.broadcast_to(k_t[:, b:b + 1], (qk, dv))
        v_b = jnp.concatenate(
            [jnp.broadcast_to(v_rows[b:b + 1, h * dv:(h + 1) * dv], (dk, dv)) for h in range(N_HEADS)], axis=0)
        s_new = a_col * s_ref[b] + k_col * v_b
        so_ref[b] = s_new
        q_m = (q_rows[b:b + 1] * hm_rows).astype(BF16)
        o_b = _dot(q_m, s_new.astype(BF16))
        for h in range(N_HEADS):
            o_scr[b:b + 1, h * dv:(h + 1) * dv] = o_b[h:h + 1]
    y_ref[:, g:2 * g] = _gla_out(o_scr[...], gn_ref[...], zc("r")).astype(BF16)

    cin = zc("c_val") * _sigmoid(zc("c_gate"))
    acc = ccb_ref[...] + ccw_ref[CF_WIDTH - 1:CF_WIDTH, :] * cin
    for kk in range(CF_WIDTH - 1):
        acc = acc + ccw_ref[kk:kk + 1, :] * pc_ref[:, kk, :]
    y_ref[:, 2 * g:3 * g] = _silu(_layernorm(acc, lcg_ref[...], lcb_ref[...])).astype(BF16)
    nc_ref[:, 0:CF_WIDTH - 2, :] = pc_ref[:, 1:CF_WIDTH - 1, :]
    nc_ref[:, CF_WIDTH - 2, :] = cin

    du = _gelu_tanh(zc("d_u"))
    vd = _layernorm(_gelu_tanh(zc("d_v")), ldg_ref[...], ldb_ref[...])
    sv_ref[...] = vd
    y_ref[:, 3 * g:4 * g] = (du * (sgw0_ref[...] * vd + sgb0_ref[...])).astype(BF16)


def _mixer_sample(z, zlr, pa, s, pc, wts, l, bt, in_place):
    nb, g = z.shape[0], wts["gn"].shape[1]
    names = ("caw", "a2", "ab", "gn", "ccw", "ccb", "lcg", "lcb", "ldg", "ldb", "sgw0", "sgb0")
    consts = [wts[n] for n in names]
    rows = lambda i: (i, 0)
    if in_place:
        state_spec = lambda a: pl.BlockSpec((None, bt) + a.shape[2:], lambda i: (l, i, 0, 0))
    else:
        state_spec = lambda a: pl.BlockSpec((a.shape[0], bt) + a.shape[2:], lambda i: (0, i, 0, 0))
    return pl.pallas_call(
        functools.partial(_mixer_sample_kernel, layer=None if in_place else l),
        grid=(nb // bt,),
        in_specs=[pl.BlockSpec((bt, z.shape[1]), rows), pl.BlockSpec((bt, LANES), rows),
                  state_spec(pa), state_spec(s), state_spec(pc)] + [_full(c.shape) for c in consts],
        out_specs=[pl.BlockSpec((bt, 4 * g), rows), state_spec(pa), state_spec(s), state_spec(pc),
                   pl.BlockSpec((bt, g), rows)],
        out_shape=[
            jax.ShapeDtypeStruct((nb, 4 * g), BF16),
            jax.ShapeDtypeStruct(pa.shape, F32),
            jax.ShapeDtypeStruct(s.shape, F32),
            jax.ShapeDtypeStruct(pc.shape, F32),
            jax.ShapeDtypeStruct((nb, g), F32),
        ],
        input_output_aliases={2: 1, 3: 2, 4: 3} if in_place else {},
        scratch_shapes=[pltpu.VMEM((bt, g), F32)],
        compiler_params=pltpu.CompilerParams(
            dimension_semantics=("arbitrary",), vmem_limit_bytes=VMEM_LIMIT),
    )(z, zlr, pa, s, pc, *consts)


def _largest_tile(m, cap, mult):
    t = min(m, cap)
    while m % t or t % mult:
        t -= mult
    return t


def _layer_weights(l, g, conv_a_w, gla_a2, gla_a_bias, gla_norm, conv_c_w, conv_c_b,
                   ln_c_g, ln_c_b, ln_d_g, ln_d_b, sg_w, sg_b):
    dv = g // N_HEADS
    row = lambda v: v.reshape(1, -1)
    return {
        "caw": conv_a_w[l],
        "a2": jnp.pad(gla_a2[l], ((0, LANES - GLA_LOWRANK), (0, 0))).astype(BF16),
        "ab": row(gla_a_bias[l]), "gn": row(gla_norm[l]),
        "ccw": conv_c_w[l], "ccb": row(conv_c_b[l]),
        "lcg": row(ln_c_g[l]), "lcb": row(ln_c_b[l]), "ldg": row(ln_d_g[l]), "ldb": row(ln_d_b[l]),
        "sgw": sg_w[l],
        "sgb": jnp.repeat(sg_b[l].T, dv, axis=1),
        "sgw0": row(jnp.repeat(sg_w[l][:, 0, 0], dv)),
        "sgb0": row(jnp.repeat(sg_b[l][:, 0], dv)),
    }


def _gla_constants(g, tl):
    qk, dv = g // 2, g // N_HEADS
    r = jnp.arange(tl)
    tri = ((r[:, None] >= r[None, :]) & (r[:, None] // GLA_CHUNK == r[None, :] // GLA_CHUNK)).astype(BF16)
    e2 = (jnp.arange(qk)[:, None] // (qk // N_HEADS) == jnp.arange(g)[None, :] // dv).astype(BF16)
    return tri, e2


def kernel(x_prompt, x_sample, state_conv_a, state_gla, state_conv_c, norm_mix, w_in, conv_a_w, gla_a2, gla_a_bias,
           gla_norm, conv_c_w, conv_c_b, ln_c_g, ln_c_b, ln_d_g, ln_d_b, sg_w, sg_b, w_o, norm_ffn, w_gate, w_up,
           w_down, norm_final):
    batch, seq, d = x_prompt.shape
    nb, dec_seq, _ = x_sample.shape
    depth = w_in.shape[0]
    g = d // 4
    qk, dv = g // 2, g // N_HEADS
    assert dec_seq == 1 and seq % SG_CHUNK == 0 and g % (N_HEADS * LANES) == 0
    dff = w_gate.shape[2]
    n_main = w_in.shape[2] - GLA_LOWRANK
    tl = _largest_tile(seq, FRONT_ROWS, SG_CHUNK)
    tri, e2 = _gla_constants(g, tl)

    lr0 = 3 * g + 2 * qk + 2 * g
    w_head = w_in[:, :, :lr0].astype(BF16)
    w_tail = w_in[:, :, lr0 + GLA_LOWRANK:].astype(BF16)
    w_lr = jnp.pad(w_in[:, :, lr0:lr0 + GLA_LOWRANK], ((0, 0), (0, 0), (0, LANES - GLA_LOWRANK))).astype(BF16)

    mp = batch * seq
    tm_p = _largest_tile(mp, FFN_ROWS, BF16_ROWS)
    tm_s = nb
    tn_in = _largest_tile(math.gcd(n_main, lr0), 4 * MXU_DIM, MXU_DIM)
    tf_cast = _largest_tile(dff, FFN_COLS_F32, MXU_DIM)
    tf = _largest_tile(dff, FFN_COLS_BF16, MXU_DIM)

    xp = x_prompt.reshape(mp, d)
    xs = x_sample.reshape(nb, d)
    st_a, st_c = state_conv_a, state_conv_c
    st_s = state_gla.reshape(depth, nb, qk, dv)
    outs_p = {k: [] for k in ("a", "s", "c", "v")}
    sv_s = []
    for l in range(depth):
        wts = _layer_weights(l, g, conv_a_w, gla_a2, gla_a_bias, gla_norm, conv_c_w, conv_c_b,
                             ln_c_g, ln_c_b, ln_d_g, ln_d_b, sg_w, sg_b)
        wts["tri"], wts["e2"] = tri, e2
        gm, gf = norm_mix[l].reshape(1, d), norm_ffn[l].reshape(1, d)
        gfin = norm_final.reshape(1, d)
        final = l == depth - 1

        z, zlr = _norm_proj(xs, gm, w_head, w_tail, w_lr, l, tm_s, tn_in)
        y, st_a, st_s, st_c, sv = _mixer_sample(z, zlr, st_a, st_s, st_c, wts, l,
                                                SAMPLE_SEQS if l else SAMPLE_SEQS // 2, l > 0)
        xs, hs, w_ob = _proj_res(y, w_o, xs, gf, l, tn_in)

        xp, hn, na, s_new, nc, svp = _front(xp, gm, w_head, w_tail, w_lr, w_ob, gf, l, batch, seq, wts, tl)
        xp, xs, wg_b, wu_b, wd_b = _ffn(hn, xp, w_gate, w_up, w_down, gfin, l, tm_p, tf_cast, 0, 1, final,
                                        rider=(hs, xs))
        if mp > tm_p:
            xp, = _ffn(hn, xp, wg_b, wu_b, wd_b, gfin, None, tm_p, tf, 1, mp // tm_p - 1, final)
        outs_p["a"].append(na)
        outs_p["s"].append(s_new.reshape(batch, N_HEADS, qk // N_HEADS, dv))
        outs_p["c"].append(nc)
        outs_p["v"].append(svp)

        sv_s.append(sv.reshape(nb, 1, g))

    st = jnp.stack
    return (xp.reshape(batch, seq, d), xs.reshape(nb, 1, d),
            st(outs_p["a"]), st_a, st(outs_p["s"]), st_s.reshape(depth, nb, N_HEADS, qk // N_HEADS, dv),
            st(outs_p["c"]), st_c, st(outs_p["v"]), st(sv_s))
```

```python
import functools
import math

import jax
import jax.numpy as jnp
from jax import lax
from jax.experimental import pallas as pl
from jax.experimental.pallas import tpu as pltpu

F32 = jnp.float32
BF16 = jnp.bfloat16
EPS = 1e-6

N_HEADS = 4
GLA_LOWRANK = 16
GLA_TAU = 16.0
SC_WIDTH = 3
CF_WIDTH = 31
SG_CHUNK = 128
GLA_CHUNK = 64
GLA_SUB = 16
FRONT_ROWS = 256
PROJ_SLAB = 256
LANES = 128
SUBLANES = 8
MXU_DIM = 256
BF16_ROWS = 16
FFN_ROWS = 1024
FFN_COLS_BF16 = 512
FFN_COLS_F32 = 256
SAMPLE_SEQS = 32
VMEM_LIMIT = 60 * 1024 * 1024


def _dot(a, b):
    return jnp.dot(a, b, preferred_element_type=F32)


def _dot_nt(a, b):
    return lax.dot_general(a, b, (((1,), (1,)), ((), ())), preferred_element_type=F32)


def _dot_tn(a, b):
    return lax.dot_general(a, b, (((0,), (0,)), ((), ())), preferred_element_type=F32)


def _split_bf16(x):
    hi = x.astype(BF16)
    lo = (x - hi.astype(F32)).astype(BF16)
    return hi, lo


def _rmsnorm(x, g):
    return x * lax.rsqrt(jnp.mean(x * x, axis=-1, keepdims=True) + EPS) * g


def _layernorm(x, g, b):
    mu = jnp.mean(x, axis=-1, keepdims=True)
    xc = x - mu
    return xc * lax.rsqrt(jnp.mean(xc * xc, axis=-1, keepdims=True) + EPS) * g + b


def _sigmoid(x):
    return 1.0 / (1.0 + jnp.exp(-x))


def _silu(x):
    return x * _sigmoid(x)


def _gelu_tanh(x):
    c = math.sqrt(2.0 / math.pi)
    return 0.5 * x * (1.0 + jnp.tanh(c * (x + 0.044715 * (x * x * x))))


def _log_sigmoid(x):
    return jnp.minimum(x, 0.0) - jnp.log(1.0 + jnp.exp(-jnp.abs(x)))


def _norm_proj_kernel(x_ref, g_ref, wh_ref, wt_ref, wlr_ref, z_ref, zlr_ref, h_ref, *, n_head):
    j = pl.program_id(1)

    @pl.when(j == 0)
    def _():
        h = _rmsnorm(x_ref[...], g_ref[...]).astype(BF16)
        h_ref[...] = h
        zlr_ref[...] = _dot(h, wlr_ref[...])

    @pl.when(j < n_head)
    def _():
        z_ref[...] = _dot(h_ref[...], wh_ref[...])

    @pl.when(j >= n_head)
    def _():
        z_ref[...] = _dot(h_ref[...], wt_ref[...])


def _norm_proj(x, g, w_head, w_tail, w_lr, l, tm, tn):
    m, d = x.shape
    n_head, n_tail = w_head.shape[2] // tn, w_tail.shape[2] // tn
    return pl.pallas_call(
        functools.partial(_norm_proj_kernel, n_head=n_head),
        grid=(m // tm, n_head + n_tail),
        in_specs=[
            pl.BlockSpec((tm, d), lambda i, j: (i, 0)),
            pl.BlockSpec((1, d), lambda i, j: (0, 0)),
            pl.BlockSpec((None, d, tn), lambda i, j: (l, 0, jnp.minimum(j, n_head - 1))),
            pl.BlockSpec((None, d, tn), lambda i, j: (l, 0, jnp.maximum(j - n_head, 0))),
            pl.BlockSpec((None, d, LANES), lambda i, j: (l, 0, 0)),
        ],
        out_specs=[
            pl.BlockSpec((tm, tn), lambda i, j: (i, j)),
            pl.BlockSpec((tm, LANES), lambda i, j: (i, 0)),
        ],
        out_shape=[jax.ShapeDtypeStruct((m, (n_head + n_tail) * tn), F32), jax.ShapeDtypeStruct((m, LANES), F32)],
        scratch_shapes=[pltpu.VMEM((tm, d), BF16)],
        compiler_params=pltpu.CompilerParams(
            dimension_semantics=("arbitrary", "arbitrary"), vmem_limit_bytes=VMEM_LIMIT),
    )(x, g, w_head, w_tail, w_lr)


def _proj_res_kernel(y_ref, w_ref, x_ref, g_ref, o_ref, h_ref, wb_ref):
    k = pl.program_id(0)

    @pl.when(k == 0)
    def _():
        o_ref[...] = x_ref[...]

    w = w_ref[...].astype(BF16)
    wb_ref[...] = w
    o_ref[...] += _dot(y_ref[...], w)

    @pl.when(k == pl.num_programs(0) - 1)
    def _():
        h_ref[...] = _rmsnorm(o_ref[...], g_ref[...]).astype(BF16)


def _proj_res(y, w, x, g, l, tk):
    m, k = y.shape
    n = w.shape[2]
    return pl.pallas_call(
        _proj_res_kernel,
        grid=(k // tk,),
        in_specs=[
            pl.BlockSpec((m, tk), lambda i: (0, i)),
            pl.BlockSpec((None, tk, n), lambda i: (l, i, 0)),
            pl.BlockSpec((m, n), lambda i: (0, 0)),
            pl.BlockSpec((1, n), lambda i: (0, 0)),
        ],
        out_specs=[pl.BlockSpec((m, n), lambda i: (0, 0)), pl.BlockSpec((m, n), lambda i: (0, 0)),
                   pl.BlockSpec((tk, n), lambda i: (i, 0))],
        out_shape=[jax.ShapeDtypeStruct((m, n), F32), jax.ShapeDtypeStruct((m, n), BF16),
                   jax.ShapeDtypeStruct((k, n), BF16)],
        compiler_params=pltpu.CompilerParams(
            dimension_semantics=("arbitrary",), vmem_limit_bytes=VMEM_LIMIT),
    )(y, w, x, g)


def _ffn_kernel(*refs, n_groups, final_norm):
    hx = [(refs[2 * k], refs[2 * k + 1]) for k in range(n_groups)]
    wg_ref, wu_ref, wd_ref, g_ref = refs[2 * n_groups:2 * n_groups + 4]
    o_refs = refs[2 * n_groups + 4:3 * n_groups + 4]
    cast_refs = refs[3 * n_groups + 4:]
    j = pl.program_id(1)

    @pl.when(j == 0)
    def _():
        for (_, x_ref), o_ref in zip(hx, o_refs):
            o_ref[...] = x_ref[...]

    wg, wu, wd = wg_ref[...].astype(BF16), wu_ref[...].astype(BF16), wd_ref[...].astype(BF16)
    for ref, w in zip(cast_refs, (wg, wu, wd)):
        ref[...] = w
    for (h_ref, _), o_ref in zip(hx, o_refs):
        h = h_ref[...]
        o_ref[...] += _dot((_silu(_dot(h, wg)) * _dot(h, wu)).astype(BF16), wd)

    if final_norm:
        @pl.when(j == pl.num_programs(1) - 1)
        def _():
            for o_ref in o_refs:
                o_ref[...] = _rmsnorm(o_ref[...], g_ref[...])


def _ffn(h, x, wg, wu, wd, g, l, tm, tf, tile0, n_tiles, final_norm, rider=None):
    m, d = x.shape
    f = wg.shape[-1]
    rows = lambda i, j: (tile0 + i, 0)
    row_mode = pl.Buffered(1) if n_tiles == 1 else None
    row_specs = [pl.BlockSpec((tm, d), rows, pipeline_mode=row_mode)] * 2
    out_specs = [pl.BlockSpec((tm, d), rows)]
    out_shape = [jax.ShapeDtypeStruct((m, d), F32)]
    operands = [h, x]
    aliases = {1: 0}
    if rider is not None:
        assert n_tiles == 1
        whole = pl.BlockSpec(rider[1].shape, lambda i, j: (0, 0))
        row_specs += [whole, whole]
        out_specs.append(whole)
        out_shape.append(jax.ShapeDtypeStruct(rider[1].shape, F32))
        operands += list(rider)
        aliases[3] = 1
    if l is None:
        w_specs = [pl.BlockSpec((d, tf), lambda i, j: (0, j)), pl.BlockSpec((d, tf), lambda i, j: (0, j)),
                   pl.BlockSpec((tf, d), lambda i, j: (j, 0))]
    else:
        w_specs = [pl.BlockSpec((None, d, tf), lambda i, j: (l, 0, j)),
                   pl.BlockSpec((None, d, tf), lambda i, j: (l, 0, j)),
                   pl.BlockSpec((None, tf, d), lambda i, j: (l, j, 0))]
        out_specs += [pl.BlockSpec((d, tf), lambda i, j: (0, j)), pl.BlockSpec((d, tf), lambda i, j: (0, j)),
                      pl.BlockSpec((tf, d), lambda i, j: (j, 0))]
        out_shape += [jax.ShapeDtypeStruct((d, f), BF16), jax.ShapeDtypeStruct((d, f), BF16),
                      jax.ShapeDtypeStruct((f, d), BF16)]
    return pl.pallas_call(
        functools.partial(_ffn_kernel, n_groups=len(operands) // 2, final_norm=final_norm),
        grid=(n_tiles, f // tf),
        in_specs=row_specs + w_specs + [pl.BlockSpec((1, d), lambda i, j: (0, 0))],
        out_specs=out_specs,
        out_shape=out_shape,
        input_output_aliases=aliases,
        compiler_params=pltpu.CompilerParams(
            dimension_semantics=("arbitrary", "arbitrary"), vmem_limit_bytes=VMEM_LIMIT),
    )(*operands, wg, wu, wd, g)


def _gla_gate(zlr, a2_ref, ab_ref):
    xg = _dot(zlr.astype(BF16), a2_ref[...]) + ab_ref[...]
    return _log_sigmoid(xg) * (1.0 / GLA_TAU)


def _head_masks(qk):
    lane_head = lax.broadcasted_iota(jnp.int32, (1, qk), 1) // (qk // N_HEADS)
    return [(lane_head == h).astype(F32) for h in range(N_HEADS)]


def _gla_chunk(qc, kc, vc, bc, la_hi, la_lo, s_flat, e2, hm, between):
    c, qk = qc.shape
    dv = vc.shape[1] // N_HEADS
    dk = qk // N_HEADS
    nb = c // GLA_SUB
    vcb = vc.astype(BF16)
    sub_row = lax.broadcasted_iota(jnp.int32, (GLA_SUB, 1), 0)
    col_j = lax.broadcasted_iota(jnp.int32, (1, c), 1)

    sg = SUBLANES
    ts = []
    for blk in range(nb):
        r0 = blk * GLA_SUB
        q_b, k_b, b_b = qc[r0:r0 + GLA_SUB], kc[r0:r0 + GLA_SUB], bc[r0:r0 + GLA_SUB]
        for j in range(GLA_SUB):
            i0 = j // sg * sg
            dec = jnp.exp2(jnp.where(sub_row[i0:] >= j, b_b[i0:] - b_b[j:j + 1], -jnp.inf))
            ts.append(q_b[i0:] * k_b[j:j + 1] * dec)
    r = _dot(jnp.concatenate(ts, axis=0).astype(BF16), e2)
    o_diag = []
    off = 0
    for blk in range(nb):
        v_b = vc[blk * GLA_SUB:(blk + 1) * GLA_SUB]
        acc = [None] * (GLA_SUB // sg)
        for j in range(GLA_SUB):
            for i0 in range(j // sg * sg, GLA_SUB, sg):
                term = r[off:off + sg] * v_b[j:j + 1]
                acc[i0 // sg] = term if acc[i0 // sg] is None else acc[i0 // sg] + term
                off += sg
        o_diag.extend(acc)
    o = jnp.concatenate(o_diag, axis=0)
    between()

    bref = [None] + [bc[blk * GLA_SUB - 1:blk * GLA_SUB] for blk in range(1, nb)]
    bref_rows = jnp.concatenate(
        [jnp.zeros((GLA_SUB, qk), F32)] + [jnp.broadcast_to(bref[blk], (GLA_SUB, qk)) for blk in range(1, nb)],
        axis=0)
    qt = qc * jnp.exp2(bc - bref_rows)
    att_blk = [None]
    for blk in range(1, nb):
        kt = (kc * jnp.exp2(jnp.minimum(bref[blk] - bc, 0.0))).astype(BF16)
        q_b = qt[blk * GLA_SUB:(blk + 1) * GLA_SUB]
        qm = jnp.concatenate([q_b * hm[h] for h in range(N_HEADS)], axis=0).astype(BF16)
        a = _dot_nt(qm, kt)
        att_blk.append(jnp.where(col_j < blk * GLA_SUB, a, 0.0))
    o_off = []
    for h in range(N_HEADS):
        att = jnp.concatenate(
            [jnp.zeros((GLA_SUB, c), F32)] + [att_blk[blk][h * GLA_SUB:(h + 1) * GLA_SUB] for blk in range(1, nb)],
            axis=0)
        o_off.append(_dot(att.astype(BF16), vcb[:, h * dv:(h + 1) * dv]))
    o = o + jnp.concatenate(o_off, axis=1)
    between()

    qe = qc * jnp.exp2(bc)
    qe_m = jnp.concatenate([qe * hm[h] for h in range(N_HEADS)], axis=0).astype(BF16)
    o_int = _dot(qe_m, s_flat.astype(BF16))
    o = o + jnp.concatenate([o_int[h * c:(h + 1) * c] for h in range(N_HEADS)], axis=1)
    between()

    kk =(kc * jnp.exp2(bc[c - 1:c] - bc)).astype(BF16)
    kv = _dot_tn(kk, vcb)
    kv_d = jnp.concatenate([kv[h * dk:(h + 1) * dk, h * dv:(h + 1) * dv] for h in range(N_HEADS)], axis=0)
    ones = jnp.ones((c, dv), BF16)
    decay = jnp.exp(_dot_tn(la_hi, ones) + _dot_tn(la_lo, ones))
    between()
    return o, decay * s_flat + kv_d


def _gla_out(o, gn, r):
    dv = o.shape[1] // N_HEADS
    parts = []
    for h in range(N_HEADS):
        oh = o[:, h * dv:(h + 1) * dv]
        parts.append(oh * lax.rsqrt(jnp.mean(oh * oh, axis=-1, keepdims=True) + EPS))
    return jnp.concatenate(parts, axis=1) * gn * _silu(r)


def _cols(g):
    qk = g // 2
    c = {}
    off = 0
    for name, width in (("a_x", g), ("a_b", g), ("a_c", g), ("q", qk), ("k", qk), ("v", g), ("r", g),
                        ("c_val", g), ("c_gate", g), ("d_u", g), ("d_v", g)):
        c[name] = (off, off + width)
        off += width
    return c, off


def _conv_taps(ccw_ref, cbuf, zbuf, tl, head, between):
    sub = SUBLANES
    base = head - (CF_WIDTH - 1)
    cols = []
    for c0 in range(0, cbuf.shape[1], LANES):
        lanes = slice(c0, c0 + LANES)
        y = None
        for r in range(sub):
            taps = [k for k in range(CF_WIDTH) if (base + k) % sub == r]
            rows = tl if r == 0 else tl + sub
            part = None
            for k in taps:
                off = base + k - r
                term = ccw_ref[k:k + 1, lanes] * cbuf[off:off + rows, lanes]
                part = term if part is None else part + term
            if r == 0:
                shifted = part
            else:
                zbuf[0:rows, lanes] = part
                shifted = zbuf[r:r + tl, lanes]
            y = shifted if y is None else y + shifted
            between()
        cols.append(y)
    return jnp.concatenate(cols, axis=1)


def _front_kernel(x_ref, gm_ref, wh_ref, wt_ref, wlr_ref, caw_ref, a2_ref, ab_ref, gn_ref, ccw_ref, ccb_ref,
                  lcg_ref, lcb_ref, ldg_ref, ldb_ref, sgw_ref, sgb_ref, tri_ref, e2_ref, wo_ref, gf_ref,
                  o_ref, hn_ref, na_ref, s_ref, nc_ref, sv_ref, abuf, cbuf, zbuf):
    t = pl.program_id(1)
    last = pl.num_programs(1) - 1
    tl = x_ref.shape[0]
    g = gn_ref.shape[1]
    qk = g // 2
    dv = g // N_HEADS
    cols, _ = _cols(g)
    a_head, c_head = SUBLANES, 4 * SUBLANES

    @pl.when(t == 0)
    def _():
        abuf[0:a_head, :] = jnp.zeros((a_head, g), F32)
        cbuf[0:c_head, :] = jnp.zeros((c_head, g), F32)
        s_ref[...] = jnp.zeros(s_ref.shape, F32)

    x = x_ref[...]
    h = _rmsnorm(x, gm_ref[...]).astype(BF16)

    slab = PROJ_SLAB
    order = [c0 for n in ("q", "k", "v", "r", "c_val", "c_gate", "a_x", "a_b", "a_c", "d_u", "d_v")
             for c0 in range(cols[n][0], cols[n][1], slab) if c0 % slab == 0]
    pending = list(dict.fromkeys(order))
    slabs = {}

    n_head = wh_ref.shape[1]

    def pump():
        if pending:
            c0 = pending.pop(0)
            w = wh_ref[:, c0:c0 + slab] if c0 < n_head else wt_ref[:, c0 - n_head:c0 - n_head + slab]
            slabs[c0] = _dot(h, w)

    def proj(*names):
        out = []
        for n in names:
            lo, hi = cols[n]
            parts = []
            for c0 in range(lo - lo % slab, hi, slab):
                while c0 not in slabs:
                    pump()
                s_lo, s_hi = max(lo, c0) - c0, min(hi, c0 + slab) - c0
                parts.append(slabs[c0][:, s_lo:s_hi])
            out.append(parts[0] if len(parts) == 1 else jnp.concatenate(parts, axis=1))
        return out

    def out_proj(y, idx):
        return _dot(y.astype(BF16), wo_ref[idx * g:(idx + 1) * g, :])

    q_all, k_all, v_all, r_all = proj("q", "k", "v", "r")
    la = _gla_gate(_dot(h, wlr_ref[...]), a2_ref, ab_ref)
    la_hi, la_lo = _split_bf16(la)
    tri = tri_ref[...]
    b_all = (_dot(tri, la_hi) + _dot(tri, la_lo)) * math.log2(math.e)
    q_all = q_all * (float(qk // N_HEADS) ** -0.5)
    hm = _head_masks(qk)
    e2 = e2_ref[...]
    s_flat = s_ref[...]
    outs = []
    for ci in range(tl // GLA_CHUNK):
        sl = slice(ci * GLA_CHUNK, (ci + 1) * GLA_CHUNK)
        o_c, s_flat = _gla_chunk(q_all[sl], k_all[sl], v_all[sl], b_all[sl], la_hi[sl], la_lo[sl], s_flat, e2, hm,
                                 pump)
        outs.append(o_c)
    s_ref[...] = s_flat
    y_b = _gla_out(jnp.concatenate(outs, axis=0), gn_ref[...], r_all)

    c_val, c_gate = proj("c_val", "c_gate")
    cin = c_val * _sigmoid(c_gate)
    cbuf[c_head:c_head + tl, :] = cin

    a_x, a_b, a_c = proj("a_x", "a_b", "a_c")
    xa = a_c * a_x
    abuf[a_head:a_head + tl, :] = xa
    conv = caw_ref[SC_WIDTH - 1:SC_WIDTH, :] * xa
    for kk in range(SC_WIDTH - 1):
        sh = SC_WIDTH - 1 - kk
        conv = conv + caw_ref[kk:kk + 1, :] * abuf[a_head - sh:a_head - sh + tl, :]
    y_a = a_b * conv

    d_u, d_v = proj("d_u", "d_v")
    du = _gelu_tanh(d_u)
    vd = _layernorm(_gelu_tanh(d_v), ldg_ref[...], ldb_ref[...])
    sv_ref[...] = vd[tl - SG_CHUNK:tl]
    vdb = vd.astype(BF16)
    row = lax.broadcasted_iota(jnp.int32, (SG_CHUNK, SG_CHUNK), 0)
    col = lax.broadcasted_iota(jnp.int32, (SG_CHUNK, SG_CHUNK), 1)
    ws = [jnp.where(row >= col, sgw_ref[hh], 0.0).astype(BF16) for hh in range(N_HEADS)]
    sv_rows = []
    for ci in range(tl // SG_CHUNK):
        sl = slice(ci * SG_CHUNK, (ci + 1) * SG_CHUNK)
        sv_rows.append(jnp.concatenate(
            [_dot(ws[hh], vdb[sl, hh * dv:(hh + 1) * dv]) for hh in range(N_HEADS)], axis=1) + sgb_ref[...])
    y_d = du * jnp.concatenate(sv_rows, axis=0)

    terms = []
    jobs = [lambda: terms.append(out_proj(y_b, 1)), lambda: terms.append(out_proj(y_a, 0)),
            lambda: terms.append(out_proj(y_d, 3))] + [pump] * len(pending)
    n_calls = (g // LANES) * SUBLANES
    progress = {"calls": 0, "done": 0}

    def between():
        progress["calls"] += 1
        while progress["done"] * n_calls < progress["calls"] * len(jobs):
            jobs[progress["done"]]()
            progress["done"] += 1

    conv_c = _conv_taps(ccw_ref, cbuf, zbuf, tl, c_head, between) + ccb_ref[...]
    assert progress["done"] == len(jobs)
    acc = x + terms[0] + terms[1] + terms[2] + out_proj(_silu(_layernorm(conv_c, lcg_ref[...], lcb_ref[...])), 2)
    o_ref[...] = acc
    hn_ref[...] = _rmsnorm(acc, gf_ref[...]).astype(BF16)

    @pl.when(t == last)
    def _():
        na_ref[...] = abuf[a_head + tl - (SC_WIDTH - 1):a_head + tl, :]
        nc_ref[...] = cbuf[c_head + tl - (CF_WIDTH - 1):c_head + tl, :]

    abuf[0:a_head, :] = abuf[tl:tl + a_head, :]
    cbuf[0:c_head, :] = cbuf[tl:tl + c_head, :]


def _full(shape):
    nd = len(shape)
    return pl.BlockSpec(shape, lambda *_: (0,) * nd)


def _resident(a, l=None):
    if l is None:
        return pl.BlockSpec(a.shape, lambda *_: (0,) * a.ndim, pipeline_mode=pl.Buffered(1))
    return pl.BlockSpec((None,) + a.shape[1:], lambda *_: (l,) + (0,) * (a.ndim - 1), pipeline_mode=pl.Buffered(1))


def _front(x, gm, w_head, w_tail, w_lr, w_o, gf, l, batch, seq, wts, tl):
    d = x.shape[1]
    g = wts["gn"].shape[1]
    nt = seq // tl
    row_map = lambda b, t: (b * nt + t, 0)
    names = ("caw", "a2", "ab", "gn", "ccw", "ccb", "lcg", "lcb", "ldg", "ldb", "sgw", "sgb", "tri", "e2")
    consts = [wts[n] for n in names]
    return pl.pallas_call(
        _front_kernel,
        grid=(batch, nt),
        in_specs=[pl.BlockSpec((tl, d), row_map), _full(gm.shape),
                  _resident(w_head, l), _resident(w_tail, l), _resident(w_lr, l)]
                 + [_full(c.shape) for c in consts] + [_resident(w_o), _full(gf.shape)],
        out_specs=[
            pl.BlockSpec((tl, d), row_map),
            pl.BlockSpec((tl, d), row_map),
            pl.BlockSpec((None, SC_WIDTH - 1, g), lambda b, t: (b, 0, 0)),
            pl.BlockSpec((None, g // 2, g // N_HEADS), lambda b, t: (b, 0, 0)),
            pl.BlockSpec((None, CF_WIDTH - 1, g), lambda b, t: (b, 0, 0)),
            pl.BlockSpec((None, SG_CHUNK, g), lambda b, t: (b, 0, 0)),
        ],
        out_shape=[
            jax.ShapeDtypeStruct(x.shape, F32),
            jax.ShapeDtypeStruct(x.shape, BF16),
            jax.ShapeDtypeStruct((batch, SC_WIDTH - 1, g), F32),
            jax.ShapeDtypeStruct((batch, g // 2, g // N_HEADS), F32),
            jax.ShapeDtypeStruct((batch, CF_WIDTH - 1, g), F32),
            jax.ShapeDtypeStruct((batch, SG_CHUNK, g), F32),
        ],
        scratch_shapes=[pltpu.VMEM((SUBLANES + tl, g), F32), pltpu.VMEM((4 * SUBLANES + tl, g), F32),
                        pltpu.VMEM((SUBLANES + tl, g), F32)],
        compiler_params=pltpu.CompilerParams(
            dimension_semantics=("arbitrary", "arbitrary"), vmem_limit_bytes=VMEM_LIMIT),
    )(x, gm, w_head, w_tail, w_lr, *consts, w_o, gf)


def _mixer_sample_kernel(z_ref, zlr_ref, pa_ref, s_ref, pc_ref, caw_ref, a2_ref, ab_ref, gn_ref, ccw_ref, ccb_ref,
                         lcg_ref, lcb_ref, ldg_ref, ldb_ref, sgw0_ref, sgb0_ref,
                         y_ref, na_ref, so_ref, nc_ref, sv_ref, o_scr, *, layer):
    if layer is not None:
        for ll in range(pa_ref.shape[0]):
            if ll != layer:
                na_ref[ll] = pa_ref[ll]
                so_ref[ll] = s_ref[ll]
                nc_ref[ll] = pc_ref[ll]
        pa_ref, s_ref, pc_ref = pa_ref.at[layer], s_ref.at[layer], pc_ref.at[layer]
        na_ref, so_ref, nc_ref = na_ref.at[layer], so_ref.at[layer], nc_ref.at[layer]
    bt = z_ref.shape[0]
    g = gn_ref.shape[1]
    qk = g // 2
    dk = qk // N_HEADS
    dv = g // N_HEADS
    cols, _ = _cols(g)

    def zc(name):
        lo, hi = cols[name]
        return z_ref[:, lo:hi]

    xa = zc("a_c") * zc("a_x")
    conv = caw_ref[SC_WIDTH - 1:SC_WIDTH, :] * xa
    for kk in range(SC_WIDTH - 1):
        conv = conv + caw_ref[kk:kk + 1, :] * pa_ref[:, kk, :]
    y_ref[:, 0:g] = (zc("a_b") * conv).astype(BF16)
    for kk in range(1, SC_WIDTH - 1):
        na_ref[:, kk - 1, :] = pa_ref[:, kk, :]
    na_ref[:, SC_WIDTH - 2, :] = xa

    la = _gla_gate(zlr_ref[...], a2_ref, ab_ref)
    a_t = jnp.exp(la).T
    k_t = zc("k").T
    q_rows = zc("q") * (float(dk) ** -0.5)
    v_rows = zc("v")
    hm = _head_masks(qk)
    hm_rows = jnp.concatenate(hm + [jnp.zeros_like(hm[0])] * (SUBLANES - N_HEADS), axis=0)
    for b in range(bt):
        a_col = jnp.broadcast_to(a_t[:, b:b + 1], (qk, dv))
        k_col = jnp.broadcast_to(k_t[:, b:b + 1], (qk, dv))
        v_b = jnp.concatenate(
            [jnp.broadcast_to(v_rows[b:b + 1, h * dv:(h + 1) * dv], (dk, dv)) for h in range(N_HEADS)], axis=0)
        s_new = a_col * s_ref[b] + k_col * v_b
        so_ref[b] = s_new
        q_m = (q_rows[b:b + 1] * hm_rows).astype(BF16)
        o_b = _dot(q_m, s_new.astype(BF16))
        for h in range(N_HEADS):
            o_scr[b:b + 1, h * dv:(h + 1) * dv] = o_b[h:h + 1]
    y_ref[:, g:2 * g] = _gla_out(o_scr[...], gn_ref[...], zc("r")).astype(BF16)

    cin = zc("c_val") * _sigmoid(zc("c_gate"))
    acc = ccb_ref[...] + ccw_ref[CF_WIDTH - 1:CF_WIDTH, :] * cin
    for kk in range(CF_WIDTH - 1):
        acc = acc + ccw_ref[kk:kk + 1, :] * pc_ref[:, kk, :]
    y_ref[:, 2 * g:3 * g] = _silu(_layernorm(acc, lcg_ref[...], lcb_ref[...])).astype(BF16)
    nc_ref[:, 0:CF_WIDTH - 2, :] = pc_ref[:, 1:CF_WIDTH - 1, :]
    nc_ref[:, CF_WIDTH - 2, :] = cin

    du = _gelu_tanh(zc("d_u"))
    vd = _layernorm(_gelu_tanh(zc("d_v")), ldg_ref[...], ldb_ref[...])
    sv_ref[...] = vd
    y_ref[:, 3 * g:4 * g] = (du * (sgw0_ref[...] * vd + sgb0_ref[...])).astype(BF16)


def _mixer_sample(z, zlr, pa, s, pc, wts, l, bt, in_place):
    nb, g = z.shape[0], wts["gn"].shape[1]
    names = ("caw", "a2", "ab", "gn", "ccw", "ccb", "lcg", "lcb", "ldg", "ldb", "sgw0", "sgb0")
    consts = [wts[n] for n in names]
    rows = lambda i: (i, 0)
    if in_place:
        state_spec = lambda a: pl.BlockSpec((None, bt) + a.shape[2:], lambda i: (l, i, 0, 0))
    else:
        state_spec = lambda a: pl.BlockSpec((a.shape[0], bt) + a.shape[2:], lambda i: (0, i, 0, 0))
    return pl.pallas_call(
        functools.partial(_mixer_sample_kernel, layer=None if in_place else l),
        grid=(nb // bt,),
        in_specs=[pl.BlockSpec((bt, z.shape[1]), rows), pl.BlockSpec((bt, LANES), rows),
                  state_spec(pa), state_spec(s), state_spec(pc)] + [_full(c.shape) for c in consts],
        out_specs=[pl.BlockSpec((bt, 4 * g), rows), state_spec(pa), state_spec(s), state_spec(pc),
                   pl.BlockSpec((bt, g), rows)],
        out_shape=[
            jax.ShapeDtypeStruct((nb, 4 * g), BF16),
            jax.ShapeDtypeStruct(pa.shape, F32),
            jax.ShapeDtypeStruct(s.shape, F32),
            jax.ShapeDtypeStruct(pc.shape, F32),
            jax.ShapeDtypeStruct((nb, g), F32),
        ],
        input_output_aliases={2: 1, 3: 2, 4: 3} if in_place else {},
        scratch_shapes=[pltpu.VMEM((bt, g), F32)],
        compiler_params=pltpu.CompilerParams(
            dimension_semantics=("arbitrary",), vmem_limit_bytes=VMEM_LIMIT),
    )(z, zlr, pa, s, pc, *consts)


def _largest_tile(m, cap, mult):
    t = min(m, cap)
    while m % t or t % mult:
        t -= mult
    return t


def _layer_weights(l, g, conv_a_w, gla_a2, gla_a_bias, gla_norm, conv_c_w, conv_c_b,
                   ln_c_g, ln_c_b, ln_d_g, ln_d_b, sg_w, sg_b):
    dv = g // N_HEADS
    row = lambda v: v.reshape(1, -1)
    return {
        "caw": conv_a_w[l],
        "a2": jnp.pad(gla_a2[l], ((0, LANES - GLA_LOWRANK), (0, 0))).astype(BF16),
        "ab": row(gla_a_bias[l]), "gn": row(gla_norm[l]),
        "ccw": conv_c_w[l], "ccb": row(conv_c_b[l]),
        "lcg": row(ln_c_g[l]), "lcb": row(ln_c_b[l]), "ldg": row(ln_d_g[l]), "ldb": row(ln_d_b[l]),
        "sgw": sg_w[l],
        "sgb": jnp.repeat(sg_b[l].T, dv, axis=1),
        "sgw0": row(jnp.repeat(sg_w[l][:, 0, 0], dv)),
        "sgb0": row(jnp.repeat(sg_b[l][:, 0], dv)),
    }


def _gla_constants(g, tl):
    qk, dv = g // 2, g // N_HEADS
    r = jnp.arange(tl)
    tri = ((r[:, None] >= r[None, :]) & (r[:, None] // GLA_CHUNK == r[None, :] // GLA_CHUNK)).astype(BF16)
    e2 = (jnp.arange(qk)[:, None] // (qk // N_HEADS) == jnp.arange(g)[None, :] // dv).astype(BF16)
    return tri, e2


def kernel(x_prompt, x_sample, state_conv_a, state_gla, state_conv_c, norm_mix, w_in, conv_a_w, gla_a2, gla_a_bias,
           gla_norm, conv_c_w, conv_c_b, ln_c_g, ln_c_b, ln_d_g, ln_d_b, sg_w, sg_b, w_o, norm_ffn, w_gate, w_up,
           w_down, norm_final):
    batch, seq, d = x_prompt.shape
    nb, dec_seq, _ = x_sample.shape
    depth = w_in.shape[0]
    g = d // 4
    qk, dv = g // 2, g // N_HEADS
    assert dec_seq == 1 and seq % SG_CHUNK == 0 and g % (N_HEADS * LANES) == 0
    dff = w_gate.shape[2]
    n_main = w_in.shape[2] - GLA_LOWRANK
    tl = _largest_tile(seq, FRONT_ROWS, SG_CHUNK)
    tri, e2 = _gla_constants(g, tl)

    lr0 = 3 * g + 2 * qk + 2 * g
    w_head = w_in[:, :, :lr0].astype(BF16)
    w_tail = w_in[:, :, lr0 + GLA_LOWRANK:].astype(BF16)
    w_lr = jnp.pad(w_in[:, :, lr0:lr0 + GLA_LOWRANK], ((0, 0), (0, 0), (0, LANES - GLA_LOWRANK))).astype(BF16)

    mp = batch * seq
    tm_p = _largest_tile(mp, FFN_ROWS, BF16_ROWS)
    tm_s = nb
    tn_in = _largest_tile(math.gcd(n_main, lr0), 4 * MXU_DIM, MXU_DIM)
    tf_cast = _largest_tile(dff, FFN_COLS_F32, MXU_DIM)
    tf = _largest_tile(dff, FFN_COLS_BF16, MXU_DIM)

    xp = x_prompt.reshape(mp, d)
    xs = x_sample.reshape(nb, d)
    st_a, st_c = state_conv_a, state_conv_c
    st_s = state_gla.reshape(depth, nb, qk, dv)
    outs_p = {k: [] for k in ("a", "s", "c", "v")}
    sv_s = []
    for l in range(depth):
        wts = _layer_weights(l, g, conv_a_w, gla_a2, gla_a_bias, gla_norm, conv_c_w, conv_c_b,
                             ln_c_g, ln_c_b, ln_d_g, ln_d_b, sg_w, sg_b)
        wts["tri"], wts["e2"] = tri, e2
        gm, gf = norm_mix[l].reshape(1, d), norm_ffn[l].reshape(1, d)
        gfin = norm_final.reshape(1, d)
        final = l == depth - 1

        z, zlr = _norm_proj(xs, gm, w_head, w_tail, w_lr, l, tm_s, tn_in)
        y, st_a, st_s, st_c, sv = _mixer_sample(z, zlr, st_a, st_s, st_c, wts, l,
                                                SAMPLE_SEQS if l else SAMPLE_SEQS // 2, l > 0)
        xs, hs, w_ob = _proj_res(y, w_o, xs, gf, l, tn_in)

        xp, hn, na, s_new, nc, svp = _front(xp, gm, w_head, w_tail, w_lr, w_ob, gf, l, batch, seq, wts, tl)
        xp, xs, wg_b, wu_b, wd_b = _ffn(hn, xp, w_gate, w_up, w_down, gfin, l, tm_p, tf_cast, 0, 1, final,
                                        rider=(hs, xs))
        if mp > tm_p:
            xp, = _ffn(hn, xp, wg_b, wu_b, wd_b, gfin, None, tm_p, tf, 1, mp // tm_p - 1, final)
        outs_p["a"].append(na)
        outs_p["s"].append(s_new.reshape(batch, N_HEADS, qk // N_HEADS, dv))
        outs_p["c"].append(nc)
        outs_p["v"].append(svp)

        sv_s.append(sv.reshape(nb, 1, g))

    st = jnp.stack
    return (xp.reshape(batch, seq, d), xs.reshape(nb, 1, d),
            st(outs_p["a"]), st_a, st(outs_p["s"]), st_s.reshape(depth, nb, N_HEADS, qk // N_HEADS, dv),
            st(outs_p["c"]), st_c, st(outs_p["v"]), st(sv_s))
```

```python
import functools
import math

import jax
import jax.numpy as jnp
from jax import lax
from jax.experimental import pallas as pl
from jax.experimental.pallas import tpu as pltpu

F32 = jnp.float32
BF16 = jnp.bfloat16
EPS = 1e-6

N_HEADS = 4
GLA_LOWRANK = 16
GLA_TAU = 16.0
SC_WIDTH = 3
CF_WIDTH = 31
SG_CHUNK = 128
GLA_CHUNK = 64
GLA_SUB = 16
FRONT_ROWS = 256
PROJ_SLAB = 256
LANES = 128
SUBLANES = 8
MXU_DIM = 256
BF16_ROWS = 16
FFN_ROWS = 1024
FFN_COLS_BF16 = 512
FFN_COLS_F32 = 256
SAMPLE_SEQS = 32
VMEM_LIMIT = 60 * 1024 * 1024


def _dot(a, b):
    return jnp.dot(a, b, preferred_element_type=F32)


def _dot_nt(a, b):
    return lax.dot_general(a, b, (((1,), (1,)), ((), ())), preferred_element_type=F32)


def _dot_tn(a, b):
    return lax.dot_general(a, b, (((0,), (0,)), ((), ())), preferred_element_type=F32)


def _split_bf16(x):
    hi = x.astype(BF16)
    lo = (x - hi.astype(F32)).astype(BF16)
    return hi, lo


def _rmsnorm(x, g):
    return x * lax.rsqrt(jnp.mean(x * x, axis=-1, keepdims=True) + EPS) * g


def _layernorm(x, g, b):
    mu = jnp.mean(x, axis=-1, keepdims=True)
    xc = x - mu
    return xc * lax.rsqrt(jnp.mean(xc * xc, axis=-1, keepdims=True) + EPS) * g + b


def _sigmoid(x):
    return 1.0 / (1.0 + jnp.exp(-x))


def _silu(x):
    return x * _sigmoid(x)


def _gelu_tanh(x):
    c = math.sqrt(2.0 / math.pi)
    return 0.5 * x * (1.0 + jnp.tanh(c * (x + 0.044715 * (x * x * x))))


def _log_sigmoid(x):
    return jnp.minimum(x, 0.0) - jnp.log(1.0 + jnp.exp(-jnp.abs(x)))


def _norm_proj_kernel(x_ref, g_ref, wh_ref, wt_ref, wlr_ref, z_ref, zlr_ref, h_ref, *, n_head):
    j = pl.program_id(1)

    @pl.when(j == 0)
    def _():
        h = _rmsnorm(x_ref[...], g_ref[...]).astype(BF16)
        h_ref[...] = h
        zlr_ref[...] = _dot(h, wlr_ref[...])

    @pl.when(j < n_head)
    def _():
        z_ref[...] = _dot(h_ref[...], wh_ref[...])

    @pl.when(j >= n_head)
    def _():
        z_ref[...] = _dot(h_ref[...], wt_ref[...])


def _norm_proj(x, g, w_head, w_tail, w_lr, l, tm, tn):
    m, d = x.shape
    n_head, n_tail = w_head.shape[2] // tn, w_tail.shape[2] // tn
    return pl.pallas_call(
        functools.partial(_norm_proj_kernel, n_head=n_head),
        grid=(m // tm, n_head + n_tail),
        in_specs=[
            pl.BlockSpec((tm, d), lambda i, j: (i, 0)),
            pl.BlockSpec((1, d), lambda i, j: (0, 0)),
            pl.BlockSpec((None, d, tn), lambda i, j: (l, 0, jnp.minimum(j, n_head - 1))),
            pl.BlockSpec((None, d, tn), lambda i, j: (l, 0, jnp.maximum(j - n_head, 0))),
            pl.BlockSpec((None, d, LANES), lambda i, j: (l, 0, 0)),
        ],
        out_specs=[
            pl.BlockSpec((tm, tn), lambda i, j: (i, j)),
            pl.BlockSpec((tm, LANES), lambda i, j: (i, 0)),
        ],
        out_shape=[jax.ShapeDtypeStruct((m, (n_head + n_tail) * tn), F32), jax.ShapeDtypeStruct((m, LANES), F32)],
        scratch_shapes=[pltpu.VMEM((tm, d), BF16)],
        compiler_params=pltpu.CompilerParams(
            dimension_semantics=("arbitrary", "arbitrary"), vmem_limit_bytes=VMEM_LIMIT),
    )(x, g, w_head, w_tail, w_lr)


def _proj_res_kernel(y_ref, w_ref, x_ref, g_ref, o_ref, h_ref, wb_ref):
    k = pl.program_id(0)

    @pl.when(k == 0)
    def _():
        o_ref[...] = x_ref[...]

    w = w_ref[...].astype(BF16)
    wb_ref[...] = w
    o_ref[...] += _dot(y_ref[...], w)

    @pl.when(k == pl.num_programs(0) - 1)
    def _():
        h_ref[...] = _rmsnorm(o_ref[...], g_ref[...]).astype(BF16)


def _proj_res(y, w, x, g, l, tk):
    m, k = y.shape
    n = w.shape[2]
    return pl.pallas_call(
        _proj_res_kernel,
        grid=(k // tk,),
        in_specs=[
            pl.BlockSpec((m, tk), lambda i: (0, i)),
            pl.BlockSpec((None, tk, n), lambda i: (l, i, 0)),
            pl.BlockSpec((m, n), lambda i: (0, 0)),
            pl.BlockSpec((1, n), lambda i: (0, 0)),
        ],
        out_specs=[pl.BlockSpec((m, n), lambda i: (0, 0)), pl.BlockSpec((m, n), lambda i: (0, 0)),
                   pl.BlockSpec((tk, n), lambda i: (i, 0))],
        out_shape=[jax.ShapeDtypeStruct((m, n), F32), jax.ShapeDtypeStruct((m, n), BF16),
                   jax.ShapeDtypeStruct((k, n), BF16)],
        compiler_params=pltpu.CompilerParams(
            dimension_semantics=("arbitrary",), vmem_limit_bytes=VMEM_LIMIT),
    )(y, w, x, g)


def _ffn_kernel(*refs, n_groups, final_norm):
    hx = [(refs[2 * k], refs[2 * k + 1]) for k in range(n_groups)]
    wg_ref, wu_ref, wd_ref, g_ref = refs[2 * n_groups:2 * n_groups + 4]
    o_refs = refs[2 * n_groups + 4:3 * n_groups + 4]
    cast_refs = refs[3 * n_groups + 4:]
    j = pl.program_id(1)

    @pl.when(j == 0)
    def _():
        for (_, x_ref), o_ref in zip(hx, o_refs):
            o_ref[...] = x_ref[...]

    wg, wu, wd = wg_ref[...].astype(BF16), wu_ref[...].astype(BF16), wd_ref[...].astype(BF16)
    for ref, w in zip(cast_refs, (wg, wu, wd)):
        ref[...] = w
    for (h_ref, _), o_ref in zip(hx, o_refs):
        h = h_ref[...]
        o_ref[...] += _dot((_silu(_dot(h, wg)) * _dot(h, wu)).astype(BF16), wd)

    if final_norm:
        @pl.when(j == pl.num_programs(1) - 1)
        def _():
            for o_ref in o_refs:
                o_ref[...] = _rmsnorm(o_ref[...], g_ref[...])


def _ffn(h, x, wg, wu, wd, g, l, tm, tf, tile0, n_tiles, final_norm, rider=None):
    m, d = x.shape
    f = wg.shape[-1]
    rows = lambda i, j: (tile0 + i, 0)
    row_mode = pl.Buffered(1) if n_tiles == 1 else None
    row_specs = [pl.BlockSpec((tm, d), rows, pipeline_mode=row_mode)] * 2
    out_specs = [pl.BlockSpec((tm, d), rows)]
    out_shape = [jax.ShapeDtypeStruct((m, d), F32)]
    operands = [h, x]
    aliases = {1: 0}
    if rider is not None:
        assert n_tiles == 1
        whole = pl.BlockSpec(rider[1].shape, lambda i, j: (0, 0))
        row_specs += [whole, whole]
        out_specs.append(whole)
        out_shape.append(jax.ShapeDtypeStruct(rider[1].shape, F32))
        operands += list(rider)
        aliases[3] = 1
    if l is None:
        w_specs = [pl.BlockSpec((d, tf), lambda i, j: (0, j)), pl.BlockSpec((d, tf), lambda i, j: (0, j)),
                   pl.BlockSpec((tf, d), lambda i, j: (j, 0))]
    else:
        w_specs = [pl.BlockSpec((None, d, tf), lambda i, j: (l, 0, j)),
                   pl.BlockSpec((None, d, tf), lambda i, j: (l, 0, j)),
                   pl.BlockSpec((None, tf, d), lambda i, j: (l, j, 0))]
        out_specs += [pl.BlockSpec((d, tf), lambda i, j: (0, j)), pl.BlockSpec((d, tf), lambda i, j: (0, j)),
                      pl.BlockSpec((tf, d), lambda i, j: (j, 0))]
        out_shape += [jax.ShapeDtypeStruct((d, f), BF16), jax.ShapeDtypeStruct((d, f), BF16),
                      jax.ShapeDtypeStruct((f, d), BF16)]
    return pl.pallas_call(
        functools.partial(_ffn_kernel, n_groups=len(operands) // 2, final_norm=final_norm),
        grid=(n_tiles, f // tf),
        in_specs=row_specs + w_specs + [pl.BlockSpec((1, d), lambda i, j: (0, 0))],
        out_specs=out_specs,
        out_shape=out_shape,
        input_output_aliases=aliases,
        compiler_params=pltpu.CompilerParams(
            dimension_semantics=("arbitrary", "arbitrary"), vmem_limit_bytes=VMEM_LIMIT),
    )(*operands, wg, wu, wd, g)


def _gla_gate(zlr, a2_ref, ab_ref):
    xg = _dot(zlr.astype(BF16), a2_ref[...]) + ab_ref[...]
    return _log_sigmoid(xg) * (1.0 / GLA_TAU)


def _head_masks(qk):
    lane_head = lax.broadcasted_iota(jnp.int32, (1, qk), 1) // (qk // N_HEADS)
    return [(lane_head == h).astype(F32) for h in range(N_HEADS)]


def _gla_chunk(qc, kc, vc, bc, la_hi, la_lo, s_flat, e2, hm, between):
    c, qk = qc.shape
    dv = vc.shape[1] // N_HEADS
    dk = qk // N_HEADS
    nb = c // GLA_SUB
    vcb = vc.astype(BF16)
    sub_row = lax.broadcasted_iota(jnp.int32, (GLA_SUB, 1), 0)
    col_j = lax.broadcasted_iota(jnp.int32, (1, c), 1)

    sg = SUBLANES
    ts = []
    for blk in range(nb):
        r0 = blk * GLA_SUB
        q_b, k_b, b_b = qc[r0:r0 + GLA_SUB], kc[r0:r0 + GLA_SUB], bc[r0:r0 + GLA_SUB]
        for j in range(GLA_SUB):
            i0 = j // sg * sg
            dec = jnp.exp2(jnp.where(sub_row[i0:] >= j, b_b[i0:] - b_b[j:j + 1], -jnp.inf))
            ts.append(q_b[i0:] * k_b[j:j + 1] * dec)
    r = _dot(jnp.concatenate(ts, axis=0).astype(BF16), e2)
    o_diag = []
    off = 0
    for blk in range(nb):
        v_b = vc[blk * GLA_SUB:(blk + 1) * GLA_SUB]
        acc = [None] * (GLA_SUB // sg)
        for j in range(GLA_SUB):
            for i0 in range(j // sg * sg, GLA_SUB, sg):
                term = r[off:off + sg] * v_b[j:j + 1]
                acc[i0 // sg] = term if acc[i0 // sg] is None else acc[i0 // sg] + term
                off += sg
        o_diag.extend(acc)
    o = jnp.concatenate(o_diag, axis=0)
    between()

    bref = [None] + [bc[blk * GLA_SUB - 1:blk * GLA_SUB] for blk in range(1, nb)]
    bref_rows = jnp.concatenate(
        [jnp.zeros((GLA_SUB, qk), F32)] + [jnp.broadcast_to(bref[blk], (GLA_SUB, qk)) for blk in range(1, nb)],
        axis=0)
    qt = qc * jnp.exp2(bc - bref_rows)
    att_blk = [None]
    for blk in range(1, nb):
        kt = (kc * jnp.exp2(jnp.minimum(bref[blk] - bc, 0.0))).astype(BF16)
        q_b = qt[blk * GLA_SUB:(blk + 1) * GLA_SUB]
        qm = jnp.concatenate([q_b * hm[h] for h in range(N_HEADS)], axis=0).astype(BF16)
        a = _dot_nt(qm, kt)
        att_blk.append(jnp.where(col_j < blk * GLA_SUB, a, 0.0))
    o_off = []
    for h in range(N_HEADS):
        att = jnp.concatenate(
            [jnp.zeros((GLA_SUB, c), F32)] + [att_blk[blk][h * GLA_SUB:(h + 1) * GLA_SUB] for blk in range(1, nb)],
            axis=0)
        o_off.append(_dot(att.astype(BF16), vcb[:, h * dv:(h + 1) * dv]))
    o = o + jnp.concatenate(o_off, axis=1)
    between()

    qe = qc * jnp.exp2(bc)
    qe_m = jnp.concatenate([qe * hm[h] for h in range(N_HEADS)], axis=0).astype(BF16)
    o_int = _dot(qe_m, s_flat.astype(BF16))
    o = o + jnp.concatenate([o_int[h * c:(h + 1) * c] for h in range(N_HEADS)], axis=1)
    between()

    kk =(kc * jnp.exp2(bc[c - 1:c] - bc)).astype(BF16)
    kv = _dot_tn(kk, vcb)
    kv_d = jnp.concatenate([kv[h * dk:(h + 1) * dk, h * dv:(h + 1) * dv] for h in range(N_HEADS)], axis=0)
    ones = jnp.ones((c, dv), BF16)
    decay = jnp.exp(_dot_tn(la_hi, ones) + _dot_tn(la_lo, ones))
    between()
    return o, decay * s_flat + kv_d


def _gla_out(o, gn, r):
    dv = o.shape[1] // N_HEADS
    parts = []
    for h in range(N_HEADS):
        oh = o[:, h * dv:(h + 1) * dv]
        parts.append(oh * lax.rsqrt(jnp.mean(oh * oh, axis=-1, keepdims=True) + EPS))
    return jnp.concatenate(parts, axis=1) * gn * _silu(r)


def _cols(g):
    qk = g // 2
    c = {}
    off = 0
    for name, width in (("a_x", g), ("a_b", g), ("a_c", g), ("q", qk), ("k", qk), ("v", g), ("r", g),
                        ("c_val", g), ("c_gate", g), ("d_u", g), ("d_v", g)):
        c[name] = (off, off + width)
        off += width
    return c, off


def _conv_taps(ccw_ref, cbuf, zbuf, tl, head, between):
    sub = SUBLANES
    base = head - (CF_WIDTH - 1)
    cols = []
    for c0 in range(0, cbuf.shape[1], LANES):
        lanes = slice(c0, c0 + LANES)
        y = None
        for r in range(sub):
            taps = [k for k in range(CF_WIDTH) if (base + k) % sub == r]
            rows = tl if r == 0 else tl + sub
            part = None
            for k in taps:
                off = base + k - r
                term = ccw_ref[k:k + 1, lanes] * cbuf[off:off + rows, lanes]
                part = term if part is None else part + term
            if r == 0:
                shifted = part
            else:
                zbuf[0:rows, lanes] = part
                shifted = zbuf[r:r + tl, lanes]
            y = shifted if y is None else y + shifted
            between()
        cols.append(y)
    return jnp.concatenate(cols, axis=1)


def _front_kernel(x_ref, gm_ref, wh_ref, wt_ref, wlr_ref, caw_ref, a2_ref, ab_ref, gn_ref, ccw_ref, ccb_ref,
                  lcg_ref, lcb_ref, ldg_ref, ldb_ref, sgw_ref, sgb_ref, tri_ref, e2_ref, wo_ref, gf_ref,
                  o_ref, hn_ref, na_ref, s_ref, nc_ref, sv_ref, abuf, cbuf, zbuf):
    t = pl.program_id(1)
    last = pl.num_programs(1) - 1
    tl = x_ref.shape[0]
    g = gn_ref.shape[1]
    qk = g // 2
    dv = g // N_HEADS
    cols, _ = _cols(g)
    a_head, c_head = SUBLANES, 4 * SUBLANES

    @pl.when(t == 0)
    def _():
        abuf[0:a_head, :] = jnp.zeros((a_head, g), F32)
        cbuf[0:c_head, :] = jnp.zeros((c_head, g), F32)
        s_ref[...] = jnp.zeros(s_ref.shape, F32)

    x = x_ref[...]
    h = _rmsnorm(x, gm_ref[...]).astype(BF16)

    slab = PROJ_SLAB
    order = [c0 for n in ("q", "k", "v", "r", "c_val", "c_gate", "a_x", "a_b", "a_c", "d_u", "d_v")
             for c0 in range(cols[n][0], cols[n][1], slab) if c0 % slab == 0]
    pending = list(dict.fromkeys(order))
    slabs = {}

    n_head = wh_ref.shape[1]

    def pump():
        if pending:
            c0 = pending.pop(0)
            w = wh_ref[:, c0:c0 + slab] if c0 < n_head else wt_ref[:, c0 - n_head:c0 - n_head + slab]
            slabs[c0] = _dot(h, w)

    def proj(*names):
        out = []
        for n in names:
            lo, hi = cols[n]
            parts = []
            for c0 in range(lo - lo % slab, hi, slab):
                while c0 not in slabs:
                    pump()
                s_lo, s_hi = max(lo, c0) - c0, min(hi, c0 + slab) - c0
                parts.append(slabs[c0][:, s_lo:s_hi])
            out.append(parts[0] if len(parts) == 1 else jnp.concatenate(parts, axis=1))
        return out

    def out_proj(y, idx):
        return _dot(y.astype(BF16), wo_ref[idx * g:(idx + 1) * g, :])

    q_all, k_all, v_all, r_all = proj("q", "k", "v", "r")
    la = _gla_gate(_dot(h, wlr_ref[...]), a2_ref, ab_ref)
    la_hi, la_lo = _split_bf16(la)
    tri = tri_ref[...]
    b_all = (_dot(tri, la_hi) + _dot(tri, la_lo)) * math.log2(math.e)
    q_all = q_all * (float(qk // N_HEADS) ** -0.5)
    hm = _head_masks(qk)
    e2 = e2_ref[...]
    s_flat = s_ref[...]
    outs = []
    for ci in range(tl // GLA_CHUNK):
        sl = slice(ci * GLA_CHUNK, (ci + 1) * GLA_CHUNK)
        o_c, s_flat = _gla_chunk(q_all[sl], k_all[sl], v_all[sl], b_all[sl], la_hi[sl], la_lo[sl], s_flat, e2, hm,
                                 pump)
        outs.append(o_c)
    s_ref[...] = s_flat
    y_b = _gla_out(jnp.concatenate(outs, axis=0), gn_ref[...], r_all)
    terms = [out_proj(y_b, 1)]

    c_val, c_gate = proj("c_val", "c_gate")
    cin = c_val * _sigmoid(c_gate)
    cbuf[c_head:c_head + tl, :] = cin

    a_x, a_b, a_c = proj("a_x", "a_b", "a_c")
    xa = a_c * a_x
    abuf[a_head:a_head + tl, :] = xa
    conv = caw_ref[SC_WIDTH - 1:SC_WIDTH, :] * xa
    for kk in range(SC_WIDTH - 1):
        sh = SC_WIDTH - 1 - kk
        conv = conv + caw_ref[kk:kk + 1, :] * abuf[a_head - sh:a_head - sh + tl, :]
    y_a = a_b * conv

    d_u, d_v = proj("d_u", "d_v")
    du = _gelu_tanh(d_u)
    vd = _layernorm(_gelu_tanh(d_v), ldg_ref[...], ldb_ref[...])
    sv_ref[...] = vd[tl - SG_CHUNK:tl]
    vdb = vd.astype(BF16)
    row = lax.broadcasted_iota(jnp.int32, (SG_CHUNK, SG_CHUNK), 0)
    col = lax.broadcasted_iota(jnp.int32, (SG_CHUNK, SG_CHUNK), 1)
    ws = [jnp.where(row >= col, sgw_ref[hh], 0.0).astype(BF16) for hh in range(N_HEADS)]
    sv_rows = []
    for ci in range(tl // SG_CHUNK):
        sl = slice(ci * SG_CHUNK, (ci + 1) * SG_CHUNK)
        sv_rows.append(jnp.concatenate(
            [_dot(ws[hh], vdb[sl, hh * dv:(hh + 1) * dv]) for hh in range(N_HEADS)], axis=1) + sgb_ref[...])
    y_d = du * jnp.concatenate(sv_rows, axis=0)

    jobs = [lambda: terms.append(out_proj(y_a, 0)), lambda: terms.append(out_proj(y_d, 3))] + [pump] * len(pending)
    n_calls = (g // LANES) * SUBLANES
    progress = {"calls": 0, "done": 0}

    def between():
        progress["calls"] += 1
        while progress["done"] * n_calls < progress["calls"] * len(jobs):
            jobs[progress["done"]]()
            progress["done"] += 1

    conv_c = _conv_taps(ccw_ref, cbuf, zbuf, tl, c_head, between) + ccb_ref[...]
    assert progress["done"] == len(jobs)
    acc = x + terms[0] + terms[1] + terms[2] + out_proj(_silu(_layernorm(conv_c, lcg_ref[...], lcb_ref[...])), 2)
    o_ref[...] = acc
    hn_ref[...] = _rmsnorm(acc, gf_ref[...]).astype(BF16)

    @pl.when(t == last)
    def _():
        na_ref[...] = abuf[a_head + tl - (SC_WIDTH - 1):a_head + tl, :]
        nc_ref[...] = cbuf[c_head + tl - (CF_WIDTH - 1):c_head + tl, :]

    abuf[0:a_head, :] = abuf[tl:tl + a_head, :]
    cbuf[0:c_head, :] = cbuf[tl:tl + c_head, :]


def _full(shape):
    nd = len(shape)
    return pl.BlockSpec(shape, lambda *_: (0,) * nd)


def _resident(a, l=None):
    if l is None:
        return pl.BlockSpec(a.shape, lambda *_: (0,) * a.ndim, pipeline_mode=pl.Buffered(1))
    return pl.BlockSpec((None,) + a.shape[1:], lambda *_: (l,) + (0,) * (a.ndim - 1), pipeline_mode=pl.Buffered(1))


def _front(x, gm, w_head, w_tail, w_lr, w_o, gf, l, batch, seq, wts, tl):
    d = x.shape[1]
    g = wts["gn"].shape[1]
    nt = seq // tl
    row_map = lambda b, t: (b * nt + t, 0)
    names = ("caw", "a2", "ab", "gn", "ccw", "ccb", "lcg", "lcb", "ldg", "ldb", "sgw", "sgb", "tri", "e2")
    consts = [wts[n] for n in names]
    return pl.pallas_call(
        _front_kernel,
        grid=(batch, nt),
        in_specs=[pl.BlockSpec((tl, d), row_map), _full(gm.shape),
                  _resident(w_head, l), _resident(w_tail, l), _resident(w_lr, l)]
                 + [_full(c.shape) for c in consts] + [_resident(w_o), _full(gf.shape)],
        out_specs=[
            pl.BlockSpec((tl, d), row_map),
            pl.BlockSpec((tl, d), row_map),
            pl.BlockSpec((None, SC_WIDTH - 1, g), lambda b, t: (b, 0, 0)),
            pl.BlockSpec((None, g // 2, g // N_HEADS), lambda b, t: (b, 0, 0)),
            pl.BlockSpec((None, CF_WIDTH - 1, g), lambda b, t: (b, 0, 0)),
            pl.BlockSpec((None, SG_CHUNK, g), lambda b, t: (b, 0, 0)),
        ],
        out_shape=[
            jax.ShapeDtypeStruct(x.shape, F32),
            jax.ShapeDtypeStruct(x.shape, BF16),
            jax.ShapeDtypeStruct((batch, SC_WIDTH - 1, g), F32),
            jax.ShapeDtypeStruct((batch, g // 2, g // N_HEADS), F32),
            jax.ShapeDtypeStruct((batch, CF_WIDTH - 1, g), F32),
            jax.ShapeDtypeStruct((batch, SG_CHUNK, g), F32),
        ],
        scratch_shapes=[pltpu.VMEM((SUBLANES + tl, g), F32), pltpu.VMEM((4 * SUBLANES + tl, g), F32),
                        pltpu.VMEM((SUBLANES + tl, g), F32)],
        compiler_params=pltpu.CompilerParams(
            dimension_semantics=("arbitrary", "arbitrary"), vmem_limit_bytes=VMEM_LIMIT),
    )(x, gm, w_head, w_tail, w_lr, *consts, w_o, gf)


def _mixer_sample_kernel(z_ref, zlr_ref, pa_ref, s_ref, pc_ref, caw_ref, a2_ref, ab_ref, gn_ref, ccw_ref, ccb_ref,
                         lcg_ref, lcb_ref, ldg_ref, ldb_ref, sgw0_ref, sgb0_ref,
                         y_ref, na_ref, so_ref, nc_ref, sv_ref, o_scr, *, layer):
    if layer is not None:
        for ll in range(pa_ref.shape[0]):
            if ll != layer:
                na_ref[ll] = pa_ref[ll]
                so_ref[ll] = s_ref[ll]
                nc_ref[ll] = pc_ref[ll]
        pa_ref, s_ref, pc_ref = pa_ref.at[layer], s_ref.at[layer], pc_ref.at[layer]
        na_ref, so_ref, nc_ref = na_ref.at[layer], so_ref.at[layer], nc_ref.at[layer]
    bt = z_ref.shape[0]
    g = gn_ref.shape[1]
    qk = g // 2
    dk = qk // N_HEADS
    dv = g // N_HEADS
    cols, _ = _cols(g)

    def zc(name):
        lo, hi = cols[name]
        return z_ref[:, lo:hi]

    xa = zc("a_c") * zc("a_x")
    conv = caw_ref[SC_WIDTH - 1:SC_WIDTH, :] * xa
    for kk in range(SC_WIDTH - 1):
        conv = conv + caw_ref[kk:kk + 1, :] * pa_ref[:, kk, :]
    y_ref[:, 0:g] = (zc("a_b") * conv).astype(BF16)
    for kk in range(1, SC_WIDTH - 1):
        na_ref[:, kk - 1, :] = pa_ref[:, kk, :]
    na_ref[:, SC_WIDTH - 2, :] = xa

    la = _gla_gate(zlr_ref[...], a2_ref, ab_ref)
    a_t = jnp.exp(la).T
    k_t = zc("k").T
    q_rows = zc("q") * (float(dk) ** -0.5)
    v_rows = zc("v")
    hm = _head_masks(qk)
    hm_rows = jnp.concatenate(hm + [jnp.zeros_like(hm[0])] * (SUBLANES - N_HEADS), axis=0)
    for b in range(bt):
        a_col = jnp.broadcast_to(a_t[:, b:b + 1], (qk, dv))
        k_col = jnp.broadcast_to(k_t[:, b:b + 1], (qk, dv))
        v_b = jnp.concatenate(
            [jnp.broadcast_to(v_rows[b:b + 1, h * dv:(h + 1) * dv], (dk, dv)) for h in range(N_HEADS)], axis=0)
        s_new = a_col * s_ref[b] + k_col * v_b
        so_ref[b] = s_new
        q_m = (q_rows[b:b + 1] * hm_rows).astype(BF16)
        o_b = _dot(q_m, s_new.astype(BF16))
        for h in range(N_HEADS):
            o_scr[b:b + 1, h * dv:(h + 1) * dv] = o_b[h:h + 1]
    y_ref[:, g:2 * g] = _gla_out(o_scr[...], gn_ref[...], zc("r")).astype(BF16)

    cin = zc("c_val") * _sigmoid(zc("c_gate"))
    acc = ccb_ref[...] + ccw_ref[CF_WIDTH - 1:CF_WIDTH, :] * cin
    for kk in range(CF_WIDTH - 1):
        acc = acc + ccw_ref[kk:kk + 1, :] * pc_ref[:, kk, :]
    y_ref[:, 2 * g:3 * g] = _silu(_layernorm(acc, lcg_ref[...], lcb_ref[...])).astype(BF16)
    nc_ref[:, 0:CF_WIDTH - 2, :] = pc_ref[:, 1:CF_WIDTH - 1, :]
    nc_ref[:, CF_WIDTH - 2, :] = cin

    du = _gelu_tanh(zc("d_u"))
    vd = _layernorm(_gelu_tanh(zc("d_v")), ldg_ref[...], ldb_ref[...])
    sv_ref[...] = vd
    y_ref[:, 3 * g:4 * g] = (du * (sgw0_ref[...] * vd + sgb0_ref[...])).astype(BF16)


def _mixer_sample(z, zlr, pa, s, pc, wts, l, bt, in_place):
    nb, g = z.shape[0], wts["gn"].shape[1]
    names = ("caw", "a2", "ab", "gn", "ccw", "ccb", "lcg", "lcb", "ldg", "ldb", "sgw0", "sgb0")
    consts = [wts[n] for n in names]
    rows = lambda i: (i, 0)
    if in_place:
        state_spec = lambda a: pl.BlockSpec((None, bt) + a.shape[2:], lambda i: (l, i, 0, 0))
    else:
        state_spec = lambda a: pl.BlockSpec((a.shape[0], bt) + a.shape[2:], lambda i: (0, i, 0, 0))
    return pl.pallas_call(
        functools.partial(_mixer_sample_kernel, layer=None if in_place else l),
        grid=(nb // bt,),
        in_specs=[pl.BlockSpec((bt, z.shape[1]), rows), pl.BlockSpec((bt, LANES), rows),
                  state_spec(pa), state_spec(s), state_spec(pc)] + [_full(c.shape) for c in consts],
        out_specs=[pl.BlockSpec((bt, 4 * g), rows), state_spec(pa), state_spec(s), state_spec(pc),
                   pl.BlockSpec((bt, g), rows)],
        out_shape=[
            jax.ShapeDtypeStruct((nb, 4 * g), BF16),
            jax.ShapeDtypeStruct(pa.shape, F32),
            jax.ShapeDtypeStruct(s.shape, F32),
            jax.ShapeDtypeStruct(pc.shape, F32),
            jax.ShapeDtypeStruct((nb, g), F32),
        ],
        input_output_aliases={2: 1, 3: 2, 4: 3} if in_place else {},
        scratch_shapes=[pltpu.VMEM((bt, g), F32)],
        compiler_params=pltpu.CompilerParams(
            dimension_semantics=("arbitrary",), vmem_limit_bytes=VMEM_LIMIT),
    )(z, zlr, pa, s, pc, *consts)


def _largest_tile(m, cap, mult):
    t = min(m, cap)
    while m % t or t % mult:
        t -= mult
    return t


def _layer_weights(l, g, conv_a_w, gla_a2, gla_a_bias, gla_norm, conv_c_w, conv_c_b,
                   ln_c_g, ln_c_b, ln_d_g, ln_d_b, sg_w, sg_b):
    dv = g // N_HEADS
    row = lambda v: v.reshape(1, -1)
    return {
        "caw": conv_a_w[l],
        "a2": jnp.pad(gla_a2[l], ((0, LANES - GLA_LOWRANK), (0, 0))).astype(BF16),
        "ab": row(gla_a_bias[l]), "gn": row(gla_norm[l]),
        "ccw": conv_c_w[l], "ccb": row(conv_c_b[l]),
        "lcg": row(ln_c_g[l]), "lcb": row(ln_c_b[l]), "ldg": row(ln_d_g[l]), "ldb": row(ln_d_b[l]),
        "sgw": sg_w[l],
        "sgb": jnp.repeat(sg_b[l].T, dv, axis=1),
        "sgw0": row(jnp.repeat(sg_w[l][:, 0, 0], dv)),
        "sgb0": row(jnp.repeat(sg_b[l][:, 0], dv)),
    }


def _gla_constants(g, tl):
    qk, dv = g // 2, g // N_HEADS
    r = jnp.arange(tl)
    tri = ((r[:, None] >= r[None, :]) & (r[:, None] // GLA_CHUNK == r[None, :] // GLA_CHUNK)).astype(BF16)
    e2 = (jnp.arange(qk)[:, None] // (qk // N_HEADS) == jnp.arange(g)[None, :] // dv).astype(BF16)
    return tri, e2


def kernel(x_prompt, x_sample, state_conv_a, state_gla, state_conv_c, norm_mix, w_in, conv_a_w, gla_a2, gla_a_bias,
           gla_norm, conv_c_w, conv_c_b, ln_c_g, ln_c_b, ln_d_g, ln_d_b, sg_w, sg_b, w_o, norm_ffn, w_gate, w_up,
           w_down, norm_final):
    batch, seq, d = x_prompt.shape
    nb, dec_seq, _ = x_sample.shape
    depth = w_in.shape[0]
    g = d // 4
    qk, dv = g // 2, g // N_HEADS
    assert dec_seq == 1 and seq % SG_CHUNK == 0 and g % (N_HEADS * LANES) == 0
    dff = w_gate.shape[2]
    n_main = w_in.shape[2] - GLA_LOWRANK
    tl = _largest_tile(seq, FRONT_ROWS, SG_CHUNK)
    tri, e2 = _gla_constants(g, tl)

    lr0 = 3 * g + 2 * qk + 2 * g
    w_head = w_in[:, :, :lr0].astype(BF16)
    w_tail = w_in[:, :, lr0 + GLA_LOWRANK:].astype(BF16)
    w_lr = jnp.pad(w_in[:, :, lr0:lr0 + GLA_LOWRANK], ((0, 0), (0, 0), (0, LANES - GLA_LOWRANK))).astype(BF16)

    mp = batch * seq
    tm_p = _largest_tile(mp, FFN_ROWS, BF16_ROWS)
    tm_s = nb
    tn_in = _largest_tile(math.gcd(n_main, lr0), 4 * MXU_DIM, MXU_DIM)
    tf_cast = _largest_tile(dff, FFN_COLS_F32, MXU_DIM)
    tf = _largest_tile(dff, FFN_COLS_BF16, MXU_DIM)

    xp = x_prompt.reshape(mp, d)
    xs = x_sample.reshape(nb, d)
    st_a, st_c = state_conv_a, state_conv_c
    st_s = state_gla.reshape(depth, nb, qk, dv)
    outs_p = {k: [] for k in ("a", "s", "c", "v")}
    sv_s = []
    for l in range(depth):
        wts = _layer_weights(l, g, conv_a_w, gla_a2, gla_a_bias, gla_norm, conv_c_w, conv_c_b,
                             ln_c_g, ln_c_b, ln_d_g, ln_d_b, sg_w, sg_b)
        wts["tri"], wts["e2"] = tri, e2
        gm, gf = norm_mix[l].reshape(1, d), norm_ffn[l].reshape(1, d)
        gfin = norm_final.reshape(1, d)
        final = l == depth - 1

        z, zlr = _norm_proj(xs, gm, w_head, w_tail, w_lr, l, tm_s, tn_in)
        y, st_a, st_s, st_c, sv = _mixer_sample(z, zlr, st_a, st_s, st_c, wts, l,
                                                SAMPLE_SEQS if l else SAMPLE_SEQS // 2, l > 0)
        xs, hs, w_ob = _proj_res(y, w_o, xs, gf, l, tn_in)

        xp, hn, na, s_new, nc, svp = _front(xp, gm, w_head, w_tail, w_lr, w_ob, gf, l, batch, seq, wts, tl)
        xp, xs, wg_b, wu_b, wd_b = _ffn(hn, xp, w_gate, w_up, w_down, gfin, l, tm_p, tf_cast, 0, 1, final,
                                        rider=(hs, xs))
        if mp > tm_p:
            xp, = _ffn(hn, xp, wg_b, wu_b, wd_b, gfin, None, tm_p, tf, 1, mp // tm_p - 1, final)
        outs_p["a"].append(na)
        outs_p["s"].append(s_new.reshape(batch, N_HEADS, qk // N_HEADS, dv))
        outs_p["c"].append(nc)
        outs_p["v"].append(svp)

        sv_s.append(sv.reshape(nb, 1, g))

    st = jnp.stack
    return (xp.reshape(batch, seq, d), xs.reshape(nb, 1, d),
            st(outs_p["a"]), st_a, st(outs_p["s"]), st_s.reshape(depth, nb, N_HEADS, qk // N_HEADS, dv),
            st(outs_p["c"]), st_c, st(outs_p["v"]), st(sv_s))
```
